```python
import jax, jax.numpy as jnp
from jax import lax
import numpy as np

D_MODEL = 1024
BATCH = 16
SEQ = 2048
DEPTH = 2
DEC_BATCH = 128
DEC_SEQ = 4
PAST_LEN = 16384
PAGE_SIZE = 128

M_HEADS = 4
M_QK_DIM = 64
M_V_DIM = 128
M_WIDTH = M_HEADS * M_V_DIM
M_CHUNK = 128
FORGET_BIAS = 3.0
A_HEADS = 8
A_KV_HEADS = 2
A_GROUP = A_HEADS // A_KV_HEADS
A_HEAD_DIM = 64
A_WIDTH = A_HEADS * A_HEAD_DIM
WINDOW = 128
ROPE_THETA = 10000.0
D_FF = 2816
CONV_W = 3
EPS = 1e-6

SPLIT_SIZES = (M_HEADS * M_QK_DIM, M_HEADS * M_QK_DIM, M_WIDTH, M_WIDTH, M_HEADS, M_HEADS,
               A_WIDTH, A_KV_HEADS * A_HEAD_DIM, A_KV_HEADS * A_HEAD_DIM, 2 * D_MODEL)
D_IN_PROJ = sum(SPLIT_SIZES)

kernel_name = "hybrid_mlstm_swa_convffn_step"

F32 = jnp.float32


def rmsnorm(x, g):
    xf = x.astype(F32)
    r = lax.rsqrt(jnp.mean(xf * xf, axis=-1, keepdims=True) + EPS)
    return (xf * r * g.astype(F32)).astype(x.dtype)


def rope(x, pos):
    hd = x.shape[-1]
    inv = ROPE_THETA ** (-jnp.arange(0, hd, 2, dtype=F32) / hd)
    ang = pos.astype(F32)[:, None] * inv[None, :]
    ang = jnp.concatenate([ang, ang], axis=-1)
    shape = (1, pos.shape[0]) + (1,) * (x.ndim - 3) + (hd,)
    cos = jnp.cos(ang).reshape(shape)
    sin = jnp.sin(ang).reshape(shape)
    xf = x.astype(F32)
    x1, x2 = jnp.split(xf, 2, axis=-1)
    return (xf * cos + jnp.concatenate([-x2, x1], axis=-1) * sin).astype(x.dtype)


def split_proj(proj):
    points = np.cumsum(np.array(SPLIT_SIZES))[:-1].tolist()
    return jnp.split(proj, points, axis=-1)


def mlstm_chunk(carry, blk):
    C0, n0, m0 = carry
    q, k, v, li, lf = blk
    L = q.shape[-2]
    b = jnp.cumsum(lf, axis=-1)
    causal = jnp.tril(jnp.ones((L, L), dtype=bool))
    dmat = jnp.where(causal, b[..., :, None] - b[..., None, :] + li[..., None, :], -jnp.inf)
    inter = b + m0[..., None]
    m = jnp.maximum(inter, jnp.max(dmat, axis=-1))
    dexp = jnp.exp(dmat - m[..., None])
    iscale = jnp.exp(inter - m)
    qs = q * (M_QK_DIM ** -0.5)
    s = jnp.einsum('bhtd,bhsd->bhts', qs, k) * dexp
    num = iscale[..., None] * jnp.einsum('bhtd,bhdv->bhtv', qs, C0) + jnp.einsum('bhts,bhsv->bhtv', s, v)
    den = iscale * jnp.einsum('bhtd,bhd->bht', qs, n0) + jnp.sum(s, axis=-1)
    h = num / jnp.maximum(jnp.abs(den), jnp.exp(-m))[..., None]
    m_new = m[..., -1]
    w = jnp.exp(b[..., -1:] - b + li - m_new[..., None])
    decay = jnp.exp(b[..., -1] + m0 - m_new)
    C_new = decay[..., None, None] * C0 + jnp.einsum('bhs,bhsd,bhsv->bhdv', w, k, v)
    n_new = decay[..., None] * n0 + jnp.einsum('bhs,bhsd->bhd', w, k)
    return (C_new, n_new, m_new), h


def mlstm(q, k, v, li, lf, C0, n0, m0, chunk):
    B, S, H, _ = q.shape
    nb = S // chunk

    def blocks(a):
        a = a.astype(F32).reshape((B, nb, chunk) + a.shape[2:])
        if a.ndim == 5:
            return jnp.transpose(a, (1, 0, 3, 2, 4))
        return jnp.transpose(a, (1, 0, 3, 2))

    carry0 = (C0.astype(F32), n0.astype(F32), m0.astype(F32))
    (C, n, m), h = lax.scan(mlstm_chunk, carry0, (blocks(q), blocks(k), blocks(v), blocks(li), blocks(lf)))
    h = jnp.transpose(h, (1, 0, 3, 2, 4)).reshape(B, S, H, M_V_DIM)
    return h, C.astype(C0.dtype), n.astype(n0.dtype), m.astype(m0.dtype)


def swa_attend(q, k, v, buf_k, buf_v, buf_valid, sinks, block):
    B, S = q.shape[:2]
    nb = S // block
    ext_k = jnp.concatenate([buf_k.astype(k.dtype), k], axis=1)
    ext_v = jnp.concatenate([buf_v.astype(v.dtype), v], axis=1)
    idx = jnp.arange(nb)[:, None] * block + jnp.arange(WINDOW + block)[None, :]
    kb = ext_k[:, idx]
    vb = ext_v[:, idx]
    valid = jnp.concatenate([jnp.full((WINDOW,), buf_valid), jnp.ones((S,), dtype=bool)])[idx]
    i = jnp.arange(block)[:, None]
    j = jnp.arange(WINDOW + block)[None, :]
    band = (j >= i) & (j <= i + WINDOW)
    mask = band[None] & valid[:, None, :]
    qb = q.reshape(B, nb, block, A_KV_HEADS, A_GROUP, A_HEAD_DIM)
    scores = jnp.einsum('bnikgd,bnjkd->bnkgij', qb.astype(F32), kb.astype(F32)) * (A_HEAD_DIM ** -0.5)
    scores = jnp.where(mask[None, :, None, None], scores, -jnp.inf)
    sink = sinks.astype(F32)[None, None, :, :, None]
    mx = jnp.maximum(jnp.max(scores, axis=-1), sink)
    p = jnp.exp(scores - mx[..., None])
    denom = jnp.sum(p, axis=-1) + jnp.exp(sink - mx)
    out = jnp.einsum('bnkgij,bnjkd->bnikgd', p / denom[..., None], vb.astype(F32))
    return out.reshape(B, S, A_WIDTH).astype(q.dtype), ext_k[:, -WINDOW:], ext_v[:, -WINDOW:]


def conv_ffn(xn, buf, w_up, w_conv, b_conv, w_down):
    S = xn.shape[1]
    u = xn @ w_up
    ext = jnp.concatenate([buf.astype(u.dtype), u], axis=1)
    c = b_conv
    for tap in range(CONV_W):
        c = c + w_conv[tap] * ext[:, tap:tap + S]
    a, g = jnp.split(c, 2, axis=-1)
    h = jax.nn.gelu(a, approximate=False) * g
    return h @ w_down, ext[:, -(CONV_W - 1):]


def trunk(x, pos0, st_C, st_n, st_m, st_k, st_v, st_conv, buf_valid, m_chunk, a_block, p):
    B, S = x.shape[:2]
    pos = pos0 + jnp.arange(S, dtype=jnp.int32)
    new_C, new_n, new_m, new_k, new_v, new_conv = [], [], [], [], [], []
    for l in range(DEPTH):
        xn = rmsnorm(x, p['g_mix'][l])
        proj = xn @ p['w_in'][l]
        mq, mk, mv, mo, mi, mf, aq, ak, av, gt = split_proj(proj)
        li = mi.astype(F32) + p['b_igate'][l].astype(F32)
        lf = jax.nn.log_sigmoid(mf.astype(F32) + p['b_fgate'][l].astype(F32))
        h_m, C, n, m = mlstm(mq.reshape(B, S, M_HEADS, M_QK_DIM), mk.reshape(B, S, M_HEADS, M_QK_DIM),
                             mv.reshape(B, S, M_HEADS, M_V_DIM), li, lf,
                             st_C[l], st_n[l], st_m[l], m_chunk)
        h_m = h_m * lax.rsqrt(jnp.mean(h_m * h_m, axis=-1, keepdims=True) + EPS) * p['g_mhead'][l].astype(F32)
        h_m = (h_m.reshape(B, S, M_WIDTH) * jax.nn.sigmoid(mo.astype(F32))).astype(x.dtype)
        q = rope(aq.reshape(B, S, A_KV_HEADS, A_GROUP, A_HEAD_DIM), pos)
        k = rope(ak.reshape(B, S, A_KV_HEADS, A_HEAD_DIM), pos)
        v = av.reshape(B, S, A_KV_HEADS, A_HEAD_DIM)
        h_a, nk, nv = swa_attend(q, k, v, st_k[l], st_v[l], buf_valid, p['a_sinks'][l], a_block)
        ga, gb = jnp.split(jax.nn.sigmoid(gt), 2, axis=-1)
        mixed = ga * (h_m @ p['w_a'][l]) + gb * (h_a @ p['w_b'][l])
        x = x + mixed @ p['w_out'][l]
        y, cbuf = conv_ffn(rmsnorm(x, p['g_ffn'][l]), st_conv[l], p['w_up'][l], p['w_conv'][l],
                           p['b_conv'][l], p['w_down'][l])
        x = x + y
        new_C.append(C); new_n.append(n); new_m.append(m)
        new_k.append(nk); new_v.append(nv); new_conv.append(cbuf)
    x = rmsnorm(x, p['g_final'])
    return (x, jnp.stack(new_C), jnp.stack(new_n), jnp.stack(new_m), jnp.stack(new_k),
            jnp.stack(new_v), jnp.stack(new_conv))


def setup_inputs(seed: int = 0) -> dict:
    key = jax.random.key(seed)
    ks = jax.random.split(key, 24)
    nrm = jax.random.normal
    D = D_MODEL
    return {
        "x_prompt": nrm(ks[0], (BATCH, SEQ, D), F32),
        "x_sample": nrm(ks[1], (DEC_BATCH, DEC_SEQ, D), F32),
        "state_mlstm_C": 0.5 * nrm(ks[2], (DEPTH, DEC_BATCH, M_HEADS, M_QK_DIM, M_V_DIM), F32),
        "state_mlstm_n": 0.5 * nrm(ks[3], (DEPTH, DEC_BATCH, M_HEADS, M_QK_DIM), F32),
        "state_mlstm_m": nrm(ks[4], (DEPTH, DEC_BATCH, M_HEADS), F32),
        "cache_swa_k": nrm(ks[5], (DEPTH, DEC_BATCH, WINDOW, A_KV_HEADS, A_HEAD_DIM), F32),
        "cache_swa_v": nrm(ks[6], (DEPTH, DEC_BATCH, WINDOW, A_KV_HEADS, A_HEAD_DIM), F32),
        "state_ffn_conv": nrm(ks[7], (DEPTH, DEC_BATCH, CONV_W - 1, 2 * D_FF), F32),
        "g_mix": 1.0 + 0.05 * nrm(ks[8], (DEPTH, D), F32),
        "w_in": nrm(ks[9], (DEPTH, D, D_IN_PROJ), F32) * D ** -0.5,
        "b_igate": 0.1 * nrm(ks[10], (DEPTH, M_HEADS), F32),
        "b_fgate": FORGET_BIAS + 0.5 * nrm(ks[11], (DEPTH, M_HEADS), F32),
        "g_mhead": 1.0 + 0.05 * nrm(ks[12], (DEPTH, M_HEADS, M_V_DIM), F32),
        "a_sinks": 0.5 * nrm(ks[13], (DEPTH, A_KV_HEADS, A_GROUP), F32),
        "w_a": nrm(ks[14], (DEPTH, M_WIDTH, D), F32) * M_WIDTH ** -0.5,
        "w_b": nrm(ks[15], (DEPTH, A_WIDTH, D), F32) * A_WIDTH ** -0.5,
        "w_out": nrm(ks[16], (DEPTH, D, D), F32) * D ** -0.5,
        "g_ffn": 1.0 + 0.05 * nrm(ks[17], (DEPTH, D), F32),
        "w_up": nrm(ks[18], (DEPTH, D, 2 * D_FF), F32) * D ** -0.5,
        "w_conv": nrm(ks[19], (DEPTH, CONV_W, 2 * D_FF), F32) * CONV_W ** -0.5,
        "b_conv": 0.02 * nrm(ks[20], (DEPTH, 2 * D_FF), F32),
        "w_down": nrm(ks[21], (DEPTH, D_FF, D), F32) * D_FF ** -0.5,
        "g_final": 1.0 + 0.05 * nrm(ks[22], (D,), F32),
    }


def reference(x_prompt, x_sample, state_mlstm_C, state_mlstm_n, state_mlstm_m, cache_swa_k, cache_swa_v,
              state_ffn_conv, g_mix, w_in, b_igate, b_fgate, g_mhead, a_sinks, w_a, w_b, w_out,
              g_ffn, w_up, w_conv, b_conv, w_down, g_final):
    params = dict(g_mix=g_mix, w_in=w_in, b_igate=b_igate, b_fgate=b_fgate, g_mhead=g_mhead,
                  a_sinks=a_sinks, w_a=w_a, w_b=w_b, w_out=w_out, g_ffn=g_ffn, w_up=w_up,
                  w_conv=w_conv, b_conv=b_conv, w_down=w_down, g_final=g_final)
    B = x_prompt.shape[0]
    z_C = jnp.zeros((DEPTH, B, M_HEADS, M_QK_DIM, M_V_DIM), state_mlstm_C.dtype)
    z_n = jnp.zeros((DEPTH, B, M_HEADS, M_QK_DIM), state_mlstm_n.dtype)
    z_m = jnp.zeros((DEPTH, B, M_HEADS), state_mlstm_m.dtype)
    z_k = jnp.zeros((DEPTH, B, WINDOW, A_KV_HEADS, A_HEAD_DIM), x_prompt.dtype)
    z_conv = jnp.zeros((DEPTH, B, CONV_W - 1, 2 * D_FF), x_prompt.dtype)
    y_prompt, p_C, p_n, p_m, p_k, p_v, p_conv = trunk(
        x_prompt, 0, z_C, z_n, z_m, z_k, z_k, z_conv, False, M_CHUNK, WINDOW, params)
    t = x_sample.shape[1]
    y_sample, s_C, s_n, s_m, s_k, s_v, s_conv = trunk(
        x_sample, PAST_LEN, state_mlstm_C, state_mlstm_n, state_mlstm_m, cache_swa_k, cache_swa_v,
        state_ffn_conv, True, t, t, params)
    return (y_prompt, y_sample, p_C, p_n, p_m, p_k, p_v, p_conv, s_C, s_n, s_m, s_k, s_v, s_conv)
```

```python
import functools

import numpy as np
import jax
import jax.numpy as jnp
from jax import lax
from jax.experimental import pallas as pl
from jax.experimental.pallas import tpu as pltpu

F32 = jnp.float32
BF16 = jnp.bfloat16

D_MODEL = 1024
DEPTH = 2
PAST_LEN = 16384
M_HEADS = 4
M_QK_DIM = 64
M_V_DIM = 128
M_WIDTH = M_HEADS * M_V_DIM
M_CHUNK = 128
A_HEADS = 8
A_KV_HEADS = 2
A_GROUP = A_HEADS // A_KV_HEADS
A_HEAD_DIM = 64
A_WIDTH = A_HEADS * A_HEAD_DIM
WINDOW = 128
ROPE_THETA = 10000.0
D_FF = 2816
CONV_W = 3
EPS = 1e-6

QK_W = M_HEADS * M_QK_DIM
KV_W = A_KV_HEADS * A_HEAD_DIM
GATE_W = 2 * D_MODEL
_SPLIT = (QK_W, QK_W, M_WIDTH, M_WIDTH, M_HEADS, M_HEADS, A_WIDTH, KV_W, KV_W, GATE_W)
_OFF = np.concatenate([[0], np.cumsum(_SPLIT)]).tolist()
_MAIN_W = (QK_W, QK_W, M_WIDTH, M_WIDTH, A_WIDTH, KV_W, KV_W, GATE_W)
_MAIN_OFF = np.concatenate([[0], np.cumsum(_MAIN_W)]).tolist()
MAIN_COLS = _MAIN_OFF[-1]

LANES_V7X = 128
SUBLANES_V7X = 8
VMEM_BYTES_V7X = 64 * 1024 * 1024
VMEM_LIMIT_CAP_V7X = 60000 * 1024

SAMPLE_PAD_T = SUBLANES_V7X
FF_CHUNK = 2 * LANES_V7X
N_FF_CHUNKS = D_FF // FF_CHUNK
ROW_TILE = 512
SAMPLE_FFN_GROUP = 32


def _vmem_limit(block_bytes, scratch_bytes, temp_bytes):
    need = 2 * block_bytes + scratch_bytes + temp_bytes
    return int(min(max(need, 16 * 1024 * 1024), VMEM_LIMIT_CAP_V7X))


def _params(vmem_bytes, n_axes):
    return pltpu.CompilerParams(dimension_semantics=("arbitrary",) * n_axes,
                                vmem_limit_bytes=vmem_bytes)


def _resident(shape):
    nd = len(shape)
    return pl.BlockSpec(shape, lambda *_: (0,) * nd, pipeline_mode=pl.Buffered(1))


def _log_sigmoid(x):
    return jnp.minimum(x, 0.0) - jnp.log1p(jnp.exp(-jnp.abs(x)))


def _inproj_kernel(x_ref, g_ref, w_ref, wg_ref, wgt_ref, cos_ref, sin_ref,
                   mq_ref, mk_ref, mv_ref, mo_ref, aq_ref, ak_ref, av_ref, gt_ref, gc_ref, gr_ref):
    x = x_ref[...]
    r = lax.rsqrt(jnp.mean(x * x, axis=-1, keepdims=True) + EPS)
    xn = (x * r * g_ref[...]).astype(BF16)

    def proj(idx):
        return jnp.dot(xn, w_ref[:, _MAIN_OFF[idx]:_MAIN_OFF[idx + 1]], preferred_element_type=F32)

    mq_ref[...] = proj(0) * (M_QK_DIM ** -0.5)
    mk_ref[...] = proj(1)
    mv_ref[...] = proj(2)
    mo_ref[...] = proj(3)
    av_ref[...] = proj(6)
    gt_ref[...] = proj(7)

    cos = cos_ref[...]
    sin = sin_ref[...]
    rows = x.shape[0]
    lane = lax.broadcasted_iota(jnp.int32, (rows, LANES_V7X), 1)
    first_half = (lane % A_HEAD_DIM) < (A_HEAD_DIM // 2)

    def rope(v):
        rot = jnp.where(first_half,
                        pltpu.roll(v, LANES_V7X - A_HEAD_DIM // 2, 1),
                        pltpu.roll(v, A_HEAD_DIM // 2, 1))
        return v * cos + rot * sin

    aq = proj(4)
    for c in range(A_WIDTH // LANES_V7X):
        sl = slice(c * LANES_V7X, (c + 1) * LANES_V7X)
        aq_ref[:, sl] = rope(aq[:, sl]) * (A_HEAD_DIM ** -0.5)
    ak_ref[...] = rope(proj(5))

    gc_ref[...] = jnp.dot(xn, wg_ref[...], preferred_element_type=F32)
    grt = lax.dot_general(wgt_ref[...], xn, (((1,), (1,)), ((), ())), preferred_element_type=F32)
    gr_ref[...] = grt[:SUBLANES_V7X]


def _inproj(x2d, g, w_main, w_gate, w_gate_t, cos, sin, tiles_per_seq):
    n = x2d.shape[0]
    t = min(ROW_TILE, n)
    grid = (n // t,)
    row = lambda w: pl.BlockSpec((t, w), lambda i: (i, 0))
    tab = pl.BlockSpec((t, LANES_V7X), lambda i: (i % tiles_per_seq, 0))
    out_shapes = [jax.ShapeDtypeStruct((n, w), F32) for w in _MAIN_W]
    out_shapes += [jax.ShapeDtypeStruct((n, LANES_V7X), F32), jax.ShapeDtypeStruct((SUBLANES_V7X, n), F32)]
    out_specs = [row(w) for w in _MAIN_W]
    out_specs += [row(LANES_V7X), pl.BlockSpec((SUBLANES_V7X, t), lambda i: (0, i))]
    blk = 4 * t * (D_MODEL + MAIN_COLS + 3 * LANES_V7X + SUBLANES_V7X)
    res = 2 * D_MODEL * (MAIN_COLS + LANES_V7X + 2 * SUBLANES_V7X)
    tmp = 4 * t * (GATE_W + D_MODEL)
    return pl.pallas_call(
        _inproj_kernel,
        grid=grid,
        in_specs=[row(D_MODEL), _resident((1, D_MODEL)), _resident((D_MODEL, MAIN_COLS)),
                  _resident((D_MODEL, LANES_V7X)), _resident((2 * SUBLANES_V7X, D_MODEL)), tab, tab],
        out_specs=out_specs,
        out_shape=out_shapes,
        compiler_params=_params(_vmem_limit(blk, res, tmp), 1),
        name="inproj",
    )(x2d, g, w_main, w_gate, w_gate_t, cos, sin)


def _mlstm_kernel(q_ref, k_ref, v_ref, gc_ref, gr_ref, brow_ref, bcol_ref, c0_ref, n0_ref, m0_ref,
                  h_ref, c_ref, n_ref, m_ref, *, chunk, real, bt, mm_dtype):
    L = chunk
    ci = pl.program_id(1)

    @pl.when(ci == 0)
    def _():
        c_ref[...] = c0_ref[...]
        n_ref[...] = n0_ref[...]
        m_ref[...] = m0_ref[...]

    def mx(a):
        return a.astype(mm_dtype)

    def split3(a):
        hi = a.astype(BF16).astype(F32)
        r1 = a - hi
        mid = r1.astype(BF16).astype(F32)
        lo = (r1 - mid).astype(BF16).astype(F32)
        return hi, mid, lo

    ti = lax.broadcasted_iota(jnp.int32, (L, L), 0)
    si = lax.broadcasted_iota(jnp.int32, (L, L), 1)
    causal = si <= ti
    tri = causal.astype(F32)
    tri_t = (ti <= si).astype(F32)
    lane_g = lax.broadcasted_iota(jnp.int32, (L, LANES_V7X), 1)
    row_g = lax.broadcasted_iota(jnp.int32, (L, LANES_V7X), 0)
    is_f_col = (lane_g >= M_HEADS) & (lane_g < 2 * M_HEADS)
    row_r = lax.broadcasted_iota(jnp.int32, (SUBLANES_V7X, L), 0)
    col_r = lax.broadcasted_iota(jnp.int32, (SUBLANES_V7X, L), 1)
    is_f_row = row_r >= M_HEADS
    lane_p = lax.broadcasted_iota(jnp.int32, (L, LANES_V7X), 1)
    half_masks = (lane_p < M_QK_DIM, lane_p >= M_QK_DIM)
    rows_c = lax.broadcasted_iota(jnp.int32, (2 * M_QK_DIM, 1), 0)
    lanes_n = lax.broadcasted_iota(jnp.int32, (1, LANES_V7X), 1)
    neg_inf = jnp.float32(-jnp.inf)

    for bi in range(bt):
        rs = slice(bi * L, (bi + 1) * L)
        gcol = gc_ref[rs, :] + brow_ref[...]
        gcol = jnp.where(is_f_col, _log_sigmoid(gcol), gcol)
        fcol = jnp.where(is_f_col, gcol, 0.0)
        grow = gr_ref[:, rs] + bcol_ref[:, :L]
        grow = jnp.where(is_f_row, _log_sigmoid(grow), grow)
        frow = jnp.where(is_f_row, grow, 0.0)
        if real < L:
            gcol = jnp.where(row_g < real, gcol, neg_inf)
            fcol = jnp.where(row_g < real, fcol, 0.0)
            grow = jnp.where(col_r < real, grow, neg_inf)
            frow = jnp.where(col_r < real, frow, 0.0)
        cum_c = sum(jnp.dot(tri, p, preferred_element_type=F32) for p in split3(fcol))
        cum_r = sum(jnp.dot(p, tri_t, preferred_element_type=F32) for p in split3(frow))

        new_m = []
        for pair in range(M_HEADS // 2):
            ps = slice(pair * LANES_V7X, (pair + 1) * LANES_V7X)
            qp = q_ref[rs, ps]
            kp = k_ref[rs, ps]
            cp = c_ref[bi, pair]
            npair = n_ref[bi, pair:pair + 1, :]
            d_c = []
            d_n = []
            decays = []
            for half in range(2):
                h = 2 * pair + half
                hm = half_masks[half]
                b_col = cum_c[:, M_HEADS + h:M_HEADS + h + 1]
                li_col = gcol[:, h:h + 1]
                b_row = cum_r[M_HEADS + h:M_HEADS + h + 1, :]
                li_row = grow[h:h + 1, :]
                m0 = m_ref[bi, h:h + 1, 0:1]
                vh = v_ref[rs, h * M_V_DIM:(h + 1) * M_V_DIM]

                dmat = jnp.where(causal, b_col - b_row + li_row, neg_inf)
                inter = b_col + m0
                mrow = jnp.maximum(inter, jnp.max(dmat, axis=-1, keepdims=True))
                dexp = jnp.exp(dmat - mrow)
                iscale = jnp.exp(inter - mrow)
                qh = jnp.where(hm, qp, 0.0)
                s = lax.dot_general(mx(qh), mx(kp), (((1,), (1,)), ((), ())),
                                    preferred_element_type=F32) * dexp
                num = iscale * jnp.dot(mx(qh), mx(cp), preferred_element_type=F32) \
                    + jnp.dot(mx(s), mx(vh), preferred_element_type=F32)
                qn = jnp.sum(qh * npair, axis=-1, keepdims=True)
                den = iscale * qn + jnp.sum(s, axis=-1, keepdims=True)
                h_ref[rs, h * M_V_DIM:(h + 1) * M_V_DIM] = \
                    num / jnp.maximum(jnp.abs(den), jnp.exp(-mrow))

                m_new = mrow[L - 1:L, :]
                b_last = b_col[L - 1:L, :]
                w_col = jnp.exp(b_last - b_col + li_col - m_new)
                decays.append(jnp.exp(b_last + m0 - m_new))
                kw = jnp.where(hm, kp * w_col, 0.0)
                d_c.append(lax.dot_general(mx(kw), mx(vh), (((0,), (0,)), ((), ())),
                                           preferred_element_type=F32))
                d_n.append(jnp.sum(kw, axis=0, keepdims=True))
                new_m.append(m_new)
            c_ref[bi, pair] = jnp.where(rows_c < M_QK_DIM, decays[0], decays[1]) * cp + d_c[0] + d_c[1]
            n_ref[bi, pair:pair + 1, :] = \
                jnp.where(lanes_n < M_QK_DIM, decays[0], decays[1]) * npair + d_n[0] + d_n[1]
        for h in range(M_HEADS):
            m_ref[bi, h:h + 1, :] = jnp.broadcast_to(new_m[h], (1, LANES_V7X))


def _mlstm(q, k, v, gc, gr, brow, bcol, c0, n0, m0, *, nseq, nchunks, chunk, real, bt, mm_dtype):
    rows = bt * chunk
    grid = (nseq // bt, nchunks)
    rmap = lambda b, c: (b * nchunks + c, 0)
    smap3 = lambda b, c: (b, 0, 0)
    smap4 = lambda b, c: (b, 0, 0, 0)
    c_spec = pl.BlockSpec((bt, 2, LANES_V7X, LANES_V7X), smap4)
    n_spec = pl.BlockSpec((bt, 2, LANES_V7X), smap3)
    m_spec = pl.BlockSpec((bt, SUBLANES_V7X, LANES_V7X), smap3)
    n = q.shape[0]
    blk = 4 * (rows * (2 * QK_W + 2 * M_WIDTH + LANES_V7X + SUBLANES_V7X)
               + 2 * bt * (2 * LANES_V7X * LANES_V7X + 2 * LANES_V7X + SUBLANES_V7X * LANES_V7X))
    tmp = 4 * 16 * chunk * max(chunk, LANES_V7X)
    kern = functools.partial(_mlstm_kernel, chunk=chunk, real=real, bt=bt, mm_dtype=mm_dtype)
    return pl.pallas_call(
        kern,
        grid=grid,
        in_specs=[pl.BlockSpec((rows, QK_W), rmap), pl.BlockSpec((rows, QK_W), rmap),
                  pl.BlockSpec((rows, M_WIDTH), rmap), pl.BlockSpec((rows, LANES_V7X), rmap),
                  pl.BlockSpec((SUBLANES_V7X, rows), lambda b, c: (0, b * nchunks + c)),
                  _resident((1, LANES_V7X)), _resident((SUBLANES_V7X, LANES_V7X)),
                  c_spec, n_spec, m_spec],
        out_specs=[pl.BlockSpec((rows, M_WIDTH), rmap), c_spec, n_spec, m_spec],
        out_shape=[jax.ShapeDtypeStruct((n, M_WIDTH), F32),
                   jax.ShapeDtypeStruct((nseq, 2, LANES_V7X, LANES_V7X), F32),
                   jax.ShapeDtypeStruct((nseq, 2, LANES_V7X), F32),
                   jax.ShapeDtypeStruct((nseq, SUBLANES_V7X, LANES_V7X), F32)],
        compiler_params=_params(_vmem_limit(blk, 0, tmp), 2),
        name="mlstm",
    )(q, k, v, gc, gr, brow, bcol, c0, n0, m0)


def _kv_halves(a, dt):
    lane = lax.broadcasted_iota(jnp.int32, a.shape, 1)
    low = lane < A_HEAD_DIM
    a_sw = pltpu.roll(a, A_HEAD_DIM, 1)
    zero = jnp.zeros_like(a)
    return ((jnp.where(low, a, zero).astype(dt), jnp.where(low, zero, a_sw).astype(dt)),
            (jnp.where(low, a_sw, zero).astype(dt), jnp.where(low, zero, a).astype(dt)))


def _attend_pair(q_pair, keys, vals, masks, sinks, dt):
    out = None
    nt = (((1,), (1,)), ((), ()))
    qb = q_pair.astype(dt)
    for half in range(2):
        scores = [jnp.where(m, lax.dot_general(qb, kseg[half], nt, preferred_element_type=F32), -jnp.inf)
                  for kseg, m in zip(keys, masks)]
        mxv = sinks[half]
        for sc in scores:
            mxv = jnp.maximum(mxv, jnp.max(sc, axis=-1, keepdims=True))
        den = jnp.exp(sinks[half] - mxv)
        pv = None
        for sc, vseg in zip(scores, vals):
            p = jnp.exp(sc - mxv)
            den = den + jnp.sum(p, axis=-1, keepdims=True)
            t = jnp.dot(p.astype(dt), vseg[half], preferred_element_type=F32)
            pv = t if pv is None else pv + t
        pv = pv / den
        out = pv if out is None else out + pv
    return out


def _swa_prompt_kernel(q_ref, kc_ref, kp_ref, vc_ref, vp_ref, sink_ref, o_ref):
    blk = pl.program_id(1)
    L = WINDOW
    r = lax.broadcasted_iota(jnp.int32, (L, L), 0)
    c = lax.broadcasted_iota(jnp.int32, (L, L), 1)
    prev_off = jnp.where(blk > 0, 0, L)
    masks = [c >= r + prev_off, c <= r]
    ksel = [_kv_halves(kp_ref[...], BF16), _kv_halves(kc_ref[...], BF16)]
    vsel = [_kv_halves(vp_ref[...], BF16), _kv_halves(vc_ref[...], BF16)]
    for kv in range(A_KV_HEADS):
        for pair in range(A_GROUP // 2):
            col = (kv * A_GROUP // 2 + pair) * LANES_V7X
            hd = kv * A_GROUP + 2 * pair
            sinks = [sink_ref[hd + half:hd + half + 1, 0:1] for half in range(2)]
            o_ref[:, col:col + LANES_V7X] = _attend_pair(
                q_ref[:, col:col + LANES_V7X], [ks[kv] for ks in ksel], [vs[kv] for vs in vsel],
                masks, sinks, BF16)


def _swa_prompt(q, k, v, sinks, *, nseq, nblk):
    n = q.shape[0]
    L = WINDOW
    cur = lambda b, i: (b * nblk + i, 0)
    prev = lambda b, i: (b * nblk + jnp.maximum(i - 1, 0), 0)
    blk = 4 * L * (2 * A_WIDTH + 4 * KV_W)
    tmp = 4 * 24 * L * L
    return pl.pallas_call(
        _swa_prompt_kernel,
        grid=(nseq, nblk),
        in_specs=[pl.BlockSpec((L, A_WIDTH), cur),
                  pl.BlockSpec((L, KV_W), cur), pl.BlockSpec((L, KV_W), prev),
                  pl.BlockSpec((L, KV_W), cur), pl.BlockSpec((L, KV_W), prev),
                  _resident((A_HEADS, LANES_V7X))],
        out_specs=pl.BlockSpec((L, A_WIDTH), cur),
        out_shape=jax.ShapeDtypeStruct((n, A_WIDTH), F32),
        compiler_params=_params(_vmem_limit(blk, 0, tmp), 2),
        name="swa_prompt",
    )(q, k, k, v, v, sinks)


def _swa_sample_kernel(q_ref, kn_ref, vn_ref, kc_ref, vc_ref, sink_ref, o_ref, ko_ref, vo_ref,
                       *, bt, real):
    P = SAMPLE_PAD_T
    W = WINDOW
    r_c = lax.broadcasted_iota(jnp.int32, (P, W), 0)
    c_c = lax.broadcasted_iota(jnp.int32, (P, W), 1)
    r_n = lax.broadcasted_iota(jnp.int32, (P, P), 0)
    c_n = lax.broadcasted_iota(jnp.int32, (P, P), 1)
    masks = [c_c >= r_c, (c_n <= r_n) & (c_n < real)]
    for bi in range(bt):
        rs = slice(bi * P, (bi + 1) * P)
        kc = kc_ref[bi]
        vc = vc_ref[bi]
        kn = kn_ref[rs, :]
        vn = vn_ref[rs, :]
        ksel = [_kv_halves(kc, F32), _kv_halves(kn, F32)]
        vsel = [_kv_halves(vc, F32), _kv_halves(vn, F32)]
        for kv in range(A_KV_HEADS):
            for pair in range(A_GROUP // 2):
                col = (kv * A_GROUP // 2 + pair) * LANES_V7X
                hd = kv * A_GROUP + 2 * pair
                sinks = [sink_ref[hd + half:hd + half + 1, 0:1] for half in range(2)]
                o_ref[rs, col:col + LANES_V7X] = _attend_pair(
                    q_ref[rs, col:col + LANES_V7X], [ks[kv] for ks in ksel],
                    [vs[kv] for vs in vsel], masks, sinks, F32)
        ko_ref[bi, 0:W - real, :] = kc_ref[bi, real:W, :]
        ko_ref[bi, W - real:W, :] = kn[0:real, :]
        vo_ref[bi, 0:W - real, :] = vc_ref[bi, real:W, :]
        vo_ref[bi, W - real:W, :] = vn[0:real, :]


def _swa_sample(q, kn, vn, kcache, vcache, sinks, *, layer, nseq, bt, real):
    n = q.shape[0]
    rows = bt * SAMPLE_PAD_T
    nb = nseq // bt
    rmap = lambda i: (i, 0)
    cmap = lambda i: (layer * nb + i, 0, 0)
    omap = lambda i: (i, 0, 0)
    cache = pl.BlockSpec((bt, WINDOW, KV_W), cmap)
    blk = 4 * (rows * (2 * A_WIDTH + 2 * KV_W) + 4 * bt * WINDOW * KV_W)
    tmp = 4 * 64 * WINDOW * LANES_V7X
    kern = functools.partial(_swa_sample_kernel, bt=bt, real=real)
    return pl.pallas_call(
        kern,
        grid=(nb,),
        in_specs=[pl.BlockSpec((rows, A_WIDTH), rmap), pl.BlockSpec((rows, KV_W), rmap),
                  pl.BlockSpec((rows, KV_W), rmap), cache, cache, _resident((A_HEADS, LANES_V7X))],
        out_specs=[pl.BlockSpec((rows, A_WIDTH), rmap),
                   pl.BlockSpec((bt, WINDOW, KV_W), omap), pl.BlockSpec((bt, WINDOW, KV_W), omap)],
        out_shape=[jax.ShapeDtypeStruct((n, A_WIDTH), F32),
                   jax.ShapeDtypeStruct((nseq, WINDOW, KV_W), F32),
                   jax.ShapeDtypeStruct((nseq, WINDOW, KV_W), F32)],
        compiler_params=_params(_vmem_limit(blk, 0, tmp), 1),
        name="swa_sample",
    )(q, kn, vn, kcache, vcache, sinks)


def _merge_kernel(hm_ref, mo_ref, ha_ref, gt_ref, x_ref, gmh_ref, wa_ref, wb_ref, wo_ref, o_ref):
    hm = hm_ref[...]
    parts = []
    for h in range(M_HEADS):
        v = hm[:, h * M_V_DIM:(h + 1) * M_V_DIM]
        parts.append(v * lax.rsqrt(jnp.mean(v * v, axis=-1, keepdims=True) + EPS))
    hn = jnp.concatenate(parts, axis=1) * gmh_ref[...] * jax.nn.sigmoid(mo_ref[...])
    a = jnp.dot(hn.astype(BF16), wa_ref[...], preferred_element_type=F32)
    b = jnp.dot(ha_ref[...].astype(BF16), wb_ref[...], preferred_element_type=F32)
    gates = jax.nn.sigmoid(gt_ref[...])
    mixed = gates[:, :D_MODEL] * a + gates[:, D_MODEL:] * b
    o_ref[...] = x_ref[...] + jnp.dot(mixed.astype(BF16), wo_ref[...], preferred_element_type=F32)


def _merge(hm, mo, ha, gt, x2d, gmh, wa, wb, wo):
    n = x2d.shape[0]
    t = min(ROW_TILE, n)
    row = lambda w: pl.BlockSpec((t, w), lambda i: (i, 0))
    blk = 4 * t * (3 * M_WIDTH + GATE_W + 2 * D_MODEL)
    res = 2 * (2 * M_WIDTH * D_MODEL + D_MODEL * D_MODEL)
    tmp = 4 * t * (GATE_W + 3 * D_MODEL)
    return pl.pallas_call(
        _merge_kernel,
        grid=(n // t,),
        in_specs=[row(M_WIDTH), row(M_WIDTH), row(A_WIDTH), row(GATE_W), row(D_MODEL),
                  _resident((1, M_WIDTH)), _resident((M_WIDTH, D_MODEL)),
                  _resident((A_WIDTH, D_MODEL)), _resident((D_MODEL, D_MODEL))],
        out_specs=row(D_MODEL),
        out_shape=jax.ShapeDtypeStruct((n, D_MODEL), F32),
        compiler_params=_params(_vmem_limit(blk, res, tmp), 1),
        name="merge",
    )(hm, mo, ha, gt, x2d, gmh, wa, wb, wo)


def _ffn_kernel(x_ref, gn_ref, halo_ref, wup_ref, wc_ref, wdn_ref, gfin_ref, o_ref, halo_out_ref,
                xn_s, u_s, acc_s, halo_s, *, g, final_norm):
    t = x_ref.shape[0]
    H = u_s.shape[0] - t
    ti = pl.program_id(1)
    last = pl.num_programs(1) - 1

    @pl.when(ti == 0)
    def _():
        for j in range(N_FF_CHUNKS):
            halo_s[j, :, :FF_CHUNK] = halo_ref[0, :, j * FF_CHUNK:(j + 1) * FF_CHUNK]
            halo_s[j, :, FF_CHUNK:] = halo_ref[0, :, D_FF + j * FF_CHUNK:D_FF + (j + 1) * FF_CHUNK]

    x = x_ref[...]
    xn_s[...] = (x * lax.rsqrt(jnp.mean(x * x, axis=-1, keepdims=True) + EPS) * gn_ref[...]).astype(BF16)
    acc_s[...] = jnp.zeros_like(acc_s)

    def body(j, carry):
        u = jnp.dot(xn_s[...], wup_ref[j], preferred_element_type=F32)
        u_s[H - 2 * g:H, :] = halo_s[j]
        u_s[H:H + t, :] = u
        u1 = u_s[H - g:H - g + t, :]
        u2 = u_s[H - 2 * g:H - 2 * g + t, :]
        halo_s[j] = u_s[H + t - 2 * g:H + t, :]
        wc = wc_ref[j]
        c = wc[3:4, :] + wc[0:1, :] * u2 + wc[1:2, :] * u1 + wc[2:3, :] * u
        a = c[:, :FF_CHUNK]
        act = 0.5 * a * (1.0 + lax.erf(a * np.float32(np.sqrt(0.5)))) * c[:, FF_CHUNK:]
        acc_s[...] += jnp.dot(act.astype(BF16), wdn_ref[j], preferred_element_type=F32)
        return carry

    lax.fori_loop(0, N_FF_CHUNKS, body, 0)
    y = x + acc_s[...]
    if final_norm:
        y = y * lax.rsqrt(jnp.mean(y * y, axis=-1, keepdims=True) + EPS) * gfin_ref[...]
    o_ref[...] = y

    @pl.when(ti == last)
    def _():
        for j in range(N_FF_CHUNKS):
            halo_out_ref[0, :, j * FF_CHUNK:(j + 1) * FF_CHUNK] = halo_s[j, :, :FF_CHUNK]
            halo_out_ref[0, :, D_FF + j * FF_CHUNK:D_FF + (j + 1) * FF_CHUNK] = halo_s[j, :, FF_CHUNK:]


def _ffn(x2d, gn, halo, wup, wc, wdn, gfin, *, nseq, t, g, final_norm):
    n = x2d.shape[0]
    tiles = n // (nseq * t)
    H = -(-2 * g // SUBLANES_V7X) * SUBLANES_V7X
    rmap = lambda s, i: (s * tiles + i, 0)
    hmap = lambda s, i: (s, 0, 0)
    blk = 4 * (2 * t * D_MODEL + 2 * 2 * g * 2 * D_FF)
    res = 2 * (D_MODEL * 2 * D_FF + D_FF * D_MODEL) + 4 * N_FF_CHUNKS * SUBLANES_V7X * 2 * FF_CHUNK
    scr = 2 * t * D_MODEL + 4 * (H + t) * 2 * FF_CHUNK + 4 * t * D_MODEL \
        + 4 * N_FF_CHUNKS * 2 * g * 2 * FF_CHUNK
    tmp = 4 * t * (6 * FF_CHUNK + D_MODEL)
    kern = functools.partial(_ffn_kernel, g=g, final_norm=final_norm)
    return pl.pallas_call(
        kern,
        grid=(nseq, tiles),
        in_specs=[pl.BlockSpec((t, D_MODEL), rmap), _resident((1, D_MODEL)),
                  pl.BlockSpec((1, 2 * g, 2 * D_FF), hmap),
                  _resident((N_FF_CHUNKS, D_MODEL, 2 * FF_CHUNK)),
                  _resident((N_FF_CHUNKS, SUBLANES_V7X, 2 * FF_CHUNK)),
                  _resident((N_FF_CHUNKS, FF_CHUNK, D_MODEL)), _resident((1, D_MODEL))],
        out_specs=[pl.BlockSpec((t, D_MODEL), rmap), pl.BlockSpec((1, 2 * g, 2 * D_FF), hmap)],
        out_shape=[jax.ShapeDtypeStruct((n, D_MODEL), F32),
                   jax.ShapeDtypeStruct((nseq, 2 * g, 2 * D_FF), F32)],
        scratch_shapes=[pltpu.VMEM((t, D_MODEL), BF16), pltpu.VMEM((H + t, 2 * FF_CHUNK), F32),
                        pltpu.VMEM((t, D_MODEL), F32),
                        pltpu.VMEM((N_FF_CHUNKS, 2 * g, 2 * FF_CHUNK), F32)],
        compiler_params=_params(_vmem_limit(blk, res + scr, tmp), 2),
        name="convffn",
    )(x2d, gn, halo, wup, wc, wdn, gfin)


def _prep_layer(l, g_mix, w_in, b_igate, b_fgate, g_mhead, a_sinks, w_a, w_b, w_out, g_ffn, w_up,
                w_conv, b_conv, w_down):
    w = w_in[l]
    seg = lambda i: w[:, _OFF[i]:_OFF[i + 1]]
    w_main = jnp.concatenate([seg(0), seg(1), seg(2), seg(3), seg(6), seg(7), seg(8), seg(9)],
                             axis=1).astype(BF16)
    w_gate8 = jnp.concatenate([seg(4), seg(5)], axis=1)
    w_gate = jnp.pad(w_gate8, ((0, 0), (0, LANES_V7X - 2 * M_HEADS))).astype(BF16)
    w_gate_t = jnp.pad(w_gate8.T, ((0, SUBLANES_V7X), (0, 0))).astype(BF16)
    bias8 = jnp.concatenate([b_igate[l], b_fgate[l]]).astype(F32)
    brow = jnp.pad(bias8, (0, LANES_V7X - 2 * M_HEADS)).reshape(1, LANES_V7X)
    bcol = jnp.broadcast_to(bias8[:, None], (SUBLANES_V7X, LANES_V7X))
    sinks = jnp.broadcast_to(a_sinks[l].reshape(A_HEADS, 1).astype(F32), (A_HEADS, LANES_V7X))
    up = w_up[l]
    wup = jnp.stack([jnp.concatenate([up[:, j * FF_CHUNK:(j + 1) * FF_CHUNK],
                                      up[:, D_FF + j * FF_CHUNK:D_FF + (j + 1) * FF_CHUNK]], axis=1)
                     for j in range(N_FF_CHUNKS)]).astype(BF16)
    cw = jnp.concatenate([w_conv[l], b_conv[l][None, :]], axis=0)
    cw = jnp.pad(cw, ((0, SUBLANES_V7X - CONV_W - 1), (0, 0))).astype(F32)
    wc = jnp.stack([jnp.concatenate([cw[:, j * FF_CHUNK:(j + 1) * FF_CHUNK],
                                     cw[:, D_FF + j * FF_CHUNK:D_FF + (j + 1) * FF_CHUNK]], axis=1)
                    for j in range(N_FF_CHUNKS)])
    wdn = w_down[l].reshape(N_FF_CHUNKS, FF_CHUNK, D_MODEL).astype(BF16)
    return dict(g_mix=g_mix[l].reshape(1, D_MODEL), w_main=w_main, w_gate=w_gate, w_gate_t=w_gate_t,
                brow=brow, bcol=bcol, gmh=g_mhead[l].reshape(1, M_WIDTH), sinks=sinks,
                wa=w_a[l].astype(BF16), wb=w_b[l].astype(BF16), wo=w_out[l].astype(BF16),
                g_ffn=g_ffn[l].reshape(1, D_MODEL), wup=wup, wc=wc, wdn=wdn)


def _rope_tables(pos):
    half = A_HEAD_DIM // 2
    inv = ROPE_THETA ** (-jnp.arange(0, A_HEAD_DIM, 2, dtype=F32) / A_HEAD_DIM)
    ang = pos.astype(F32)[:, None] * inv[None, :]
    ang = jnp.tile(jnp.concatenate([ang, ang], axis=-1), (1, LANES_V7X // A_HEAD_DIM))
    sign = jnp.where((jnp.arange(LANES_V7X) % A_HEAD_DIM) < half, -1.0, 1.0).astype(F32)
    return jnp.cos(ang), jnp.sin(ang) * sign[None, :]


def _prompt_trunk(x, layers, g_final):
    B, S, _ = x.shape
    n = B * S
    x2d = x.reshape(n, D_MODEL)
    cos, sin = _rope_tables(jnp.arange(S, dtype=jnp.int32))
    nchunks = S // M_CHUNK
    c0 = jnp.zeros((B, 2, LANES_V7X, LANES_V7X), F32)
    n0 = jnp.zeros((B, 2, LANES_V7X), F32)
    m0 = jnp.zeros((B, SUBLANES_V7X, LANES_V7X), F32)
    halo0 = jnp.zeros((B, CONV_W - 1, 2 * D_FF), F32)
    outs = dict(C=[], n=[], m=[], k=[], v=[], conv=[])
    for l, p in enumerate(layers):
        mq, mk, mv, mo, aq, ak, av, gt, gc, gr = _inproj(
            x2d, p["g_mix"], p["w_main"], p["w_gate"], p["w_gate_t"], cos, sin, S // ROW_TILE)
        hm, c_new, n_new, m_new = _mlstm(mq, mk, mv, gc, gr, p["brow"], p["bcol"], c0, n0, m0,
                                         nseq=B, nchunks=nchunks, chunk=M_CHUNK, real=M_CHUNK, bt=1,
                                         mm_dtype=BF16)
        ha = _swa_prompt(aq, ak, av, p["sinks"], nseq=B, nblk=S // WINDOW)
        x2d = _merge(hm, mo, ha, gt, x2d, p["gmh"], p["wa"], p["wb"], p["wo"])
        x2d, conv = _ffn(x2d, p["g_ffn"], halo0, p["wup"], p["wc"], p["wdn"], g_final,
                         nseq=B, t=ROW_TILE, g=1, final_norm=(l == DEPTH - 1))
        outs["C"].append(c_new.reshape(B, M_HEADS, M_QK_DIM, M_V_DIM))
        outs["n"].append(n_new.reshape(B, M_HEADS, M_QK_DIM))
        outs["m"].append(m_new[:, :M_HEADS, 0])
        outs["k"].append(ak.reshape(B, S, A_KV_HEADS, A_HEAD_DIM)[:, S - WINDOW:])
        outs["v"].append(av.reshape(B, S, A_KV_HEADS, A_HEAD_DIM)[:, S - WINDOW:])
        outs["conv"].append(conv)
    return (x2d.reshape(B, S, D_MODEL),) + tuple(jnp.stack(outs[k]) for k in ("C", "n", "m", "k", "v", "conv"))


def _sample_trunk(x, st_c, st_n, st_m, st_k, st_v, st_conv, layers, g_final):
    B, T, _ = x.shape
    P = SAMPLE_PAD_T
    bt = LANES_V7X // P
    gb = SAMPLE_FFN_GROUP
    ng = B // gb
    pos =PAST_LEN + (jnp.arange(B * P, dtype=jnp.int32) % P)
    cos, sin = _rope_tables(pos)
    c_all = st_c.reshape(DEPTH, B, 2, LANES_V7X, LANES_V7X)
    n_all = st_n.reshape(DEPTH, B, 2, LANES_V7X)
    k_all = st_k.reshape(DEPTH * B, WINDOW, KV_W)
    v_all = st_v.reshape(DEPTH * B, WINDOW, KV_W)
    xpad = jnp.pad(x, ((0, 0), (0, P - T), (0, 0))).reshape(B * P, D_MODEL)
    outs = dict(C=[], n=[], m=[], k=[], v=[], conv=[])
    y_bm = None
    for l, p in enumerate(layers):
        mq, mk, mv, mo, aq, ak, av, gt, gc, gr = _inproj(
            xpad, p["g_mix"], p["w_main"], p["w_gate"], p["w_gate_t"], cos, sin, (B * P) // ROW_TILE)
        m0 = jnp.broadcast_to(jnp.pad(st_m[l], ((0, 0), (0, SUBLANES_V7X - M_HEADS)))[:, :, None],
                              (B, SUBLANES_V7X, LANES_V7X))
        hm, c_new, n_new, m_new = _mlstm(mq, mk, mv, gc, gr, p["brow"], p["bcol"], c_all[l], n_all[l], m0,
                                         nseq=B, nchunks=1, chunk=P, real=T, bt=bt, mm_dtype=F32)
        ha, k_new, v_new = _swa_sample(aq, ak, av, k_all, v_all, p["sinks"],
                                       layer=l, nseq=B, bt=bt, real=T)
        xm = _merge(hm, mo, ha, gt, xpad, p["gmh"], p["wa"], p["wb"], p["wo"])
        x_tm = jnp.transpose(xm.reshape(ng, gb, P, D_MODEL)[:, :, :T], (0, 2, 1, 3)).reshape(T * B, D_MODEL)
        halo = jnp.transpose(st_conv[l].reshape(ng, gb, CONV_W - 1, 2 * D_FF),
                             (0, 2, 1, 3)).reshape(ng, (CONV_W - 1) * gb, 2 * D_FF)
        y_tm, conv = _ffn(x_tm, p["g_ffn"], halo, p["wup"], p["wc"], p["wdn"], g_final,
                          nseq=ng, t=T * gb, g=gb, final_norm=(l == DEPTH - 1))
        y_bm = jnp.transpose(y_tm.reshape(ng, T, gb, D_MODEL), (0, 2, 1, 3)).reshape(B, T, D_MODEL)
        xpad = jnp.pad(y_bm, ((0, 0), (0, P - T), (0, 0))).reshape(B * P, D_MODEL)
        outs["C"].append(c_new.reshape(B, M_HEADS, M_QK_DIM, M_V_DIM))
        outs["n"].append(n_new.reshape(B, M_HEADS, M_QK_DIM))
        outs["m"].append(m_new[:, :M_HEADS, 0])
        outs["k"].append(k_new.reshape(B, WINDOW, A_KV_HEADS, A_HEAD_DIM))
        outs["v"].append(v_new.reshape(B, WINDOW, A_KV_HEADS, A_HEAD_DIM))
        outs["conv"].append(jnp.transpose(conv.reshape(ng, CONV_W - 1, gb, 2 * D_FF),
                                          (0, 2, 1, 3)).reshape(B, CONV_W - 1, 2 * D_FF))
    return (y_bm,) + tuple(jnp.stack(outs[k]) for k in ("C", "n", "m", "k", "v", "conv"))


def kernel(x_prompt, x_sample, state_mlstm_C, state_mlstm_n, state_mlstm_m, cache_swa_k, cache_swa_v,
           state_ffn_conv, g_mix, w_in, b_igate, b_fgate, g_mhead, a_sinks, w_a, w_b, w_out, g_ffn,
           w_up, w_conv, b_conv, w_down, g_final):
    layers = [_prep_layer(l, g_mix, w_in, b_igate, b_fgate, g_mhead, a_sinks, w_a, w_b, w_out, g_ffn,
                          w_up, w_conv, b_conv, w_down) for l in range(DEPTH)]
    gfin = g_final.reshape(1, D_MODEL)
    yp, p_c, p_n, p_m, p_k, p_v, p_conv = _prompt_trunk(x_prompt, layers, gfin)
    ys, s_c, s_n, s_m, s_k, s_v, s_conv = _sample_trunk(
        x_sample, state_mlstm_C, state_mlstm_n, state_mlstm_m, cache_swa_k, cache_swa_v,
        state_ffn_conv, layers, gfin)
    return (yp, ys, p_c, p_n, p_m, p_k, p_v, p_conv, s_c, s_n, s_m, s_k, s_v, s_conv)
```

```python
import functools

import numpy as np
import jax
import jax.numpy as jnp
from jax import lax
from jax.experimental import pallas as pl
from jax.experimental.pallas import tpu as pltpu

F32 = jnp.float32
BF16 = jnp.bfloat16

D_MODEL = 1024
DEPTH = 2
PAST_LEN = 16384
M_HEADS = 4
M_QK_DIM = 64
M_V_DIM = 128
M_WIDTH = M_HEADS * M_V_DIM
M_CHUNK = 128
A_HEADS = 8
A_KV_HEADS = 2
A_GROUP = A_HEADS // A_KV_HEADS
A_HEAD_DIM = 64
A_WIDTH = A_HEADS * A_HEAD_DIM
WINDOW = 128
ROPE_THETA = 10000.0
D_FF = 2816
CONV_W = 3
EPS = 1e-6

QK_W = M_HEADS * M_QK_DIM
KV_W = A_KV_HEADS * A_HEAD_DIM
GATE_W = 2 * D_MODEL
_SPLIT = (QK_W, QK_W, M_WIDTH, M_WIDTH, M_HEADS, M_HEADS, A_WIDTH, KV_W, KV_W, GATE_W)
_OFF = np.concatenate([[0], np.cumsum(_SPLIT)]).tolist()
_MAIN_W = (QK_W, QK_W, M_WIDTH, M_WIDTH, KV_W, GATE_W)
_MAIN_OFF = np.concatenate([[0], np.cumsum(_MAIN_W)]).tolist()
MAIN_COLS = _MAIN_OFF[-1]

LANES_V7X = 128
SUBLANES_V7X = 8
VMEM_BYTES_V7X = 64 * 1024 * 1024
VMEM_LIMIT_CAP_V7X = 60000 * 1024

SAMPLE_PAD_T = SUBLANES_V7X
FF_CHUNK = 2 * LANES_V7X
N_FF_CHUNKS = D_FF // FF_CHUNK
ROW_TILE = 512
SAMPLE_FFN_GROUP = 32


def _vmem_limit(block_bytes, scratch_bytes, temp_bytes):
    need = 2 * block_bytes + scratch_bytes + temp_bytes
    return int(min(max(need, 16 * 1024 * 1024), VMEM_LIMIT_CAP_V7X))


def _params(vmem_bytes, n_axes):
    return pltpu.CompilerParams(dimension_semantics=("arbitrary",) * n_axes,
                                vmem_limit_bytes=vmem_bytes)


def _resident(shape):
    nd = len(shape)
    return pl.BlockSpec(shape, lambda *_: (0,) * nd, pipeline_mode=pl.Buffered(1))


def _log_sigmoid(x):
    return jnp.minimum(x, 0.0) - jnp.log1p(jnp.exp(-jnp.abs(x)))


_NT = (((1,), (1,)), ((), ()))


def _inproj_kernel(x_ref, g_ref, w_ref, wq_ref, wv_ref, wg_ref, wgt_ref, cos_ref, sin_ref, cosq_ref,
                   sinq_ref, mq_ref, mk_ref, mv_ref, mo_ref, ak_ref, gt_ref, aq_ref, av_ref, gc_ref,
                   gr_ref, *, tokens_on_lanes):
    x = x_ref[...]
    r = lax.rsqrt(jnp.mean(x * x, axis=-1, keepdims=True) + EPS)
    xn = (x * r * g_ref[...]).astype(BF16)

    def proj(idx):
        return jnp.dot(xn, w_ref[:, _MAIN_OFF[idx]:_MAIN_OFF[idx + 1]], preferred_element_type=F32)

    mq_ref[...] = proj(0) * (M_QK_DIM ** -0.5)
    mk_ref[...] = proj(1)
    mv_ref[...] = proj(2)
    mo_ref[...] = proj(3)
    gt_ref[...] = proj(5)

    cos = cos_ref[...]
    sin = sin_ref[...]
    rows = x.shape[0]
    lane = lax.broadcasted_iota(jnp.int32, (rows, LANES_V7X), 1)
    first_half = (lane % A_HEAD_DIM) < (A_HEAD_DIM // 2)

    def rope(v, c, s):
        rot = jnp.where(first_half,
                        pltpu.roll(v, LANES_V7X - A_HEAD_DIM // 2, 1),
                        pltpu.roll(v, A_HEAD_DIM // 2, 1))
        return v * c + rot * s

    ak_ref[...] = rope(proj(4), cos, sin)

    scale = A_HEAD_DIM ** -0.5
    if tokens_on_lanes:
        half = A_HEAD_DIM // 2
        cq = cosq_ref[...]
        sq = sinq_ref[...]
        qt = lax.dot_general(wq_ref[...], xn, _NT, preferred_element_type=F32)
        for h in range(A_HEADS):
            x1 = qt[h * A_HEAD_DIM:h * A_HEAD_DIM + half]
            x2 = qt[h * A_HEAD_DIM + half:(h + 1) * A_HEAD_DIM]
            aq_ref[h * A_HEAD_DIM:h * A_HEAD_DIM + half, :] = (x1 * cq - x2 * sq) * scale
            aq_ref[h * A_HEAD_DIM + half:(h + 1) * A_HEAD_DIM, :] = (x2 * cq + x1 * sq) * scale
        av_ref[...] = lax.dot_general(wv_ref[...], xn, _NT, preferred_element_type=F32)
    else:
        cq = cosq_ref[...]
        sq = sinq_ref[...]
        aq = jnp.dot(xn, wq_ref[...], preferred_element_type=F32)
        for c in range(A_WIDTH // LANES_V7X):
            sl = slice(c * LANES_V7X, (c + 1) * LANES_V7X)
            aq_ref[:, sl] = rope(aq[:, sl], cq, sq) * scale
        av_ref[...] = jnp.dot(xn, wv_ref[...], preferred_element_type=F32)

    gc_ref[...] = jnp.dot(xn, wg_ref[...], preferred_element_type=F32)
    grt = lax.dot_general(wgt_ref[...], xn, _NT, preferred_element_type=F32)
    gr_ref[...] = grt[:SUBLANES_V7X]


def _inproj(x2d, g, w_main, w_q, w_v, w_gate, w_gate_t, cos, sin, cos_q, sin_q, tiles_per_seq,
            tokens_on_lanes):
    n = x2d.shape[0]
    t = min(ROW_TILE, n)
    grid = (n // t,)
    row = lambda w: pl.BlockSpec((t, w), lambda i: (i, 0))
    col = lambda h: pl.BlockSpec((h, t), lambda i: (0, i))
    tab = pl.BlockSpec((t, LANES_V7X), lambda i: (i % tiles_per_seq, 0))
    if tokens_on_lanes:
        qtab = pl.BlockSpec((A_HEAD_DIM // 2, t), lambda i: (0, i % tiles_per_seq))
        q_spec, v_spec = col(A_WIDTH), col(KV_W)
        q_shape, v_shape = (A_WIDTH, n), (KV_W, n)
    else:
        qtab = tab
        q_spec, v_spec = row(A_WIDTH), row(KV_W)
        q_shape, v_shape = (n, A_WIDTH), (n, KV_W)
    out_shapes = [jax.ShapeDtypeStruct((n, w), F32) for w in _MAIN_W]
    out_shapes += [jax.ShapeDtypeStruct(q_shape, F32), jax.ShapeDtypeStruct(v_shape, F32),
                   jax.ShapeDtypeStruct((n, LANES_V7X), F32), jax.ShapeDtypeStruct((SUBLANES_V7X, n), F32)]
    out_specs = [row(w) for w in _MAIN_W] + [q_spec, v_spec, row(LANES_V7X), col(SUBLANES_V7X)]
    blk = 4 * t * (D_MODEL + MAIN_COLS + A_WIDTH + KV_W + 5 * LANES_V7X + SUBLANES_V7X)
    res = 2 * D_MODEL * (MAIN_COLS + A_WIDTH + KV_W + LANES_V7X + 2 * SUBLANES_V7X)
    tmp = 4 * t * (GATE_W + D_MODEL)
    kern = functools.partial(_inproj_kernel, tokens_on_lanes=tokens_on_lanes)
    return pl.pallas_call(
        kern,
        grid=grid,
        in_specs=[row(D_MODEL), _resident((1, D_MODEL)), _resident((D_MODEL, MAIN_COLS)),
                  _resident(w_q.shape), _resident(w_v.shape),
                  _resident((D_MODEL, LANES_V7X)), _resident((2 * SUBLANES_V7X, D_MODEL)),
                  tab, tab, qtab, qtab],
        out_specs=out_specs,
        out_shape=out_shapes,
        compiler_params=_params(_vmem_limit(blk, res, tmp), 1),
        name="inproj",
    )(x2d, g, w_main, w_q, w_v, w_gate, w_gate_t, cos, sin, cos_q, sin_q)


def _mlstm_kernel(q_ref, k_ref, v_ref, gc_ref, gr_ref, brow_ref, bcol_ref, c0_ref, n0_ref, m0_ref,
                  h_ref, c_ref, n_ref, m_ref, *, chunk, real, bt, mm_dtype):
    L = chunk
    ci = pl.program_id(1)

    @pl.when(ci == 0)
    def _():
        c_ref[...] = c0_ref[...]
        n_ref[...] = n0_ref[...]
        m_ref[...] = m0_ref[...]

    def mx(a):
        return a.astype(mm_dtype)

    def split3(a):
        hi = a.astype(BF16).astype(F32)
        r1 = a - hi
        mid = r1.astype(BF16).astype(F32)
        lo = (r1 - mid).astype(BF16).astype(F32)
        return hi, mid, lo

    ti = lax.broadcasted_iota(jnp.int32, (L, L), 0)
    si = lax.broadcasted_iota(jnp.int32, (L, L), 1)
    causal = si <= ti
    tri = causal.astype(F32)
    tri_t = (ti <= si).astype(F32)
    lane_g = lax.broadcasted_iota(jnp.int32, (L, LANES_V7X), 1)
    row_g = lax.broadcasted_iota(jnp.int32, (L, LANES_V7X), 0)
    is_f_col = (lane_g >= M_HEADS) & (lane_g < 2 * M_HEADS)
    row_r = lax.broadcasted_iota(jnp.int32, (SUBLANES_V7X, L), 0)
    col_r = lax.broadcasted_iota(jnp.int32, (SUBLANES_V7X, L), 1)
    is_f_row = row_r >= M_HEADS
    lane_p = lax.broadcasted_iota(jnp.int32, (L, LANES_V7X), 1)
    half_masks = (lane_p < M_QK_DIM, lane_p >= M_QK_DIM)
    rows_c = lax.broadcasted_iota(jnp.int32, (2 * M_QK_DIM, 1), 0)
    lanes_n = lax.broadcasted_iota(jnp.int32, (1, LANES_V7X), 1)
    neg_inf = jnp.float32(-jnp.inf)

    for bi in range(bt):
        rs = slice(bi * L, (bi + 1) * L)
        gcol = gc_ref[rs, :] + brow_ref[...]
        gcol = jnp.where(is_f_col, _log_sigmoid(gcol), gcol)
        fcol = jnp.where(is_f_col, gcol, 0.0)
        grow = gr_ref[:, rs] + bcol_ref[:, :L]
        grow = jnp.where(is_f_row, _log_sigmoid(grow), grow)
        frow = jnp.where(is_f_row, grow, 0.0)
        if real < L:
            gcol = jnp.where(row_g < real, gcol, neg_inf)
            fcol = jnp.where(row_g < real, fcol, 0.0)
            grow = jnp.where(col_r < real, grow, neg_inf)
            frow = jnp.where(col_r < real, frow, 0.0)
        cum_c = sum(jnp.dot(tri, p, preferred_element_type=F32) for p in split3(fcol))
        cum_r = sum(jnp.dot(p, tri_t, preferred_element_type=F32) for p in split3(frow))

        new_m = []
        for pair in range(M_HEADS // 2):
            ps = slice(pair * LANES_V7X, (pair + 1) * LANES_V7X)
            qp = q_ref[rs, ps]
            kp = k_ref[rs, ps]
            cp = c_ref[bi, pair]
            npair = n_ref[bi, pair:pair + 1, :]
            d_c = []
            d_n = []
            decays = []
            for half in range(2):
                h = 2 * pair + half
                hm = half_masks[half]
                b_col = cum_c[:, M_HEADS + h:M_HEADS + h + 1]
                li_col = gcol[:, h:h + 1]
                b_row = cum_r[M_HEADS + h:M_HEADS + h + 1, :]
                li_row = grow[h:h + 1, :]
                m0 = m_ref[bi, h:h + 1, 0:1]
                vh = v_ref[rs, h * M_V_DIM:(h + 1) * M_V_DIM]

                dmat = jnp.where(causal, b_col - b_row + li_row, neg_inf)
                inter = b_col + m0
                mrow = jnp.maximum(inter, jnp.max(dmat, axis=-1, keepdims=True))
                dexp = jnp.exp(dmat - mrow)
                iscale = jnp.exp(inter - mrow)
                qh = jnp.where(hm, qp, 0.0)
                s = lax.dot_general(mx(qh), mx(kp), (((1,), (1,)), ((), ())),
                                    preferred_element_type=F32) * dexp
                num = iscale * jnp.dot(mx(qh), mx(cp), preferred_element_type=F32) \
                    + jnp.dot(mx(s), mx(vh), preferred_element_type=F32)
                qn = jnp.sum(qh * npair, axis=-1, keepdims=True)
                den = iscale * qn + jnp.sum(s, axis=-1, keepdims=True)
                h_ref[rs, h * M_V_DIM:(h + 1) * M_V_DIM] = \
                    num / jnp.maximum(jnp.abs(den), jnp.exp(-mrow))

                m_new = mrow[L - 1:L, :]
                b_last = b_col[L - 1:L, :]
                w_col = jnp.exp(b_last - b_col + li_col - m_new)
                decays.append(jnp.exp(b_last + m0 - m_new))
                kw = jnp.where(hm, kp * w_col, 0.0)
                d_c.append(lax.dot_general(mx(kw), mx(vh), (((0,), (0,)), ((), ())),
                                           preferred_element_type=F32))
                d_n.append(jnp.sum(kw, axis=0, keepdims=True))
                new_m.append(m_new)
            c_ref[bi, pair] = jnp.where(rows_c < M_QK_DIM, decays[0], decays[1]) * cp + d_c[0] + d_c[1]
            n_ref[bi, pair:pair + 1, :] = \
                jnp.where(lanes_n < M_QK_DIM, decays[0], decays[1]) * npair + d_n[0] + d_n[1]
        for h in range(M_HEADS):
            m_ref[bi, h:h + 1, :] = jnp.broadcast_to(new_m[h], (1, LANES_V7X))


def _mlstm(q, k, v, gc, gr, brow, bcol, c0, n0, m0, *, nseq, nchunks, chunk, real, bt, mm_dtype):
    rows = bt * chunk
    grid = (nseq // bt, nchunks)
    rmap = lambda b, c: (b * nchunks + c, 0)
    smap3 = lambda b, c: (b, 0, 0)
    smap4 = lambda b, c: (b, 0, 0, 0)
    c_spec = pl.BlockSpec((bt, 2, LANES_V7X, LANES_V7X), smap4)
    n_spec = pl.BlockSpec((bt, 2, LANES_V7X), smap3)
    m_spec = pl.BlockSpec((bt, SUBLANES_V7X, LANES_V7X), smap3)
    n = q.shape[0]
    blk = 4 * (rows * (2 * QK_W + 2 * M_WIDTH + LANES_V7X + SUBLANES_V7X)
               + 2 * bt * (2 * LANES_V7X * LANES_V7X + 2 * LANES_V7X + SUBLANES_V7X * LANES_V7X))
    tmp = 4 * 16 * chunk * max(chunk, LANES_V7X)
    kern = functools.partial(_mlstm_kernel, chunk=chunk, real=real, bt=bt, mm_dtype=mm_dtype)
    return pl.pallas_call(
        kern,
        grid=grid,
        in_specs=[pl.BlockSpec((rows, QK_W), rmap), pl.BlockSpec((rows, QK_W), rmap),
                  pl.BlockSpec((rows, M_WIDTH), rmap), pl.BlockSpec((rows, LANES_V7X), rmap),
                  pl.BlockSpec((SUBLANES_V7X, rows), lambda b, c: (0, b * nchunks + c)),
                  _resident((1, LANES_V7X)), _resident((SUBLANES_V7X, LANES_V7X)),
                  c_spec, n_spec, m_spec],
        out_specs=[pl.BlockSpec((rows, M_WIDTH), rmap), c_spec, n_spec, m_spec],
        out_shape=[jax.ShapeDtypeStruct((n, M_WIDTH), F32),
                   jax.ShapeDtypeStruct((nseq, 2, LANES_V7X, LANES_V7X), F32),
                   jax.ShapeDtypeStruct((nseq, 2, LANES_V7X), F32),
                   jax.ShapeDtypeStruct((nseq, SUBLANES_V7X, LANES_V7X), F32)],
        compiler_params=_params(_vmem_limit(blk, 0, tmp), 2),
        name="mlstm",
    )(q, k, v, gc, gr, brow, bcol, c0, n0, m0)


def _swa_prompt_kernel(qt_ref, kc_ref, kp_ref, vtc_ref, vtp_ref, sink_ref, o_ref):
    blk = pl.program_id(1)
    L = WINDOW
    key = lax.broadcasted_iota(jnp.int32, (2 * L, L), 0)
    qry = lax.broadcasted_iota(jnp.int32, (2 * L, L), 1)
    prev_off = jnp.where(blk > 0, 0, L)
    mask = ((key < L) & (key >= qry + prev_off)) | ((key >= L) & (key - L <= qry))
    k2 = jnp.concatenate([kp_ref[...], kc_ref[...]], axis=0).astype(BF16)
    vt2 = jnp.concatenate([vtp_ref[...], vtc_ref[...]], axis=1).astype(BF16)
    zeros = jnp.zeros((A_HEAD_DIM, L), BF16)
    outs = []
    for hd in range(A_HEADS):
        kv = hd // A_GROUP
        qh = qt_ref[hd * A_HEAD_DIM:(hd + 1) * A_HEAD_DIM, :].astype(BF16)
        qz = jnp.concatenate([qh, zeros] if kv == 0 else [zeros, qh], axis=0)
        s = jnp.where(mask, jnp.dot(k2, qz, preferred_element_type=F32), -jnp.inf)
        sink = sink_ref[hd:hd + 1, 0:1]
        m = jnp.maximum(jnp.max(s, axis=0, keepdims=True), sink)
        p = jnp.exp(s - m)
        den = jnp.sum(p, axis=0, keepdims=True) + jnp.exp(sink - m)
        o = jnp.dot(vt2[kv * A_HEAD_DIM:(kv + 1) * A_HEAD_DIM, :], p.astype(BF16),
                    preferred_element_type=F32)
        outs.append(o / den)
    o_ref[...] = jnp.concatenate(outs, axis=0).T


def _swa_prompt(qt, k, vt, sinks, *, nseq, nblk):
    n = k.shape[0]
    L = WINDOW
    cur = lambda b, i: (b * nblk + i, 0)
    prev = lambda b, i: (b * nblk + jnp.maximum(i - 1, 0), 0)
    cur_t = lambda b, i: (0, b * nblk + i)
    prev_t = lambda b, i: (0, b * nblk + jnp.maximum(i - 1, 0))
    blk = 4 * L * (2 * A_WIDTH + 4 * KV_W)
    tmp = 4 * 24 * 2 * L * L
    return pl.pallas_call(
        _swa_prompt_kernel,
        grid=(nseq, nblk),
        in_specs=[pl.BlockSpec((A_WIDTH, L), cur_t),
                  pl.BlockSpec((L, KV_W), cur), pl.BlockSpec((L, KV_W), prev),
                  pl.BlockSpec((KV_W, L), cur_t), pl.BlockSpec((KV_W, L), prev_t),
                  _resident((A_HEADS, LANES_V7X))],
        out_specs=pl.BlockSpec((L, A_WIDTH), cur),
        out_shape=jax.ShapeDtypeStruct((n, A_WIDTH), F32),
        compiler_params=_params(_vmem_limit(blk, 0, tmp), 2),
        name="swa_prompt",
    )(qt, k, k, vt, vt, sinks)


def _swa_sample_kernel(q_ref, kn_ref, vn_ref, kc_ref, vc_ref, sink_ref, o_ref, ko_ref, vo_ref,
                       *, bt, real):
    P = SAMPLE_PAD_T
    W = WINDOW
    R = A_HEADS * P
    row = lax.broadcasted_iota(jnp.int32, (R, 2 * W), 0)
    key = lax.broadcasted_iota(jnp.int32, (R, 2 * W), 1)
    step = row % P
    mask = ((key < W) & (key >= step)) | ((key >= W) & (key - W <= step) & (key - W < real))
    sink = sink_ref[:, 0:1]
    pad = jnp.zeros((W - P, KV_W), F32)
    for bi in range(bt):
        rs = slice(bi * P, (bi + 1) * P)
        kn = kn_ref[rs, :]
        vn = vn_ref[rs, :]
        kext = jnp.concatenate([kc_ref[bi], kn, pad], axis=0)
        vext = jnp.concatenate([vc_ref[bi], vn, pad], axis=0)
        s = lax.dot_general(q_ref[bi], kext, _NT, preferred_element_type=F32)
        s = jnp.where(mask, s, -jnp.inf)
        m = jnp.maximum(jnp.max(s, axis=-1, keepdims=True), sink)
        p = jnp.exp(s - m)
        den = jnp.sum(p, axis=-1, keepdims=True) + jnp.exp(sink - m)
        o_ref[bi] = jnp.dot(p, vext, preferred_element_type=F32) / den
        ko_ref[bi, 0:W - real, :] = kc_ref[bi, real:W, :]
        ko_ref[bi, W - real:W, :] = kn[0:real, :]
        vo_ref[bi, 0:W - real, :] = vc_ref[bi, real:W, :]
        vo_ref[bi, W - real:W, :] = vn[0:real, :]


def _swa_sample(q_rows, kn, vn, kcache, vcache, sinks, *, layer, nseq, bt, real):
    rows = bt * SAMPLE_PAD_T
    R = A_HEADS * SAMPLE_PAD_T
    nb = nseq // bt
    rmap = lambda i: (i, 0)
    cmap = lambda i: (layer * nb + i, 0, 0)
    omap = lambda i: (i, 0, 0)
    cache = pl.BlockSpec((bt, WINDOW, KV_W), cmap)
    blk = 4 * (2 * bt * R * KV_W + 2 * rows * KV_W + 4 * bt * WINDOW * KV_W)
    tmp = 4 * 16 * R * 2 * WINDOW
    kern = functools.partial(_swa_sample_kernel, bt=bt, real=real)
    return pl.pallas_call(
        kern,
        grid=(nb,),
        in_specs=[pl.BlockSpec((bt, R, KV_W), omap), pl.BlockSpec((rows, KV_W), rmap),
                  pl.BlockSpec((rows, KV_W), rmap), cache, cache, _resident((R, LANES_V7X))],
        out_specs=[pl.BlockSpec((bt, R, KV_W), omap),
                   pl.BlockSpec((bt, WINDOW, KV_W), omap), pl.BlockSpec((bt, WINDOW, KV_W), omap)],
        out_shape=[jax.ShapeDtypeStruct((nseq, R, KV_W), F32),
                   jax.ShapeDtypeStruct((nseq, WINDOW, KV_W), F32),
                   jax.ShapeDtypeStruct((nseq, WINDOW, KV_W), F32)],
        compiler_params=_params(_vmem_limit(blk, 0, tmp), 1),
        name="swa_sample",
    )(q_rows, kn, vn, kcache, vcache, sinks)


def _merge_kernel(hm_ref, mo_ref, ha_ref, gt_ref, x_ref, gmh_ref, wa_ref, wb_ref, wo_ref, o_ref):
    hm = hm_ref[...]
    parts = []
    for h in range(M_HEADS):
        v = hm[:, h * M_V_DIM:(h + 1) * M_V_DIM]
        parts.append(v * lax.rsqrt(jnp.mean(v * v, axis=-1, keepdims=True) + EPS))
    hn = jnp.concatenate(parts, axis=1) * gmh_ref[...] * jax.nn.sigmoid(mo_ref[...])
    a = jnp.dot(hn.astype(BF16), wa_ref[...], preferred_element_type=F32)
    b = jnp.dot(ha_ref[...].astype(BF16), wb_ref[...], preferred_element_type=F32)
    gates = jax.nn.sigmoid(gt_ref[...])
    mixed = gates[:, :D_MODEL] * a + gates[:, D_MODEL:] * b
    o_ref[...] = x_ref[...] + jnp.dot(mixed.astype(BF16), wo_ref[...], preferred_element_type=F32)


def _merge(hm, mo, ha, gt, x2d, gmh, wa, wb, wo):
    n = x2d.shape[0]
    t = min(ROW_TILE, n)
    row = lambda w: pl.BlockSpec((t, w), lambda i: (i, 0))
    blk = 4 * t * (3 * M_WIDTH + GATE_W + 2 * D_MODEL)
    res = 2 * (2 * M_WIDTH * D_MODEL + D_MODEL * D_MODEL)
    tmp = 4 * t * (GATE_W + 3 * D_MODEL)
    return pl.pallas_call(
        _merge_kernel,
        grid=(n // t,),
        in_specs=[row(M_WIDTH), row(M_WIDTH), row(A_WIDTH), row(GATE_W), row(D_MODEL),
                  _resident((1, M_WIDTH)), _resident((M_WIDTH, D_MODEL)),
                  _resident((A_WIDTH, D_MODEL)), _resident((D_MODEL, D_MODEL))],
        out_specs=row(D_MODEL),
        out_shape=jax.ShapeDtypeStruct((n, D_MODEL), F32),
        compiler_params=_params(_vmem_limit(blk, res, tmp), 1),
        name="merge",
    )(hm, mo, ha, gt, x2d, gmh, wa, wb, wo)


def _ffn_kernel(x_ref, gn_ref, halo_ref, wup_ref, wc_ref, wdn_ref, gfin_ref, o_ref, halo_out_ref,
                xn_s, u_a, u_b, act_a, act_b, acc_s, halo_s, *, g, final_norm):
    t = x_ref.shape[0]
    H = u_a.shape[0] - t
    ti = pl.program_id(1)
    last = pl.num_programs(1) - 1

    @pl.when(ti == 0)
    def _():
        for j in range(N_FF_CHUNKS):
            halo_s[j, :, :FF_CHUNK] = halo_ref[0, :, j * FF_CHUNK:(j + 1) * FF_CHUNK]
            halo_s[j, :, FF_CHUNK:] = halo_ref[0, :, D_FF + j * FF_CHUNK:D_FF + (j + 1) * FF_CHUNK]

    x = x_ref[...]
    xn_s[...] = (x * lax.rsqrt(jnp.mean(x * x, axis=-1, keepdims=True) + EPS) * gn_ref[...]).astype(BF16)

    def up(j, u_s):
        u_s[H:H + t, :] = jnp.dot(xn_s[...], wup_ref[j], preferred_element_type=F32)

    def conv_act(j, u_s, act_s):
        u_s[H - 2 * g:H, :] = halo_s[j]
        u = u_s[H:H + t, :]
        u1 = u_s[H - g:H - g + t, :]
        u2 = u_s[H - 2 * g:H - 2 * g + t, :]
        halo_s[j] = u_s[H + t - 2 * g:H + t, :]
        wc = wc_ref[j]
        c = wc[3:4, :] + wc[0:1, :] * u2 + wc[1:2, :] * u1 + wc[2:3, :] * u
        a = c[:, :FF_CHUNK]
        act = 0.5 * a * (1.0 + lax.erf(a * np.float32(np.sqrt(0.5)))) * c[:, FF_CHUNK:]
        act_s[...] = act.astype(BF16)

    def down(j, act_s, first=False):
        d = jnp.dot(act_s[...], wdn_ref[j], preferred_element_type=F32)
        if first:
            acc_s[...] = d
        else:
            acc_s[...] += d

    n = N_FF_CHUNKS
    assert n % 2 == 1 and n >= 5
    up(0, u_a)
    up(1, u_b)
    conv_act(0, u_a, act_a)
    up(2, u_a)
    conv_act(1, u_b, act_b)
    down(0, act_a, first=True)

    def tick_pair(p, carry):
        i = 3 + 2 * p
        up(i, u_b)
        conv_act(i - 1, u_a, act_a)
        down(i - 2, act_b)
        up(i + 1, u_a)
        conv_act(i, u_b, act_b)
        down(i - 1, act_a)
        return carry

    lax.fori_loop(0, (n - 3) // 2, tick_pair, 0)
    conv_act(n - 1, u_a, act_a)
    down(n - 2, act_b)
    down(n - 1, act_a)
    y = x + acc_s[...]
    if final_norm:
        y = y * lax.rsqrt(jnp.mean(y * y, axis=-1, keepdims=True) + EPS) * gfin_ref[...]
    o_ref[...] = y

    @pl.when(ti == last)
    def _():
        for j in range(N_FF_CHUNKS):
            halo_out_ref[0, :, j * FF_CHUNK:(j + 1) * FF_CHUNK] = halo_s[j, :, :FF_CHUNK]
            halo_out_ref[0, :, D_FF + j * FF_CHUNK:D_FF + (j + 1) * FF_CHUNK] = halo_s[j, :, FF_CHUNK:]


def _ffn(x2d, gn, halo, wup, wc, wdn, gfin, *, nseq, t, g, final_norm):
    n = x2d.shape[0]
    tiles = n // (nseq * t)
    H = -(-2 * g // SUBLANES_V7X) * SUBLANES_V7X
    rmap = lambda s, i: (s * tiles + i, 0)
    hmap = lambda s, i: (s, 0, 0)
    blk = 4 * (2 * t * D_MODEL + 2 * 2 * g * 2 * D_FF)
    res = 2 * (D_MODEL * 2 * D_FF + D_FF * D_MODEL) + 4 * N_FF_CHUNKS * SUBLANES_V7X * 2 * FF_CHUNK
    scr = 2 * t * D_MODEL + 2 * 4 * (H + t) * 2 * FF_CHUNK + 2 * 2 * t * FF_CHUNK + 4 * t * D_MODEL \
        + 4 * N_FF_CHUNKS * 2 * g * 2 * FF_CHUNK
    tmp = 4 * t * (6 * FF_CHUNK + D_MODEL)
    kern = functools.partial(_ffn_kernel, g=g, final_norm=final_norm)
    return pl.pallas_call(
        kern,
        grid=(nseq, tiles),
        in_specs=[pl.BlockSpec((t, D_MODEL), rmap), _resident((1, D_MODEL)),
                  pl.BlockSpec((1, 2 * g, 2 * D_FF), hmap),
                  _resident((N_FF_CHUNKS, D_MODEL, 2 * FF_CHUNK)),
                  _resident((N_FF_CHUNKS, SUBLANES_V7X, 2 * FF_CHUNK)),
                  _resident((N_FF_CHUNKS, FF_CHUNK, D_MODEL)), _resident((1, D_MODEL))],
        out_specs=[pl.BlockSpec((t, D_MODEL), rmap), pl.BlockSpec((1, 2 * g, 2 * D_FF), hmap)],
        out_shape=[jax.ShapeDtypeStruct((n, D_MODEL), F32),
                   jax.ShapeDtypeStruct((nseq, 2 * g, 2 * D_FF), F32)],
        scratch_shapes=[pltpu.VMEM((t, D_MODEL), BF16),
                        pltpu.VMEM((H + t, 2 * FF_CHUNK), F32), pltpu.VMEM((H + t, 2 * FF_CHUNK), F32),
                        pltpu.VMEM((t, FF_CHUNK), BF16), pltpu.VMEM((t, FF_CHUNK), BF16),
                        pltpu.VMEM((t, D_MODEL), F32),
                        pltpu.VMEM((N_FF_CHUNKS, 2 * g, 2 * FF_CHUNK), F32)],
        compiler_params=_params(_vmem_limit(blk, res + scr, tmp), 2),
        name="convffn",
    )(x2d, gn, halo, wup, wc, wdn, gfin)


def _prep_layer(l, g_mix, w_in, b_igate, b_fgate, g_mhead, a_sinks, w_a, w_b, w_out, g_ffn, w_up,
                w_conv, b_conv, w_down):
    w = w_in[l]
    seg = lambda i: w[:, _OFF[i]:_OFF[i + 1]]
    w_main = jnp.concatenate([seg(0), seg(1), seg(2), seg(3), seg(7), seg(9)], axis=1).astype(BF16)
    w_q = seg(6).astype(BF16)
    w_v = seg(8).astype(BF16)
    w_gate8 = jnp.concatenate([seg(4), seg(5)], axis=1)
    w_gate = jnp.pad(w_gate8, ((0, 0), (0, LANES_V7X - 2 * M_HEADS))).astype(BF16)
    w_gate_t = jnp.pad(w_gate8.T, ((0, SUBLANES_V7X), (0, 0))).astype(BF16)
    bias8 = jnp.concatenate([b_igate[l], b_fgate[l]]).astype(F32)
    brow = jnp.pad(bias8, (0, LANES_V7X - 2 * M_HEADS)).reshape(1, LANES_V7X)
    bcol = jnp.broadcast_to(bias8[:, None], (SUBLANES_V7X, LANES_V7X))
    sink8 = a_sinks[l].reshape(A_HEADS, 1).astype(F32)
    sinks = jnp.broadcast_to(sink8, (A_HEADS, LANES_V7X))
    sink_rows = jnp.broadcast_to(jnp.repeat(sink8, SAMPLE_PAD_T, axis=0), (A_HEADS * SAMPLE_PAD_T, LANES_V7X))
    up = w_up[l]
    wup = jnp.stack([jnp.concatenate([up[:, j * FF_CHUNK:(j + 1) * FF_CHUNK],
                                      up[:, D_FF + j * FF_CHUNK:D_FF + (j + 1) * FF_CHUNK]], axis=1)
                     for j in range(N_FF_CHUNKS)]).astype(BF16)
    cw = jnp.concatenate([w_conv[l], b_conv[l][None, :]], axis=0)
    cw = jnp.pad(cw, ((0, SUBLANES_V7X - CONV_W - 1), (0, 0))).astype(F32)
    wc = jnp.stack([jnp.concatenate([cw[:, j * FF_CHUNK:(j + 1) * FF_CHUNK],
                                     cw[:, D_FF + j * FF_CHUNK:D_FF + (j + 1) * FF_CHUNK]], axis=1)
                    for j in range(N_FF_CHUNKS)])
    wdn = w_down[l].reshape(N_FF_CHUNKS, FF_CHUNK, D_MODEL).astype(BF16)
    return dict(g_mix=g_mix[l].reshape(1, D_MODEL), w_main=w_main, w_q=w_q, w_v=w_v, w_q_t=w_q.T,
                w_v_t=w_v.T, w_gate=w_gate, w_gate_t=w_gate_t,
                brow=brow, bcol=bcol, gmh=g_mhead[l].reshape(1, M_WIDTH), sinks=sinks, sink_rows=sink_rows,
                wa=w_a[l].astype(BF16), wb=w_b[l].astype(BF16), wo=w_out[l].astype(BF16),
                g_ffn=g_ffn[l].reshape(1, D_MODEL), wup=wup, wc=wc, wdn=wdn)


def _rope_tables(pos):
    half = A_HEAD_DIM // 2
    inv = ROPE_THETA ** (-jnp.arange(0, A_HEAD_DIM, 2, dtype=F32) / A_HEAD_DIM)
    ang = pos.astype(F32)[:, None] * inv[None, :]
    ang = jnp.tile(jnp.concatenate([ang, ang], axis=-1), (1, LANES_V7X // A_HEAD_DIM))
    sign = jnp.where((jnp.arange(LANES_V7X) % A_HEAD_DIM) < half, -1.0, 1.0).astype(F32)
    return jnp.cos(ang), jnp.sin(ang) * sign[None, :]


def _rope_tables_t(pos):
    inv = ROPE_THETA ** (-jnp.arange(0, A_HEAD_DIM, 2, dtype=F32) / A_HEAD_DIM)
    ang = inv[:, None] * pos.astype(F32)[None, :]
    return jnp.cos(ang), jnp.sin(ang)


def _prompt_trunk(x, layers, g_final):
    B, S, _ = x.shape
    n = B * S
    x2d = x.reshape(n, D_MODEL)
    cos, sin = _rope_tables(jnp.arange(S, dtype=jnp.int32))
    cos_t, sin_t = _rope_tables_t(jnp.arange(S, dtype=jnp.int32))
    nchunks = S // M_CHUNK
    c0 = jnp.zeros((B, 2, LANES_V7X, LANES_V7X), F32)
    n0 = jnp.zeros((B, 2, LANES_V7X), F32)
    m0 = jnp.zeros((B, SUBLANES_V7X, LANES_V7X), F32)
    halo0 = jnp.zeros((B, CONV_W - 1, 2 * D_FF), F32)
    outs = dict(C=[], n=[], m=[], k=[], v=[], conv=[])
    for l, p in enumerate(layers):
        mq, mk, mv, mo, ak, gt, aq_t, av_t, gc, gr = _inproj(
            x2d, p["g_mix"], p["w_main"], p["w_q_t"], p["w_v_t"], p["w_gate"], p["w_gate_t"],
            cos, sin, cos_t, sin_t, S // ROW_TILE, True)
        hm, c_new, n_new, m_new = _mlstm(mq, mk, mv, gc, gr, p["brow"], p["bcol"], c0, n0, m0,
                                         nseq=B, nchunks=nchunks, chunk=M_CHUNK, real=M_CHUNK, bt=1,
                                         mm_dtype=BF16)
        ha = _swa_prompt(aq_t, ak, av_t, p["sinks"], nseq=B, nblk=S // WINDOW)
        x2d = _merge(hm, mo, ha, gt, x2d, p["gmh"], p["wa"], p["wb"], p["wo"])
        x2d, conv = _ffn(x2d, p["g_ffn"], halo0, p["wup"], p["wc"], p["wdn"], g_final,
                         nseq=B, t=ROW_TILE, g=1, final_norm=(l == DEPTH - 1))
        outs["C"].append(c_new.reshape(B, M_HEADS, M_QK_DIM, M_V_DIM))
        outs["n"].append(n_new.reshape(B, M_HEADS, M_QK_DIM))
        outs["m"].append(m_new[:, :M_HEADS, 0])
        outs["k"].append(ak.reshape(B, S, A_KV_HEADS, A_HEAD_DIM)[:, S - WINDOW:])
        v_last = av_t.reshape(KV_W, B, S)[:, :, S - WINDOW:]
        outs["v"].append(jnp.transpose(v_last, (1, 2, 0)).reshape(B, WINDOW, A_KV_HEADS, A_HEAD_DIM))
        outs["conv"].append(conv)
    return (x2d.reshape(B, S, D_MODEL),) + tuple(jnp.stack(outs[k]) for k in ("C", "n", "m", "k", "v", "conv"))


def _sample_trunk(x, st_c, st_n, st_m, st_k, st_v, st_conv, layers, g_final):
    B, T, _ = x.shape
    P = SAMPLE_PAD_T
    bt = LANES_V7X // P
    gb = SAMPLE_FFN_GROUP
    ng = B // gb
    pos =PAST_LEN + (jnp.arange(B * P, dtype=jnp.int32) % P)
    cos, sin = _rope_tables(pos)
    c_all = st_c.reshape(DEPTH, B, 2, LANES_V7X, LANES_V7X)
    n_all = st_n.reshape(DEPTH, B, 2, LANES_V7X)
    k_all = st_k.reshape(DEPTH * B, WINDOW, KV_W)
    v_all = st_v.reshape(DEPTH * B, WINDOW, KV_W)
    xpad = jnp.pad(x, ((0, 0), (0, P - T), (0, 0))).reshape(B * P, D_MODEL)
    outs = dict(C=[], n=[], m=[], k=[], v=[], conv=[])
    y_bm = None
    for l, p in enumerate(layers):
        mq, mk, mv, mo, ak, gt, aq, av, gc, gr = _inproj(
            xpad, p["g_mix"], p["w_main"], p["w_q"], p["w_v"], p["w_gate"], p["w_gate_t"],
            cos, sin, cos, sin, (B * P) // ROW_TILE, False)
        m0 = jnp.broadcast_to(jnp.pad(st_m[l], ((0, 0), (0, SUBLANES_V7X - M_HEADS)))[:, :, None],
                              (B, SUBLANES_V7X, LANES_V7X))
        hm, c_new, n_new, m_new = _mlstm(mq, mk, mv, gc, gr, p["brow"], p["bcol"], c_all[l], n_all[l], m0,
                                         nseq=B, nchunks=1, chunk=P, real=T, bt=bt, mm_dtype=F32)
        q5 = jnp.transpose(aq.reshape(B, P, A_KV_HEADS, A_GROUP, A_HEAD_DIM), (0, 2, 3, 1, 4))
        q5 = q5.reshape(B, A_KV_HEADS, A_GROUP * P, A_HEAD_DIM)
        zq = jnp.zeros_like(q5[:, 0])
        q_rows = jnp.concatenate([jnp.concatenate([q5[:, 0], zq], axis=-1),
                                  jnp.concatenate([zq, q5[:, 1]], axis=-1)], axis=1)
        o_rows, k_new, v_new = _swa_sample(q_rows, ak, av, k_all, v_all, p["sink_rows"],
                                           layer=l, nseq=B, bt=bt, real=T)
        o5 = jnp.stack([o_rows[:, :A_GROUP * P, :A_HEAD_DIM], o_rows[:, A_GROUP * P:, A_HEAD_DIM:]], axis=1)
        ha = jnp.transpose(o5.reshape(B, A_KV_HEADS, A_GROUP, P, A_HEAD_DIM),
                           (0, 3, 1, 2, 4)).reshape(B * P, A_WIDTH)
        xm = _merge(hm, mo, ha, gt, xpad, p["gmh"], p["wa"], p["wb"], p["wo"])
        x_tm = jnp.transpose(xm.reshape(ng, gb, P, D_MODEL)[:, :, :T], (0, 2, 1, 3)).reshape(T * B, D_MODEL)
        halo = jnp.transpose(st_conv[l].reshape(ng, gb, CONV_W - 1, 2 * D_FF),
                             (0, 2, 1, 3)).reshape(ng, (CONV_W - 1) * gb, 2 * D_FF)
        y_tm, conv = _ffn(x_tm, p["g_ffn"], halo, p["wup"], p["wc"], p["wdn"], g_final,
                          nseq=ng, t=T * gb, g=gb, final_norm=(l == DEPTH - 1))
        y_bm = jnp.transpose(y_tm.reshape(ng, T, gb, D_MODEL), (0, 2, 1, 3)).reshape(B, T, D_MODEL)
        xpad = jnp.pad(y_bm, ((0, 0), (0, P - T), (0, 0))).reshape(B * P, D_MODEL)
        outs["C"].append(c_new.reshape(B, M_HEADS, M_QK_DIM, M_V_DIM))
        outs["n"].append(n_new.reshape(B, M_HEADS, M_QK_DIM))
        outs["m"].append(m_new[:, :M_HEADS, 0])
        outs["k"].append(k_new.reshape(B, WINDOW, A_KV_HEADS, A_HEAD_DIM))
        outs["v"].append(v_new.reshape(B, WINDOW, A_KV_HEADS, A_HEAD_DIM))
        outs["conv"].append(jnp.transpose(conv.reshape(ng, CONV_W - 1, gb, 2 * D_FF),
                                          (0, 2, 1, 3)).reshape(B, CONV_W - 1, 2 * D_FF))
    return (y_bm,) + tuple(jnp.stack(outs[k]) for k in ("C", "n", "m", "k", "v", "conv"))


def kernel(x_prompt, x_sample, state_mlstm_C, state_mlstm_n, state_mlstm_m, cache_swa_k, cache_swa_v,
           state_ffn_conv, g_mix, w_in, b_igate, b_fgate, g_mhead, a_sinks, w_a, w_b, w_out, g_ffn,
           w_up, w_conv, b_conv, w_down, g_final):
    layers = [_prep_layer(l, g_mix, w_in, b_igate, b_fgate, g_mhead, a_sinks, w_a, w_b, w_out, g_ffn,
                          w_up, w_conv, b_conv, w_down) for l in range(DEPTH)]
    gfin = g_final.reshape(1, D_MODEL)
    yp, p_c, p_n, p_m, p_k, p_v, p_conv = _prompt_trunk(x_prompt, layers, gfin)
    ys, s_c, s_n, s_m, s_k, s_v, s_conv = _sample_trunk(
        x_sample, state_mlstm_C, state_mlstm_n, state_mlstm_m, cache_swa_k, cache_swa_v,
        state_ffn_conv, layers, gfin)
    return (yp, ys, p_c, p_n, p_m, p_k, p_v, p_conv, s_c, s_n, s_m, s_k, s_v, s_conv)
```

```python
import functools

import numpy as np
import jax
import jax.numpy as jnp
from jax import lax
from jax.experimental import pallas as pl
from jax.experimental.pallas import tpu as pltpu

F32 = jnp.float32
BF16 = jnp.bfloat16

D_MODEL = 1024
DEPTH = 2
PAST_LEN = 16384
M_HEADS = 4
M_QK_DIM = 64
M_V_DIM = 128
M_WIDTH = M_HEADS * M_V_DIM
M_CHUNK = 128
A_HEADS = 8
A_KV_HEADS = 2
A_GROUP = A_HEADS // A_KV_HEADS
A_HEAD_DIM = 64
A_WIDTH = A_HEADS * A_HEAD_DIM
WINDOW = 128
ROPE_THETA = 10000.0
D_FF = 2816
CONV_W = 3
EPS = 1e-6

QK_W = M_HEADS * M_QK_DIM
KV_W = A_KV_HEADS * A_HEAD_DIM
GATE_W = 2 * D_MODEL
_SPLIT = (QK_W, QK_W, M_WIDTH, M_WIDTH, M_HEADS, M_HEADS, A_WIDTH, KV_W, KV_W, GATE_W)
_OFF = np.concatenate([[0], np.cumsum(_SPLIT)]).tolist()
_MAIN_W = (QK_W, QK_W, M_WIDTH, M_WIDTH, KV_W, GATE_W)
_MAIN_OFF = np.concatenate([[0], np.cumsum(_MAIN_W)]).tolist()
MAIN_COLS = _MAIN_OFF[-1]

LANES_V7X = 128
SUBLANES_V7X = 8
VMEM_BYTES_V7X = 64 * 1024 * 1024
VMEM_LIMIT_CAP_V7X = 60000 * 1024

SAMPLE_PAD_T = SUBLANES_V7X
FF_CHUNK = 2 * LANES_V7X
N_FF_CHUNKS = D_FF // FF_CHUNK
ROW_TILE = 512
SAMPLE_FFN_GROUP = 32


def _vmem_limit(block_bytes, scratch_bytes, temp_bytes):
    need = 2 * block_bytes + scratch_bytes + temp_bytes
    return int(min(max(need, 16 * 1024 * 1024), VMEM_LIMIT_CAP_V7X))


def _params(vmem_bytes, n_axes):
    return pltpu.CompilerParams(dimension_semantics=("arbitrary",) * n_axes,
                                vmem_limit_bytes=vmem_bytes)


def _resident(shape):
    nd = len(shape)
    return pl.BlockSpec(shape, lambda *_: (0,) * nd, pipeline_mode=pl.Buffered(1))


def _log_sigmoid(x):
    return jnp.minimum(x, 0.0) - jnp.log1p(jnp.exp(-jnp.abs(x)))


_NT = (((1,), (1,)), ((), ()))


def _inproj_kernel(x_ref, g_ref, w_ref, wq_ref, wv_ref, wg_ref, wgt_ref, cos_ref, sin_ref, cosq_ref,
                   sinq_ref, mq_ref, mk_ref, mv_ref, mo_ref, ak_ref, gt_ref, aq_ref, av_ref, gc_ref,
                   gr_ref, *, tokens_on_lanes):
    x = x_ref[...]
    r = lax.rsqrt(jnp.mean(x * x, axis=-1, keepdims=True) + EPS)
    xn = (x * r * g_ref[...]).astype(BF16)

    def proj(idx):
        return jnp.dot(xn, w_ref[:, _MAIN_OFF[idx]:_MAIN_OFF[idx + 1]], preferred_element_type=F32)

    mq_ref[...] = proj(0) * (M_QK_DIM ** -0.5)
    mk_ref[...] = proj(1)
    mv_ref[...] = proj(2)
    mo_ref[...] = proj(3)
    gt_ref[...] = proj(5)

    cos = cos_ref[...]
    sin = sin_ref[...]
    rows = x.shape[0]
    lane = lax.broadcasted_iota(jnp.int32, (rows, LANES_V7X), 1)
    first_half = (lane % A_HEAD_DIM) < (A_HEAD_DIM // 2)

    def rope(v, c, s):
        rot = jnp.where(first_half,
                        pltpu.roll(v, LANES_V7X - A_HEAD_DIM // 2, 1),
                        pltpu.roll(v, A_HEAD_DIM // 2, 1))
        return v * c + rot * s

    ak_ref[...] = rope(proj(4), cos, sin)

    scale = A_HEAD_DIM ** -0.5
    if tokens_on_lanes:
        half = A_HEAD_DIM // 2
        cq = cosq_ref[...]
        sq = sinq_ref[...]
        qt = lax.dot_general(wq_ref[...], xn, _NT, preferred_element_type=F32)
        for h in range(A_HEADS):
            x1 = qt[h * A_HEAD_DIM:h * A_HEAD_DIM + half]
            x2 = qt[h * A_HEAD_DIM + half:(h + 1) * A_HEAD_DIM]
            aq_ref[h * A_HEAD_DIM:h * A_HEAD_DIM + half, :] = (x1 * cq - x2 * sq) * scale
            aq_ref[h * A_HEAD_DIM + half:(h + 1) * A_HEAD_DIM, :] = (x2 * cq + x1 * sq) * scale
        av_ref[...] = lax.dot_general(wv_ref[...], xn, _NT, preferred_element_type=F32)
    else:
        cq = cosq_ref[...]
        sq = sinq_ref[...]
        aq = jnp.dot(xn, wq_ref[...], preferred_element_type=F32)
        for c in range(A_WIDTH // LANES_V7X):
            sl = slice(c * LANES_V7X, (c + 1) * LANES_V7X)
            aq_ref[:, sl] = rope(aq[:, sl], cq, sq) * scale
        av_ref[...] = jnp.dot(xn, wv_ref[...], preferred_element_type=F32)

    gc_ref[...] = jnp.dot(xn, wg_ref[...], preferred_element_type=F32)
    grt = lax.dot_general(wgt_ref[...], xn, _NT, preferred_element_type=F32)
    gr_ref[...] = grt[:SUBLANES_V7X]


def _inproj(x2d, g, w_main, w_q, w_v, w_gate, w_gate_t, cos, sin, cos_q, sin_q, tiles_per_seq,
            tokens_on_lanes):
    n = x2d.shape[0]
    t = min(ROW_TILE, n)
    grid = (n // t,)
    row = lambda w: pl.BlockSpec((t, w), lambda i: (i, 0))
    col = lambda h: pl.BlockSpec((h, t), lambda i: (0, i))
    tab = pl.BlockSpec((t, LANES_V7X), lambda i: (i % tiles_per_seq, 0))
    if tokens_on_lanes:
        qtab = pl.BlockSpec((A_HEAD_DIM // 2, t), lambda i: (0, i % tiles_per_seq))
        q_spec, v_spec = col(A_WIDTH), col(KV_W)
        q_shape, v_shape = (A_WIDTH, n), (KV_W, n)
    else:
        qtab = tab
        q_spec, v_spec = row(A_WIDTH), row(KV_W)
        q_shape, v_shape = (n, A_WIDTH), (n, KV_W)
    out_shapes = [jax.ShapeDtypeStruct((n, w), F32) for w in _MAIN_W]
    out_shapes += [jax.ShapeDtypeStruct(q_shape, F32), jax.ShapeDtypeStruct(v_shape, F32),
                   jax.ShapeDtypeStruct((n, LANES_V7X), F32), jax.ShapeDtypeStruct((SUBLANES_V7X, n), F32)]
    out_specs = [row(w) for w in _MAIN_W] + [q_spec, v_spec, row(LANES_V7X), col(SUBLANES_V7X)]
    blk = 4 * t * (D_MODEL + MAIN_COLS + A_WIDTH + KV_W + 5 * LANES_V7X + SUBLANES_V7X)
    res = 2 * D_MODEL * (MAIN_COLS + A_WIDTH + KV_W + LANES_V7X + 2 * SUBLANES_V7X)
    tmp = 4 * t * (GATE_W + D_MODEL)
    kern = functools.partial(_inproj_kernel, tokens_on_lanes=tokens_on_lanes)
    return pl.pallas_call(
        kern,
        grid=grid,
        in_specs=[row(D_MODEL), _resident((1, D_MODEL)), _resident((D_MODEL, MAIN_COLS)),
                  _resident(w_q.shape), _resident(w_v.shape),
                  _resident((D_MODEL, LANES_V7X)), _resident((2 * SUBLANES_V7X, D_MODEL)),
                  tab, tab, qtab, qtab],
        out_specs=out_specs,
        out_shape=out_shapes,
        compiler_params=_params(_vmem_limit(blk, res, tmp), 1),
        name="inproj",
    )(x2d, g, w_main, w_q, w_v, w_gate, w_gate_t, cos, sin, cos_q, sin_q)


def _mlstm_kernel(q_ref, k_ref, v_ref, gc_ref, gr_ref, brow_ref, bcol_ref, c0_ref, n0_ref, m0_ref,
                  h_ref, c_ref, n_ref, m_ref, *, chunk, real, bt, mm_dtype):
    L = chunk
    ci = pl.program_id(1)

    @pl.when(ci == 0)
    def _():
        c_ref[...] = c0_ref[...]
        n_ref[...] = n0_ref[...]
        m_ref[...] = m0_ref[...]

    def mx(a):
        return a.astype(mm_dtype)

    def split3(a):
        hi = a.astype(BF16).astype(F32)
        r1 = a - hi
        mid = r1.astype(BF16).astype(F32)
        lo = (r1 - mid).astype(BF16).astype(F32)
        return hi, mid, lo

    ti = lax.broadcasted_iota(jnp.int32, (L, L), 0)
    si = lax.broadcasted_iota(jnp.int32, (L, L), 1)
    causal = si <= ti
    tri = causal.astype(F32)
    tri_t = (ti <= si).astype(F32)
    lane_g = lax.broadcasted_iota(jnp.int32, (L, LANES_V7X), 1)
    row_g = lax.broadcasted_iota(jnp.int32, (L, LANES_V7X), 0)
    is_f_col = (lane_g >= M_HEADS) & (lane_g < 2 * M_HEADS)
    row_r = lax.broadcasted_iota(jnp.int32, (SUBLANES_V7X, L), 0)
    col_r = lax.broadcasted_iota(jnp.int32, (SUBLANES_V7X, L), 1)
    is_f_row = row_r >= M_HEADS
    lane_p = lax.broadcasted_iota(jnp.int32, (L, LANES_V7X), 1)
    half_masks = (lane_p < M_QK_DIM, lane_p >= M_QK_DIM)
    rows_c = lax.broadcasted_iota(jnp.int32, (2 * M_QK_DIM, 1), 0)
    lanes_n = lax.broadcasted_iota(jnp.int32, (1, LANES_V7X), 1)
    neg_inf = jnp.float32(-jnp.inf)

    for bi in range(bt):
        rs = slice(bi * L, (bi + 1) * L)
        gcol = gc_ref[rs, :] + brow_ref[...]
        gcol = jnp.where(is_f_col, _log_sigmoid(gcol), gcol)
        fcol = jnp.where(is_f_col, gcol, 0.0)
        grow = gr_ref[:, rs] + bcol_ref[:, :L]
        grow = jnp.where(is_f_row, _log_sigmoid(grow), grow)
        frow = jnp.where(is_f_row, grow, 0.0)
        if real < L:
            gcol = jnp.where(row_g < real, gcol, neg_inf)
            fcol = jnp.where(row_g < real, fcol, 0.0)
            grow = jnp.where(col_r < real, grow, neg_inf)
            frow = jnp.where(col_r < real, frow, 0.0)
        cum_c = sum(jnp.dot(tri, p, preferred_element_type=F32) for p in split3(fcol))
        cum_r = sum(jnp.dot(p, tri_t, preferred_element_type=F32) for p in split3(frow))

        new_m = []
        for pair in range(M_HEADS // 2):
            ps = slice(pair * LANES_V7X, (pair + 1) * LANES_V7X)
            qp = q_ref[rs, ps]
            kp = k_ref[rs, ps]
            cp = c_ref[bi, pair]
            npair = n_ref[bi, pair:pair + 1, :]
            d_c = []
            d_n = []
            decays = []
            for half in range(2):
                h = 2 * pair + half
                hm = half_masks[half]
                b_col = cum_c[:, M_HEADS + h:M_HEADS + h + 1]
                li_col = gcol[:, h:h + 1]
                b_row = cum_r[M_HEADS + h:M_HEADS + h + 1, :]
                li_row = grow[h:h + 1, :]
                m0 = m_ref[bi, h:h + 1, 0:1]
                vh = v_ref[rs, h * M_V_DIM:(h + 1) * M_V_DIM]

                dmat = jnp.where(causal, b_col - b_row + li_row, neg_inf)
                inter = b_col + m0
                mrow = jnp.maximum(inter, jnp.max(dmat, axis=-1, keepdims=True))
                dexp = jnp.exp(dmat - mrow)
                iscale = jnp.exp(inter - mrow)
                qh = jnp.where(hm, qp, 0.0)
                s = lax.dot_general(mx(qh), mx(kp), (((1,), (1,)), ((), ())),
                                    preferred_element_type=F32) * dexp
                num = iscale * jnp.dot(mx(qh), mx(cp), preferred_element_type=F32) \
                    + jnp.dot(mx(s), mx(vh), preferred_element_type=F32)
                qn = jnp.sum(qh * npair, axis=-1, keepdims=True)
                den = iscale * qn + jnp.sum(s, axis=-1, keepdims=True)
                h_ref[rs, h * M_V_DIM:(h + 1) * M_V_DIM] = \
                    num / jnp.maximum(jnp.abs(den), jnp.exp(-mrow))

                m_new = mrow[L - 1:L, :]
                b_last = b_col[L - 1:L, :]
                w_col = jnp.exp(b_last - b_col + li_col - m_new)
                decays.append(jnp.exp(b_last + m0 - m_new))
                kw = jnp.where(hm, kp * w_col, 0.0)
                d_c.append(lax.dot_general(mx(kw), mx(vh), (((0,), (0,)), ((), ())),
                                           preferred_element_type=F32))
                d_n.append(jnp.sum(kw, axis=0, keepdims=True))
                new_m.append(m_new)
            c_ref[bi, pair] = jnp.where(rows_c < M_QK_DIM, decays[0], decays[1]) * cp + d_c[0] + d_c[1]
            n_ref[bi, pair:pair + 1, :] = \
                jnp.where(lanes_n < M_QK_DIM, decays[0], decays[1]) * npair + d_n[0] + d_n[1]
        for h in range(M_HEADS):
            m_ref[bi, h:h + 1, :] = jnp.broadcast_to(new_m[h], (1, LANES_V7X))


def _mlstm(q, k, v, gc, gr, brow, bcol, c0, n0, m0, *, nseq, nchunks, chunk, real, bt, mm_dtype):
    rows = bt * chunk
    grid = (nseq // bt, nchunks)
    rmap = lambda b, c: (b * nchunks + c, 0)
    smap3 = lambda b, c: (b, 0, 0)
    smap4 = lambda b, c: (b, 0, 0, 0)
    c_spec = pl.BlockSpec((bt, 2, LANES_V7X, LANES_V7X), smap4)
    n_spec = pl.BlockSpec((bt, 2, LANES_V7X), smap3)
    m_spec = pl.BlockSpec((bt, SUBLANES_V7X, LANES_V7X), smap3)
    n = q.shape[0]
    blk = 4 * (rows * (2 * QK_W + 2 * M_WIDTH + LANES_V7X + SUBLANES_V7X)
               + 2 * bt * (2 * LANES_V7X * LANES_V7X + 2 * LANES_V7X + SUBLANES_V7X * LANES_V7X))
    tmp = 4 * 16 * chunk * max(chunk, LANES_V7X)
    kern = functools.partial(_mlstm_kernel, chunk=chunk, real=real, bt=bt, mm_dtype=mm_dtype)
    return pl.pallas_call(
        kern,
        grid=grid,
        in_specs=[pl.BlockSpec((rows, QK_W), rmap), pl.BlockSpec((rows, QK_W), rmap),
                  pl.BlockSpec((rows, M_WIDTH), rmap), pl.BlockSpec((rows, LANES_V7X), rmap),
                  pl.BlockSpec((SUBLANES_V7X, rows), lambda b, c: (0, b * nchunks + c)),
                  _resident((1, LANES_V7X)), _resident((SUBLANES_V7X, LANES_V7X)),
                  c_spec, n_spec, m_spec],
        out_specs=[pl.BlockSpec((rows, M_WIDTH), rmap), c_spec, n_spec, m_spec],
        out_shape=[jax.ShapeDtypeStruct((n, M_WIDTH), F32),
                   jax.ShapeDtypeStruct((nseq, 2, LANES_V7X, LANES_V7X), F32),
                   jax.ShapeDtypeStruct((nseq, 2, LANES_V7X), F32),
                   jax.ShapeDtypeStruct((nseq, SUBLANES_V7X, LANES_V7X), F32)],
        compiler_params=_params(_vmem_limit(blk, 0, tmp), 2),
        name="mlstm",
    )(q, k, v, gc, gr, brow, bcol, c0, n0, m0)


def _swa_prompt_kernel(qt_ref, kc_ref, kp_ref, vtc_ref, vtp_ref, sink_ref, o_ref):
    blk = pl.program_id(1)
    L = WINDOW
    key = lax.broadcasted_iota(jnp.int32, (2 * L, L), 0)
    qry = lax.broadcasted_iota(jnp.int32, (2 * L, L), 1)
    prev_off = jnp.where(blk > 0, 0, L)
    mask = ((key < L) & (key >= qry + prev_off)) | ((key >= L) & (key - L <= qry))
    k2 = jnp.concatenate([kp_ref[...], kc_ref[...]], axis=0).astype(BF16)
    vt2 = jnp.concatenate([vtp_ref[...], vtc_ref[...]], axis=1).astype(BF16)
    zeros = jnp.zeros((A_HEAD_DIM, L), BF16)
    scores = []
    for hd in range(A_HEADS):
        qh = qt_ref[hd * A_HEAD_DIM:(hd + 1) * A_HEAD_DIM, :].astype(BF16)
        qz = jnp.concatenate([qh, zeros] if hd < A_GROUP else [zeros, qh], axis=0)
        scores.append(jnp.dot(k2, qz, preferred_element_type=F32))
    probs = []
    dens = []
    for hd in range(A_HEADS):
        s = jnp.where(mask, scores[hd], -jnp.inf)
        sink = sink_ref[hd:hd + 1, 0:1]
        m = jnp.maximum(jnp.max(s, axis=0, keepdims=True), sink)
        p = jnp.exp(s - m)
        dens.append(jnp.sum(p, axis=0, keepdims=True) + jnp.exp(sink - m))
        probs.append(p.astype(BF16))
    outs = []
    for hd in range(A_HEADS):
        kv = hd // A_GROUP
        o = jnp.dot(vt2[kv * A_HEAD_DIM:(kv + 1) * A_HEAD_DIM, :], probs[hd],
                    preferred_element_type=F32)
        outs.append(o / dens[hd])
    o_ref[...] = jnp.concatenate(outs, axis=0).T


def _swa_prompt(qt, k, vt, sinks, *, nseq, nblk):
    n = k.shape[0]
    L = WINDOW
    cur = lambda b, i: (b * nblk + i, 0)
    prev = lambda b, i: (b * nblk + jnp.maximum(i - 1, 0), 0)
    cur_t = lambda b, i: (0, b * nblk + i)
    prev_t = lambda b, i: (0, b * nblk + jnp.maximum(i - 1, 0))
    blk = 4 * L * (2 * A_WIDTH + 4 * KV_W)
    tmp = 4 * 24 * 2 * L * L
    return pl.pallas_call(
        _swa_prompt_kernel,
        grid=(nseq, nblk),
        in_specs=[pl.BlockSpec((A_WIDTH, L), cur_t),
                  pl.BlockSpec((L, KV_W), cur), pl.BlockSpec((L, KV_W), prev),
                  pl.BlockSpec((KV_W, L), cur_t), pl.BlockSpec((KV_W, L), prev_t),
                  _resident((A_HEADS, LANES_V7X))],
        out_specs=pl.BlockSpec((L, A_WIDTH), cur),
        out_shape=jax.ShapeDtypeStruct((n, A_WIDTH), F32),
        compiler_params=_params(_vmem_limit(blk, 0, tmp), 2),
        name="swa_prompt",
    )(qt, k, k, vt, vt, sinks)


def _swa_sample_kernel(q_ref, kn_ref, vn_ref, kc_ref, vc_ref, sink_ref, o_ref, ko_ref, vo_ref,
                       *, bt, real):
    P = SAMPLE_PAD_T
    W = WINDOW
    R = A_HEADS * P
    row = lax.broadcasted_iota(jnp.int32, (R, 2 * W), 0)
    key = lax.broadcasted_iota(jnp.int32, (R, 2 * W), 1)
    step = row % P
    mask = ((key < W) & (key >= step)) | ((key >= W) & (key - W <= step) & (key - W < real))
    sink = sink_ref[:, 0:1]
    pad = jnp.zeros((W - P, KV_W), F32)
    scores = []
    for bi in range(bt):
        kext = jnp.concatenate([kc_ref[bi], kn_ref[bi * P:(bi + 1) * P, :], pad], axis=0)
        scores.append(lax.dot_general(q_ref[bi], kext, _NT, preferred_element_type=F32))
    probs = []
    dens = []
    for bi in range(bt):
        s = jnp.where(mask, scores[bi], -jnp.inf)
        m = jnp.maximum(jnp.max(s, axis=-1, keepdims=True), sink)
        p = jnp.exp(s - m)
        dens.append(jnp.sum(p, axis=-1, keepdims=True) + jnp.exp(sink - m))
        probs.append(p)
    for bi in range(bt):
        rs = slice(bi * P, (bi + 1) * P)
        vn = vn_ref[rs, :]
        vext = jnp.concatenate([vc_ref[bi], vn, pad], axis=0)
        o_ref[bi] = jnp.dot(probs[bi], vext, preferred_element_type=F32) / dens[bi]
        ko_ref[bi, 0:W - real, :] = kc_ref[bi, real:W, :]
        ko_ref[bi, W - real:W, :] = kn_ref[bi * P:bi * P + real, :]
        vo_ref[bi, 0:W - real, :] = vc_ref[bi, real:W, :]
        vo_ref[bi, W - real:W, :] = vn[0:real, :]


def _swa_sample(q_rows, kn, vn, kcache, vcache, sinks, *, layer, nseq, bt, real):
    rows = bt * SAMPLE_PAD_T
    R = A_HEADS * SAMPLE_PAD_T
    nb = nseq // bt
    rmap = lambda i: (i, 0)
    cmap = lambda i: (layer * nb + i, 0, 0)
    omap = lambda i: (i, 0, 0)
    cache = pl.BlockSpec((bt, WINDOW, KV_W), cmap)
    blk = 4 * (2 * bt * R * KV_W + 2 * rows * KV_W + 4 * bt * WINDOW * KV_W)
    tmp = 4 * 16 * R * 2 * WINDOW
    kern = functools.partial(_swa_sample_kernel, bt=bt, real=real)
    return pl.pallas_call(
        kern,
        grid=(nb,),
        in_specs=[pl.BlockSpec((bt, R, KV_W), omap), pl.BlockSpec((rows, KV_W), rmap),
                  pl.BlockSpec((rows, KV_W), rmap), cache, cache, _resident((R, LANES_V7X))],
        out_specs=[pl.BlockSpec((bt, R, KV_W), omap),
                   pl.BlockSpec((bt, WINDOW, KV_W), omap), pl.BlockSpec((bt, WINDOW, KV_W), omap)],
        out_shape=[jax.ShapeDtypeStruct((nseq, R, KV_W), F32),
                   jax.ShapeDtypeStruct((nseq, WINDOW, KV_W), F32),
                   jax.ShapeDtypeStruct((nseq, WINDOW, KV_W), F32)],
        compiler_params=_params(_vmem_limit(blk, 0, tmp), 1),
        name="swa_sample",
    )(q_rows, kn, vn, kcache, vcache, sinks)


def _merge_kernel(hm_ref, mo_ref, ha_ref, gt_ref, x_ref, gmh_ref, wa_ref, wb_ref, wo_ref, o_ref):
    hm = hm_ref[...]
    parts = []
    for h in range(M_HEADS):
        v = hm[:, h * M_V_DIM:(h + 1) * M_V_DIM]
        parts.append(v * lax.rsqrt(jnp.mean(v * v, axis=-1, keepdims=True) + EPS))
    hn = jnp.concatenate(parts, axis=1) * gmh_ref[...] * jax.nn.sigmoid(mo_ref[...])
    a = jnp.dot(hn.astype(BF16), wa_ref[...], preferred_element_type=F32)
    b = jnp.dot(ha_ref[...].astype(BF16), wb_ref[...], preferred_element_type=F32)
    gates = jax.nn.sigmoid(gt_ref[...])
    mixed = gates[:, :D_MODEL] * a + gates[:, D_MODEL:] * b
    o_ref[...] = x_ref[...] + jnp.dot(mixed.astype(BF16), wo_ref[...], preferred_element_type=F32)


def _merge(hm, mo, ha, gt, x2d, gmh, wa, wb, wo):
    n = x2d.shape[0]
    t = min(ROW_TILE, n)
    row = lambda w: pl.BlockSpec((t, w), lambda i: (i, 0))
    blk = 4 * t * (3 * M_WIDTH + GATE_W + 2 * D_MODEL)
    res = 2 * (2 * M_WIDTH * D_MODEL + D_MODEL * D_MODEL)
    tmp = 4 * t * (GATE_W + 3 * D_MODEL)
    return pl.pallas_call(
        _merge_kernel,
        grid=(n // t,),
        in_specs=[row(M_WIDTH), row(M_WIDTH), row(A_WIDTH), row(GATE_W), row(D_MODEL),
                  _resident((1, M_WIDTH)), _resident((M_WIDTH, D_MODEL)),
                  _resident((A_WIDTH, D_MODEL)), _resident((D_MODEL, D_MODEL))],
        out_specs=row(D_MODEL),
        out_shape=jax.ShapeDtypeStruct((n, D_MODEL), F32),
        compiler_params=_params(_vmem_limit(blk, res, tmp), 1),
        name="merge",
    )(hm, mo, ha, gt, x2d, gmh, wa, wb, wo)


def _ffn_kernel(x_ref, gn_ref, halo_ref, wup_ref, wc_ref, wdn_ref, gfin_ref, o_ref, halo_out_ref,
                xn_s, u_a, u_b, act_a, act_b, acc_s, halo_s, *, g, final_norm):
    t = x_ref.shape[0]
    H = u_a.shape[0] - t
    ti = pl.program_id(1)
    last = pl.num_programs(1) - 1

    @pl.when(ti == 0)
    def _():
        for j in range(N_FF_CHUNKS):
            halo_s[j, :, :FF_CHUNK] = halo_ref[0, :, j * FF_CHUNK:(j + 1) * FF_CHUNK]
            halo_s[j, :, FF_CHUNK:] = halo_ref[0, :, D_FF + j * FF_CHUNK:D_FF + (j + 1) * FF_CHUNK]

    x = x_ref[...]
    xn_s[...] = (x * lax.rsqrt(jnp.mean(x * x, axis=-1, keepdims=True) + EPS) * gn_ref[...]).astype(BF16)

    def up(j, u_s):
        u_s[H:H + t, :] = jnp.dot(xn_s[...], wup_ref[j], preferred_element_type=F32)

    def conv_act(j, u_s, act_s):
        u_s[H - 2 * g:H, :] = halo_s[j]
        u = u_s[H:H + t, :]
        u1 = u_s[H - g:H - g + t, :]
        u2 = u_s[H - 2 * g:H - 2 * g + t, :]
        halo_s[j] = u_s[H + t - 2 * g:H + t, :]
        wc = wc_ref[j]
        c = wc[3:4, :] + wc[0:1, :] * u2 + wc[1:2, :] * u1 + wc[2:3, :] * u
        a = c[:, :FF_CHUNK]
        act = 0.5 * a * (1.0 + lax.erf(a * np.float32(np.sqrt(0.5)))) * c[:, FF_CHUNK:]
        act_s[...] = act.astype(BF16)

    def down(j, act_s, first=False):
        d = jnp.dot(act_s[...], wdn_ref[j], preferred_element_type=F32)
        if first:
            acc_s[...] = d
        else:
            acc_s[...] += d

    n = N_FF_CHUNKS
    assert n % 2 == 1 and n >= 5
    up(0, u_a)
    up(1, u_b)
    conv_act(0, u_a, act_a)
    up(2, u_a)
    conv_act(1, u_b, act_b)
    down(0, act_a, first=True)

    def tick_pair(p, carry):
        i = 3 + 2 * p
        up(i, u_b)
        conv_act(i - 1, u_a, act_a)
        down(i - 2, act_b)
        up(i + 1, u_a)
        conv_act(i, u_b, act_b)
        down(i - 1, act_a)
        return carry

    lax.fori_loop(0, (n - 3) // 2, tick_pair, 0)
    conv_act(n - 1, u_a, act_a)
    down(n - 2, act_b)
    down(n - 1, act_a)
    y = x + acc_s[...]
    if final_norm:
        y = y * lax.rsqrt(jnp.mean(y * y, axis=-1, keepdims=True) + EPS) * gfin_ref[...]
    o_ref[...] = y

    @pl.when(ti == last)
    def _():
        for j in range(N_FF_CHUNKS):
            halo_out_ref[0, :, j * FF_CHUNK:(j + 1) * FF_CHUNK] = halo_s[j, :, :FF_CHUNK]
            halo_out_ref[0, :, D_FF + j * FF_CHUNK:D_FF + (j + 1) * FF_CHUNK] = halo_s[j, :, FF_CHUNK:]


def _ffn(x2d, gn, halo, wup, wc, wdn, gfin, *, nseq, t, g, final_norm):
    n = x2d.shape[0]
    tiles = n // (nseq * t)
    H = -(-2 * g // SUBLANES_V7X) * SUBLANES_V7X
    rmap = lambda s, i: (s * tiles + i, 0)
    hmap = lambda s, i: (s, 0, 0)
    blk = 4 * (2 * t * D_MODEL + 2 * 2 * g * 2 * D_FF)
    res = 2 * (D_MODEL * 2 * D_FF + D_FF * D_MODEL) + 4 * N_FF_CHUNKS * SUBLANES_V7X * 2 * FF_CHUNK
    scr = 2 * t * D_MODEL + 2 * 4 * (H + t) * 2 * FF_CHUNK + 2 * 2 * t * FF_CHUNK + 4 * t * D_MODEL \
        + 4 * N_FF_CHUNKS * 2 * g * 2 * FF_CHUNK
    tmp = 4 * t * (6 * FF_CHUNK + D_MODEL)
    kern = functools.partial(_ffn_kernel, g=g, final_norm=final_norm)
    return pl.pallas_call(
        kern,
        grid=(nseq, tiles),
        in_specs=[pl.BlockSpec((t, D_MODEL), rmap), _resident((1, D_MODEL)),
                  pl.BlockSpec((1, 2 * g, 2 * D_FF), hmap),
                  _resident((N_FF_CHUNKS, D_MODEL, 2 * FF_CHUNK)),
                  _resident((N_FF_CHUNKS, SUBLANES_V7X, 2 * FF_CHUNK)),
                  _resident((N_FF_CHUNKS, FF_CHUNK, D_MODEL)), _resident((1, D_MODEL))],
        out_specs=[pl.BlockSpec((t, D_MODEL), rmap), pl.BlockSpec((1, 2 * g, 2 * D_FF), hmap)],
        out_shape=[jax.ShapeDtypeStruct((n, D_MODEL), F32),
                   jax.ShapeDtypeStruct((nseq, 2 * g, 2 * D_FF), F32)],
        scratch_shapes=[pltpu.VMEM((t, D_MODEL), BF16),
                        pltpu.VMEM((H + t, 2 * FF_CHUNK), F32), pltpu.VMEM((H + t, 2 * FF_CHUNK), F32),
                        pltpu.VMEM((t, FF_CHUNK), BF16), pltpu.VMEM((t, FF_CHUNK), BF16),
                        pltpu.VMEM((t, D_MODEL), F32),
                        pltpu.VMEM((N_FF_CHUNKS, 2 * g, 2 * FF_CHUNK), F32)],
        compiler_params=_params(_vmem_limit(blk, res + scr, tmp), 2),
        name="convffn",
    )(x2d, gn, halo, wup, wc, wdn, gfin)


def _prep_layer(l, g_mix, w_in, b_igate, b_fgate, g_mhead, a_sinks, w_a, w_b, w_out, g_ffn, w_up,
                w_conv, b_conv, w_down):
    w = w_in[l]
    seg = lambda i: w[:, _OFF[i]:_OFF[i + 1]]
    w_main = jnp.concatenate([seg(0), seg(1), seg(2), seg(3), seg(7), seg(9)], axis=1).astype(BF16)
    w_q = seg(6).astype(BF16)
    w_v = seg(8).astype(BF16)
    w_gate8 = jnp.concatenate([seg(4), seg(5)], axis=1)
    w_gate = jnp.pad(w_gate8, ((0, 0), (0, LANES_V7X - 2 * M_HEADS))).astype(BF16)
    w_gate_t = jnp.pad(w_gate8.T, ((0, SUBLANES_V7X), (0, 0))).astype(BF16)
    bias8 = jnp.concatenate([b_igate[l], b_fgate[l]]).astype(F32)
    brow = jnp.pad(bias8, (0, LANES_V7X - 2 * M_HEADS)).reshape(1, LANES_V7X)
    bcol = jnp.broadcast_to(bias8[:, None], (SUBLANES_V7X, LANES_V7X))
    sink8 = a_sinks[l].reshape(A_HEADS, 1).astype(F32)
    sinks = jnp.broadcast_to(sink8, (A_HEADS, LANES_V7X))
    sink_rows = jnp.broadcast_to(jnp.repeat(sink8, SAMPLE_PAD_T, axis=0), (A_HEADS * SAMPLE_PAD_T, LANES_V7X))
    up = w_up[l]
    wup = jnp.stack([jnp.concatenate([up[:, j * FF_CHUNK:(j + 1) * FF_CHUNK],
                                      up[:, D_FF + j * FF_CHUNK:D_FF + (j + 1) * FF_CHUNK]], axis=1)
                     for j in range(N_FF_CHUNKS)]).astype(BF16)
    cw = jnp.concatenate([w_conv[l], b_conv[l][None, :]], axis=0)
    cw = jnp.pad(cw, ((0, SUBLANES_V7X - CONV_W - 1), (0, 0))).astype(F32)
    wc = jnp.stack([jnp.concatenate([cw[:, j * FF_CHUNK:(j + 1) * FF_CHUNK],
                                     cw[:, D_FF + j * FF_CHUNK:D_FF + (j + 1) * FF_CHUNK]], axis=1)
                    for j in range(N_FF_CHUNKS)])
    wdn = w_down[l].reshape(N_FF_CHUNKS, FF_CHUNK, D_MODEL).astype(BF16)
    return dict(g_mix=g_mix[l].reshape(1, D_MODEL), w_main=w_main, w_q=w_q, w_v=w_v, w_q_t=w_q.T,
                w_v_t=w_v.T, w_gate=w_gate, w_gate_t=w_gate_t,
                brow=brow, bcol=bcol, gmh=g_mhead[l].reshape(1, M_WIDTH), sinks=sinks, sink_rows=sink_rows,
                wa=w_a[l].astype(BF16), wb=w_b[l].astype(BF16), wo=w_out[l].astype(BF16),
                g_ffn=g_ffn[l].reshape(1, D_MODEL), wup=wup, wc=wc, wdn=wdn)


def _rope_tables(pos):
    half = A_HEAD_DIM // 2
    inv = ROPE_THETA ** (-jnp.arange(0, A_HEAD_DIM, 2, dtype=F32) / A_HEAD_DIM)
    ang = pos.astype(F32)[:, None] * inv[None, :]
    ang = jnp.tile(jnp.concatenate([ang, ang], axis=-1), (1, LANES_V7X // A_HEAD_DIM))
    sign = jnp.where((jnp.arange(LANES_V7X) % A_HEAD_DIM) < half, -1.0, 1.0).astype(F32)
    return jnp.cos(ang), jnp.sin(ang) * sign[None, :]


def _rope_tables_t(pos):
    inv = ROPE_THETA ** (-jnp.arange(0, A_HEAD_DIM, 2, dtype=F32) / A_HEAD_DIM)
    ang = inv[:, None] * pos.astype(F32)[None, :]
    return jnp.cos(ang), jnp.sin(ang)


def _prompt_trunk(x, layers, g_final):
    B, S, _ = x.shape
    n = B * S
    x2d = x.reshape(n, D_MODEL)
    cos, sin = _rope_tables(jnp.arange(S, dtype=jnp.int32))
    cos_t, sin_t = _rope_tables_t(jnp.arange(S, dtype=jnp.int32))
    nchunks = S // M_CHUNK
    c0 = jnp.zeros((B, 2, LANES_V7X, LANES_V7X), F32)
    n0 = jnp.zeros((B, 2, LANES_V7X), F32)
    m0 = jnp.zeros((B, SUBLANES_V7X, LANES_V7X), F32)
    halo0 = jnp.zeros((B, CONV_W - 1, 2 * D_FF), F32)
    outs = dict(C=[], n=[], m=[], k=[], v=[], conv=[])
    for l, p in enumerate(layers):
        mq, mk, mv, mo, ak, gt, aq_t, av_t, gc, gr = _inproj(
            x2d, p["g_mix"], p["w_main"], p["w_q_t"], p["w_v_t"], p["w_gate"], p["w_gate_t"],
            cos, sin, cos_t, sin_t, S // ROW_TILE, True)
        hm, c_new, n_new, m_new = _mlstm(mq, mk, mv, gc, gr, p["brow"], p["bcol"], c0, n0, m0,
                                         nseq=B, nchunks=nchunks, chunk=M_CHUNK, real=M_CHUNK, bt=1,
                                         mm_dtype=BF16)
        ha = _swa_prompt(aq_t, ak, av_t, p["sinks"], nseq=B, nblk=S // WINDOW)
        x2d = _merge(hm, mo, ha, gt, x2d, p["gmh"], p["wa"], p["wb"], p["wo"])
        x2d, conv = _ffn(x2d, p["g_ffn"], halo0, p["wup"], p["wc"], p["wdn"], g_final,
                         nseq=B, t=ROW_TILE, g=1, final_norm=(l == DEPTH - 1))
        outs["C"].append(c_new.reshape(B, M_HEADS, M_QK_DIM, M_V_DIM))
        outs["n"].append(n_new.reshape(B, M_HEADS, M_QK_DIM))
        outs["m"].append(m_new[:, :M_HEADS, 0])
        outs["k"].append(ak.reshape(B, S, A_KV_HEADS, A_HEAD_DIM)[:, S - WINDOW:])
        v_last = av_t.reshape(KV_W, B, S)[:, :, S - WINDOW:]
        outs["v"].append(jnp.transpose(v_last, (1, 2, 0)).reshape(B, WINDOW, A_KV_HEADS, A_HEAD_DIM))
        outs["conv"].append(conv)
    return (x2d.reshape(B, S, D_MODEL),) + tuple(jnp.stack(outs[k]) for k in ("C", "n", "m", "k", "v", "conv"))


def _sample_trunk(x, st_c, st_n, st_m, st_k, st_v, st_conv, layers, g_final):
    B, T, _ = x.shape
    P = SAMPLE_PAD_T
    bt = LANES_V7X // P
    gb = SAMPLE_FFN_GROUP
    ng = B // gb
    pos =PAST_LEN + (jnp.arange(B * P, dtype=jnp.int32) % P)
    cos, sin = _rope_tables(pos)
    c_all = st_c.reshape(DEPTH, B, 2, LANES_V7X, LANES_V7X)
    n_all = st_n.reshape(DEPTH, B, 2, LANES_V7X)
    k_all = st_k.reshape(DEPTH * B, WINDOW, KV_W)
    v_all = st_v.reshape(DEPTH * B, WINDOW, KV_W)
    xpad = jnp.pad(x, ((0, 0), (0, P - T), (0, 0))).reshape(B * P, D_MODEL)
    outs = dict(C=[], n=[], m=[], k=[], v=[], conv=[])
    y_bm = None
    for l, p in enumerate(layers):
        mq, mk, mv, mo, ak, gt, aq, av, gc, gr = _inproj(
            xpad, p["g_mix"], p["w_main"], p["w_q"], p["w_v"], p["w_gate"], p["w_gate_t"],
            cos, sin, cos, sin, (B * P) // ROW_TILE, False)
        m0 = jnp.broadcast_to(jnp.pad(st_m[l], ((0, 0), (0, SUBLANES_V7X - M_HEADS)))[:, :, None],
                              (B, SUBLANES_V7X, LANES_V7X))
        hm, c_new, n_new, m_new = _mlstm(mq, mk, mv, gc, gr, p["brow"], p["bcol"], c_all[l], n_all[l], m0,
                                         nseq=B, nchunks=1, chunk=P, real=T, bt=bt, mm_dtype=F32)
        q5 = jnp.transpose(aq.reshape(B, P, A_KV_HEADS, A_GROUP, A_HEAD_DIM), (0, 2, 3, 1, 4))
        q5 = q5.reshape(B, A_KV_HEADS, A_GROUP * P, A_HEAD_DIM)
        zq = jnp.zeros_like(q5[:, 0])
        q_rows = jnp.concatenate([jnp.concatenate([q5[:, 0], zq], axis=-1),
                                  jnp.concatenate([zq, q5[:, 1]], axis=-1)], axis=1)
        o_rows, k_new, v_new = _swa_sample(q_rows, ak, av, k_all, v_all, p["sink_rows"],
                                           layer=l, nseq=B, bt=bt, real=T)
        o5 = jnp.stack([o_rows[:, :A_GROUP * P, :A_HEAD_DIM], o_rows[:, A_GROUP * P:, A_HEAD_DIM:]], axis=1)
        ha = jnp.transpose(o5.reshape(B, A_KV_HEADS, A_GROUP, P, A_HEAD_DIM),
                           (0, 3, 1, 2, 4)).reshape(B * P, A_WIDTH)
        xm = _merge(hm, mo, ha, gt, xpad, p["gmh"], p["wa"], p["wb"], p["wo"])
        x_tm = jnp.transpose(xm.reshape(ng, gb, P, D_MODEL)[:, :, :T], (0, 2, 1, 3)).reshape(T * B, D_MODEL)
        halo = jnp.transpose(st_conv[l].reshape(ng, gb, CONV_W - 1, 2 * D_FF),
                             (0, 2, 1, 3)).reshape(ng, (CONV_W - 1) * gb, 2 * D_FF)
        y_tm, conv = _ffn(x_tm, p["g_ffn"], halo, p["wup"], p["wc"], p["wdn"], g_final,
                          nseq=ng, t=T * gb, g=gb, final_norm=(l == DEPTH - 1))
        y_bm = jnp.transpose(y_tm.reshape(ng, T, gb, D_MODEL), (0, 2, 1, 3)).reshape(B, T, D_MODEL)
        xpad = jnp.pad(y_bm, ((0, 0), (0, P - T), (0, 0))).reshape(B * P, D_MODEL)
        outs["C"].append(c_new.reshape(B, M_HEADS, M_QK_DIM, M_V_DIM))
        outs["n"].append(n_new.reshape(B, M_HEADS, M_QK_DIM))
        outs["m"].append(m_new[:, :M_HEADS, 0])
        outs["k"].append(k_new.reshape(B, WINDOW, A_KV_HEADS, A_HEAD_DIM))
        outs["v"].append(v_new.reshape(B, WINDOW, A_KV_HEADS, A_HEAD_DIM))
        outs["conv"].append(jnp.transpose(conv.reshape(ng, CONV_W - 1, gb, 2 * D_FF),
                                          (0, 2, 1, 3)).reshape(B, CONV_W - 1, 2 * D_FF))
    return (y_bm,) + tuple(jnp.stack(outs[k]) for k in ("C", "n", "m", "k", "v", "conv"))


def kernel(x_prompt, x_sample, state_mlstm_C, state_mlstm_n, state_mlstm_m, cache_swa_k, cache_swa_v,
           state_ffn_conv, g_mix, w_in, b_igate, b_fgate, g_mhead, a_sinks, w_a, w_b, w_out, g_ffn,
           w_up, w_conv, b_conv, w_down, g_final):
    layers = [_prep_layer(l, g_mix, w_in, b_igate, b_fgate, g_mhead, a_sinks, w_a, w_b, w_out, g_ffn,
                          w_up, w_conv, b_conv, w_down) for l in range(DEPTH)]
    gfin = g_final.reshape(1, D_MODEL)
    yp, p_c, p_n, p_m, p_k, p_v, p_conv = _prompt_trunk(x_prompt, layers, gfin)
    ys, s_c, s_n, s_m, s_k, s_v, s_conv = _sample_trunk(
        x_sample, state_mlstm_C, state_mlstm_n, state_mlstm_m, cache_swa_k, cache_swa_v,
        state_ffn_conv, layers, gfin)
    return (yp, ys, p_c, p_n, p_m, p_k, p_v, p_conv, s_c, s_n, s_m, s_k, s_v, s_conv)
```

```python
import functools

import numpy as np
import jax
import jax.numpy as jnp
from jax import lax
from jax.experimental import pallas as pl
from jax.experimental.pallas import tpu as pltpu

F32 = jnp.float32
BF16 = jnp.bfloat16

D_MODEL = 1024
DEPTH = 2
PAST_LEN = 16384
M_HEADS = 4
M_QK_DIM = 64
M_V_DIM = 128
M_WIDTH = M_HEADS * M_V_DIM
M_CHUNK = 128
A_HEADS = 8
A_KV_HEADS = 2
A_GROUP = A_HEADS // A_KV_HEADS
A_HEAD_DIM = 64
A_WIDTH = A_HEADS * A_HEAD_DIM
WINDOW = 128
ROPE_THETA = 10000.0
D_FF = 2816
CONV_W = 3
EPS = 1e-6

QK_W = M_HEADS * M_QK_DIM
KV_W = A_KV_HEADS * A_HEAD_DIM
GATE_W = 2 * D_MODEL
_SPLIT = (QK_W, QK_W, M_WIDTH, M_WIDTH, M_HEADS, M_HEADS, A_WIDTH, KV_W, KV_W, GATE_W)
_OFF = np.concatenate([[0], np.cumsum(_SPLIT)]).tolist()
_MAIN_W = (QK_W, M_WIDTH, KV_W, GATE_W)
_MAIN_OFF = np.concatenate([[0], np.cumsum(_MAIN_W)]).tolist()
MAIN_COLS = _MAIN_OFF[-1]

LANES_V7X = 128
SUBLANES_V7X = 8
VMEM_BYTES_V7X = 64 * 1024 * 1024
VMEM_LIMIT_CAP_V7X = 60000 * 1024

SAMPLE_PAD_T = SUBLANES_V7X
FF_CHUNK = 2 * LANES_V7X
N_FF_CHUNKS = D_FF // FF_CHUNK
ROW_TILE = 512
SAMPLE_FFN_GROUP = 32


def _vmem_limit(block_bytes, scratch_bytes, temp_bytes):
    need = 2 * block_bytes + scratch_bytes + temp_bytes
    return int(min(max(need, 16 * 1024 * 1024), VMEM_LIMIT_CAP_V7X))


def _params(vmem_bytes, n_axes):
    return pltpu.CompilerParams(dimension_semantics=("arbitrary",) * n_axes,
                                vmem_limit_bytes=vmem_bytes)


def _resident(shape):
    nd = len(shape)
    return pl.BlockSpec(shape, lambda *_: (0,) * nd, pipeline_mode=pl.Buffered(1))


def _log_sigmoid(x):
    return jnp.minimum(x, 0.0) - jnp.log1p(jnp.exp(-jnp.abs(x)))


_NT = (((1,), (1,)), ((), ()))


def _inproj_kernel(x_ref, g_ref, w_ref, wmq_ref, wmv_ref, wq_ref, wv_ref, wg_ref, wgt_ref, cos_ref,
                   sin_ref, cosq_ref, sinq_ref, mk_ref, mo_ref, ak_ref, gt_ref, mq_ref, mv_ref, aq_ref,
                   av_ref, gc_ref, gr_ref, *, tokens_on_lanes):
    x = x_ref[...]
    r = lax.rsqrt(jnp.mean(x * x, axis=-1, keepdims=True) + EPS)
    xn = (x * r * g_ref[...]).astype(BF16)

    def proj(idx):
        return jnp.dot(xn, w_ref[:, _MAIN_OFF[idx]:_MAIN_OFF[idx + 1]], preferred_element_type=F32)

    mk_ref[...] = proj(0)
    mo_ref[...] = proj(1)
    gt_ref[...] = proj(3)
    if tokens_on_lanes:
        mq_ref[...] = lax.dot_general(wmq_ref[...], xn, _NT, preferred_element_type=F32) * (M_QK_DIM ** -0.5)
        mv_ref[...] = lax.dot_general(wmv_ref[...], xn, _NT, preferred_element_type=F32)
    else:
        mq_ref[...] = jnp.dot(xn, wmq_ref[...], preferred_element_type=F32) * (M_QK_DIM ** -0.5)
        mv_ref[...] = jnp.dot(xn, wmv_ref[...], preferred_element_type=F32)

    cos = cos_ref[...]
    sin = sin_ref[...]
    rows = x.shape[0]
    lane = lax.broadcasted_iota(jnp.int32, (rows, LANES_V7X), 1)
    first_half = (lane % A_HEAD_DIM) < (A_HEAD_DIM // 2)

    def rope(v, c, s):
        rot = jnp.where(first_half,
                        pltpu.roll(v, LANES_V7X - A_HEAD_DIM // 2, 1),
                        pltpu.roll(v, A_HEAD_DIM // 2, 1))
        return v * c + rot * s

    ak_ref[...] = rope(proj(2), cos, sin)

    scale = A_HEAD_DIM ** -0.5
    if tokens_on_lanes:
        half = A_HEAD_DIM // 2
        cq = cosq_ref[...]
        sq = sinq_ref[...]
        qt = lax.dot_general(wq_ref[...], xn, _NT, preferred_element_type=F32)
        for h in range(A_HEADS):
            x1 = qt[h * A_HEAD_DIM:h * A_HEAD_DIM + half]
            x2 = qt[h * A_HEAD_DIM + half:(h + 1) * A_HEAD_DIM]
            aq_ref[h * A_HEAD_DIM:h * A_HEAD_DIM + half, :] = (x1 * cq - x2 * sq) * scale
            aq_ref[h * A_HEAD_DIM + half:(h + 1) * A_HEAD_DIM, :] = (x2 * cq + x1 * sq) * scale
        av_ref[...] = lax.dot_general(wv_ref[...], xn, _NT, preferred_element_type=F32)
    else:
        cq = cosq_ref[...]
        sq = sinq_ref[...]
        aq = jnp.dot(xn, wq_ref[...], preferred_element_type=F32)
        for c in range(A_WIDTH // LANES_V7X):
            sl = slice(c * LANES_V7X, (c + 1) * LANES_V7X)
            aq_ref[:, sl] = rope(aq[:, sl], cq, sq) * scale
        av_ref[...] = jnp.dot(xn, wv_ref[...], preferred_element_type=F32)

    gc_ref[...] = jnp.dot(xn, wg_ref[...], preferred_element_type=F32)
    grt = lax.dot_general(wgt_ref[...], xn, _NT, preferred_element_type=F32)
    gr_ref[...] = grt[:SUBLANES_V7X]


_FLEX_W = (QK_W, M_WIDTH, A_WIDTH, KV_W)


def _inproj(x2d, g, w_main, w_flex, w_gate, w_gate_t, cos, sin, cos_q, sin_q, tiles_per_seq,
            tokens_on_lanes):
    n = x2d.shape[0]
    t = min(ROW_TILE, n)
    grid = (n // t,)
    row = lambda w: pl.BlockSpec((t, w), lambda i: (i, 0))
    col = lambda h: pl.BlockSpec((h, t), lambda i: (0, i))
    tab = pl.BlockSpec((t, LANES_V7X), lambda i: (i % tiles_per_seq, 0))
    if tokens_on_lanes:
        qtab = pl.BlockSpec((A_HEAD_DIM // 2, t), lambda i: (0, i % tiles_per_seq))
        flex_specs = [col(w) for w in _FLEX_W]
        flex_shapes = [(w, n) for w in _FLEX_W]
    else:
        qtab = tab
        flex_specs = [row(w) for w in _FLEX_W]
        flex_shapes = [(n, w) for w in _FLEX_W]
    out_shapes = [jax.ShapeDtypeStruct((n, w), F32) for w in _MAIN_W]
    out_shapes += [jax.ShapeDtypeStruct(s, F32) for s in flex_shapes]
    out_shapes += [jax.ShapeDtypeStruct((n, LANES_V7X), F32), jax.ShapeDtypeStruct((SUBLANES_V7X, n), F32)]
    out_specs = [row(w) for w in _MAIN_W] + flex_specs + [row(LANES_V7X), col(SUBLANES_V7X)]
    flex_cols = sum(_FLEX_W)
    blk = 4 * t * (D_MODEL + MAIN_COLS + flex_cols + 5 * LANES_V7X + SUBLANES_V7X)
    res = 2 * D_MODEL * (MAIN_COLS + flex_cols + LANES_V7X + 2 * SUBLANES_V7X)
    tmp = 4 * t * (GATE_W + D_MODEL)
    kern = functools.partial(_inproj_kernel, tokens_on_lanes=tokens_on_lanes)
    return pl.pallas_call(
        kern,
        grid=grid,
        in_specs=[row(D_MODEL), _resident((1, D_MODEL)), _resident((D_MODEL, MAIN_COLS))]
        + [_resident(w.shape) for w in w_flex]
        + [_resident((D_MODEL, LANES_V7X)), _resident((2 * SUBLANES_V7X, D_MODEL)), tab, tab, qtab, qtab],
        out_specs=out_specs,
        out_shape=out_shapes,
        compiler_params=_params(_vmem_limit(blk, res, tmp), 1),
        name="inproj",
    )(x2d, g, w_main, *w_flex, w_gate, w_gate_t, cos, sin, cos_q, sin_q)


def _mlstm_kernel(q_ref, k_ref, v_ref, gc_ref, gr_ref, brow_ref, bcol_ref, c0_ref, n0_ref, m0_ref,
                  h_ref, c_ref, n_ref, m_ref, *, chunk, real, bt, mm_dtype):
    L = chunk
    ci = pl.program_id(1)

    @pl.when(ci == 0)
    def _():
        c_ref[...] = c0_ref[...]
        n_ref[...] = n0_ref[...]
        m_ref[...] = m0_ref[...]

    def mx(a):
        return a.astype(mm_dtype)

    def split3(a):
        hi = a.astype(BF16).astype(F32)
        r1 = a - hi
        mid = r1.astype(BF16).astype(F32)
        lo = (r1 - mid).astype(BF16).astype(F32)
        return hi, mid, lo

    ti = lax.broadcasted_iota(jnp.int32, (L, L), 0)
    si = lax.broadcasted_iota(jnp.int32, (L, L), 1)
    causal = si <= ti
    tri = causal.astype(F32)
    tri_t = (ti <= si).astype(F32)
    lane_g = lax.broadcasted_iota(jnp.int32, (L, LANES_V7X), 1)
    row_g = lax.broadcasted_iota(jnp.int32, (L, LANES_V7X), 0)
    is_f_col = (lane_g >= M_HEADS) & (lane_g < 2 * M_HEADS)
    row_r = lax.broadcasted_iota(jnp.int32, (SUBLANES_V7X, L), 0)
    col_r = lax.broadcasted_iota(jnp.int32, (SUBLANES_V7X, L), 1)
    is_f_row = row_r >= M_HEADS
    lane_p = lax.broadcasted_iota(jnp.int32, (L, LANES_V7X), 1)
    half_masks = (lane_p < M_QK_DIM, lane_p >= M_QK_DIM)
    rows_c = lax.broadcasted_iota(jnp.int32, (2 * M_QK_DIM, 1), 0)
    lanes_n = lax.broadcasted_iota(jnp.int32, (1, LANES_V7X), 1)
    neg_inf = jnp.float32(-jnp.inf)

    units = [(bi, h) for bi in range(bt) for h in range(M_HEADS)]
    gates = []
    for bi in range(bt):
        rs = slice(bi * L, (bi + 1) * L)
        gcol = gc_ref[rs, :] + brow_ref[...]
        gcol = jnp.where(is_f_col, _log_sigmoid(gcol), gcol)
        fcol = jnp.where(is_f_col, gcol, 0.0)
        grow = gr_ref[:, rs] + bcol_ref[:, :L]
        grow = jnp.where(is_f_row, _log_sigmoid(grow), grow)
        frow = jnp.where(is_f_row, grow, 0.0)
        if real < L:
            gcol = jnp.where(row_g < real, gcol, neg_inf)
            fcol = jnp.where(row_g < real, fcol, 0.0)
            grow = jnp.where(col_r < real, grow, neg_inf)
            frow = jnp.where(col_r < real, frow, 0.0)
        cum_c = sum(jnp.dot(tri, p, preferred_element_type=F32) for p in split3(fcol))
        cum_r = sum(jnp.dot(p, tri_t, preferred_element_type=F32) for p in split3(frow))
        gates.append((gcol, grow, cum_c, cum_r))

    qhs, kps, vhs, s_raw, q_c = {}, {}, {}, {}, {}
    for bi, h in units:
        rs = slice(bi * L, (bi + 1) * L)
        pair, half = divmod(h, 2)
        ps = slice(pair * LANES_V7X, (pair + 1) * LANES_V7X)
        kp = k_ref[rs, ps]
        qh = jnp.where(half_masks[half], q_ref[rs, ps], 0.0)
        qhs[bi, h], kps[bi, h] = qh, kp
        vhs[bi, h] = v_ref[rs, h * M_V_DIM:(h + 1) * M_V_DIM]
        s_raw[bi, h] = lax.dot_general(mx(qh), mx(kp), _NT, preferred_element_type=F32)
        q_c[bi, h] = jnp.dot(mx(qh), mx(c_ref[bi, pair]), preferred_element_type=F32)

    s_w, kws, iscales, dens_, mrows, decays, new_m = {}, {}, {}, {}, {}, {}, {}
    for bi, h in units:
        gcol, grow, cum_c, cum_r = gates[bi]
        pair, half = divmod(h, 2)
        b_col = cum_c[:, M_HEADS + h:M_HEADS + h + 1]
        li_col = gcol[:, h:h + 1]
        b_row = cum_r[M_HEADS + h:M_HEADS + h + 1, :]
        li_row = grow[h:h + 1, :]
        m0 = m_ref[bi, h:h + 1, 0:1]
        npair = n_ref[bi, pair:pair + 1, :]
        dmat = jnp.where(causal, b_col - b_row + li_row, neg_inf)
        inter = b_col + m0
        mrow = jnp.maximum(inter, jnp.max(dmat, axis=-1, keepdims=True))
        iscale = jnp.exp(inter - mrow)
        s = s_raw[bi, h] * jnp.exp(dmat - mrow)
        qn = jnp.sum(qhs[bi, h] * npair, axis=-1, keepdims=True)
        dens_[bi, h] = iscale * qn + jnp.sum(s, axis=-1, keepdims=True)
        m_new = mrow[L - 1:L, :]
        b_last = b_col[L - 1:L, :]
        w_col = jnp.exp(b_last - b_col + li_col - m_new)
        decays[bi, h] = jnp.exp(b_last + m0 - m_new)
        kws[bi, h] = jnp.where(half_masks[half], kps[bi, h] * w_col, 0.0)
        s_w[bi, h], iscales[bi, h], mrows[bi, h], new_m[bi, h] = s, iscale, mrow, m_new

    d_c = {}
    for bi, h in units:
        rs = slice(bi * L, (bi + 1) * L)
        vh = vhs[bi, h]
        num = iscales[bi, h] * q_c[bi, h] + jnp.dot(mx(s_w[bi, h]), mx(vh), preferred_element_type=F32)
        h_ref[rs, h * M_V_DIM:(h + 1) * M_V_DIM] = \
            num / jnp.maximum(jnp.abs(dens_[bi, h]), jnp.exp(-mrows[bi, h]))
        d_c[bi, h] = lax.dot_general(mx(kws[bi, h]), mx(vh), (((0,), (0,)), ((), ())),
                                     preferred_element_type=F32)

    for bi in range(bt):
        for pair in range(M_HEADS // 2):
            h0, h1 = 2 * pair, 2 * pair + 1
            npair = n_ref[bi, pair:pair + 1, :]
            d_n = jnp.sum(kws[bi, h0], axis=0, keepdims=True) + jnp.sum(kws[bi, h1], axis=0, keepdims=True)
            c_ref[bi, pair] = jnp.where(rows_c < M_QK_DIM, decays[bi, h0], decays[bi, h1]) * c_ref[bi, pair] \
                + d_c[bi, h0] + d_c[bi, h1]
            n_ref[bi, pair:pair + 1, :] = \
                jnp.where(lanes_n < M_QK_DIM, decays[bi, h0], decays[bi, h1]) * npair + d_n
        for h in range(M_HEADS):
            m_ref[bi, h:h + 1, :] = jnp.broadcast_to(new_m[bi, h], (1, LANES_V7X))


def _mlstm(q, k, v, gc, gr, brow, bcol, c0, n0, m0, *, nseq, nchunks, chunk, real, bt, mm_dtype):
    rows = bt * chunk
    grid = (nseq // bt, nchunks)
    rmap = lambda b, c: (b * nchunks + c, 0)
    smap3 = lambda b, c: (b, 0, 0)
    smap4 = lambda b, c: (b, 0, 0, 0)
    c_spec = pl.BlockSpec((bt, 2, LANES_V7X, LANES_V7X), smap4)
    n_spec = pl.BlockSpec((bt, 2, LANES_V7X), smap3)
    m_spec = pl.BlockSpec((bt, SUBLANES_V7X, LANES_V7X), smap3)
    n = q.shape[0]
    blk = 4 * (rows * (2 * QK_W + 2 * M_WIDTH + LANES_V7X + SUBLANES_V7X)
               + 2 * bt * (2 * LANES_V7X * LANES_V7X + 2 * LANES_V7X + SUBLANES_V7X * LANES_V7X))
    tmp = 4 * 16 * chunk * max(chunk, LANES_V7X)
    kern = functools.partial(_mlstm_kernel, chunk=chunk, real=real, bt=bt, mm_dtype=mm_dtype)
    return pl.pallas_call(
        kern,
        grid=grid,
        in_specs=[pl.BlockSpec((rows, QK_W), rmap), pl.BlockSpec((rows, QK_W), rmap),
                  pl.BlockSpec((rows, M_WIDTH), rmap), pl.BlockSpec((rows, LANES_V7X), rmap),
                  pl.BlockSpec((SUBLANES_V7X, rows), lambda b, c: (0, b * nchunks + c)),
                  _resident((1, LANES_V7X)), _resident((SUBLANES_V7X, LANES_V7X)),
                  c_spec, n_spec, m_spec],
        out_specs=[pl.BlockSpec((rows, M_WIDTH), rmap), c_spec, n_spec, m_spec],
        out_shape=[jax.ShapeDtypeStruct((n, M_WIDTH), F32),
                   jax.ShapeDtypeStruct((nseq, 2, LANES_V7X, LANES_V7X), F32),
                   jax.ShapeDtypeStruct((nseq, 2, LANES_V7X), F32),
                   jax.ShapeDtypeStruct((nseq, SUBLANES_V7X, LANES_V7X), F32)],
        compiler_params=_params(_vmem_limit(blk, 0, tmp), 2),
        name="mlstm",
    )(q, k, v, gc, gr, brow, bcol, c0, n0, m0)


def _mlstm_t_kernel(qt_ref, k_ref, vt_ref, gc_ref, gr_ref, brow_ref, bcol_ref, c0_ref, n0_ref, m0_ref,
                    h_ref, c_ref, n_ref, m_ref):
    L = M_CHUNK
    ci = pl.program_id(1)

    @pl.when(ci == 0)
    def _():
        c_ref[...] = c0_ref[...]
        n_ref[...] = n0_ref[...]
        m_ref[...] = m0_ref[...]

    def split3(a):
        hi = a.astype(BF16).astype(F32)
        r1 = a - hi
        mid = r1.astype(BF16).astype(F32)
        lo = (r1 - mid).astype(BF16).astype(F32)
        return hi, mid, lo

    si = lax.broadcasted_iota(jnp.int32, (L, L), 0)
    ti = lax.broadcasted_iota(jnp.int32, (L, L), 1)
    causal = si <= ti
    tri = (ti <= si).astype(F32)
    tri_t = causal.astype(F32)
    lane_g = lax.broadcasted_iota(jnp.int32, (L, LANES_V7X), 1)
    is_f_col = (lane_g >= M_HEADS) & (lane_g < 2 * M_HEADS)
    row_r = lax.broadcasted_iota(jnp.int32, (SUBLANES_V7X, L), 0)
    is_f_row = row_r >= M_HEADS
    half_masks = (lane_g < M_QK_DIM, lane_g >= M_QK_DIM)
    lanes_n = lax.broadcasted_iota(jnp.int32, (1, LANES_V7X), 1)
    neg_inf = jnp.float32(-jnp.inf)
    PADR = 2 * SUBLANES_V7X

    gcol = gc_ref[...] + brow_ref[...]
    gcol = jnp.where(is_f_col, _log_sigmoid(gcol), gcol)
    grow = gr_ref[...] + bcol_ref[...]
    grow = jnp.where(is_f_row, _log_sigmoid(grow), grow)
    cum_c = sum(jnp.dot(tri, p, preferred_element_type=F32)
                for p in split3(jnp.where(is_f_col, gcol, 0.0)))
    cum_r = sum(jnp.dot(p, tri_t, preferred_element_type=F32)
                for p in split3(jnp.where(is_f_row, grow, 0.0)))

    zq = jnp.zeros((M_QK_DIM, L), BF16)
    zpad = jnp.zeros((PADR - 1, LANES_V7X), F32)
    kps, s_raw, inter = [], [], []
    for h in range(M_HEADS):
        pair, half = divmod(h, 2)
        kp = k_ref[:, pair * LANES_V7X:(pair + 1) * LANES_V7X]
        qh = qt_ref[h * M_QK_DIM:(h + 1) * M_QK_DIM, :].astype(BF16)
        qz = jnp.concatenate([qh, zq] if half == 0 else [zq, qh], axis=0)
        caug = jnp.concatenate([c_ref[0, pair], n_ref[0, pair:pair + 1, :], zpad], axis=0)
        kps.append(kp)
        s_raw.append(jnp.dot(kp.astype(BF16), qz, preferred_element_type=F32))
        inter.append(jnp.dot(caug.astype(BF16), qz, preferred_element_type=F32))
    s_w, iscales, dens_, m_rows, w_rows, decays = [], [], [], [], [], []
    for h in range(M_HEADS):
        b_row = cum_r[M_HEADS + h:M_HEADS + h + 1, :]
        li_row = grow[h:h + 1, :]
        d_col = gcol[:, h:h + 1] - cum_c[:, M_HEADS + h:M_HEADS + h + 1]
        m0 = m_ref[0, h:h + 1, 0:1]
        dmat = jnp.where(causal, b_row + d_col, neg_inf)
        inter_m = b_row + m0
        m_row = jnp.maximum(inter_m, jnp.max(dmat, axis=0, keepdims=True))
        s = s_raw[h] * jnp.exp(dmat - m_row)
        iscale = jnp.exp(inter_m - m_row)
        dens_.append(iscale * inter[h][M_V_DIM:M_V_DIM + 1, :] + jnp.sum(s, axis=0, keepdims=True))
        m_new = m_row[:, L - 1:L]
        b_last = b_row[:, L - 1:L]
        w_rows.append(jnp.exp(b_last - b_row + li_row - m_new))
        decays.append(jnp.exp(b_last + m0 - m_new))
        s_w.append(s.astype(BF16))
        iscales.append(iscale)
        m_rows.append(m_row)
    outs, upds = [], []
    for h in range(M_HEADS):
        pair, half = divmod(h, 2)
        vt = vt_ref[h * M_V_DIM:(h + 1) * M_V_DIM, :]
        num = iscales[h] * inter[h][:M_V_DIM, :] + jnp.dot(vt.astype(BF16), s_w[h], preferred_element_type=F32)
        outs.append(num / jnp.maximum(jnp.abs(dens_[h]), jnp.exp(-m_rows[h])))
        vaug = jnp.concatenate([vt * w_rows[h], w_rows[h], jnp.zeros((PADR - 1, L), F32)], axis=0)
        kmask = jnp.where(half_masks[half], kps[h], 0.0)
        upds.append(jnp.dot(vaug.astype(BF16), kmask.astype(BF16), preferred_element_type=F32))
    for pair in range(M_HEADS // 2):
        dl = jnp.where(lanes_n < M_QK_DIM, decays[2 * pair], decays[2 * pair + 1])
        u0, u1 = upds[2 * pair], upds[2 * pair + 1]
        n_old = n_ref[0, pair:pair + 1, :]
        c_ref[0, pair] = dl * c_ref[0, pair] + u0[:M_V_DIM] + u1[:M_V_DIM]
        n_ref[0, pair:pair + 1, :] = dl * n_old + u0[M_V_DIM:M_V_DIM + 1] + u1[M_V_DIM:M_V_DIM + 1]
    for h in range(M_HEADS):
        m_ref[0, h:h + 1, :] = jnp.broadcast_to(m_rows[h][:, L - 1:L], (1, LANES_V7X))
    h_ref[...] = jnp.concatenate(outs, axis=0).T


def _mlstm_t(qt, k, vt, gc, gr, brow, bcol, c0, n0, m0, *, nseq, nchunks):
    L = M_CHUNK
    n = k.shape[0]
    rmap = lambda b, c: (b * nchunks + c, 0)
    cmap = lambda b, c: (0, b * nchunks + c)
    smap3 = lambda b, c: (b, 0, 0)
    smap4 = lambda b, c: (b, 0, 0, 0)
    c_spec = pl.BlockSpec((1, 2, LANES_V7X, LANES_V7X), smap4)
    n_spec = pl.BlockSpec((1, 2, LANES_V7X), smap3)
    m_spec = pl.BlockSpec((1, SUBLANES_V7X, LANES_V7X), smap3)
    blk = 4 * (L * (2 * QK_W + 2 * M_WIDTH + LANES_V7X + SUBLANES_V7X)
               + 2 * (2 * LANES_V7X * LANES_V7X + 2 * LANES_V7X + SUBLANES_V7X * LANES_V7X))
    tmp = 4 * 48 * L * L
    return pl.pallas_call(
        _mlstm_t_kernel,
        grid=(nseq, nchunks),
        in_specs=[pl.BlockSpec((QK_W, L), cmap), pl.BlockSpec((L, QK_W), rmap),
                  pl.BlockSpec((M_WIDTH, L), cmap), pl.BlockSpec((L, LANES_V7X), rmap),
                  pl.BlockSpec((SUBLANES_V7X, L), cmap),
                  _resident((1, LANES_V7X)), _resident((SUBLANES_V7X, LANES_V7X)),
                  c_spec, n_spec, m_spec],
        out_specs=[pl.BlockSpec((L, M_WIDTH), rmap), c_spec, n_spec, m_spec],
        out_shape=[jax.ShapeDtypeStruct((n, M_WIDTH), F32),
                   jax.ShapeDtypeStruct((nseq, 2, LANES_V7X, LANES_V7X), F32),
                   jax.ShapeDtypeStruct((nseq, 2, LANES_V7X), F32),
                   jax.ShapeDtypeStruct((nseq, SUBLANES_V7X, LANES_V7X), F32)],
        compiler_params=_params(_vmem_limit(blk, 0, tmp), 2),
        name="mlstm_t",
    )(qt, k, vt, gc, gr, brow, bcol, c0, n0, m0)


def _swa_prompt_kernel(qt_ref, kc_ref, kp_ref, vtc_ref, vtp_ref, sink_ref, o_ref):
    blk = pl.program_id(1)
    L = WINDOW
    key = lax.broadcasted_iota(jnp.int32, (2 * L, L), 0)
    qry = lax.broadcasted_iota(jnp.int32, (2 * L, L), 1)
    prev_off = jnp.where(blk > 0, 0, L)
    mask = ((key < L) & (key >= qry + prev_off)) | ((key >= L) & (key - L <= qry))
    k2 = jnp.concatenate([kp_ref[...], kc_ref[...]], axis=0).astype(BF16)
    vt2 = jnp.concatenate([vtp_ref[...], vtc_ref[...]], axis=1).astype(BF16)
    zeros = jnp.zeros((A_HEAD_DIM, L), BF16)
    scores = []
    for hd in range(A_HEADS):
        qh = qt_ref[hd * A_HEAD_DIM:(hd + 1) * A_HEAD_DIM, :].astype(BF16)
        qz = jnp.concatenate([qh, zeros] if hd < A_GROUP else [zeros, qh], axis=0)
        scores.append(jnp.dot(k2, qz, preferred_element_type=F32))
    probs = []
    dens = []
    for hd in range(A_HEADS):
        s = jnp.where(mask, scores[hd], -jnp.inf)
        sink = sink_ref[hd:hd + 1, 0:1]
        m = jnp.maximum(jnp.max(s, axis=0, keepdims=True), sink)
        p = jnp.exp(s - m)
        dens.append(jnp.sum(p, axis=0, keepdims=True) + jnp.exp(sink - m))
        probs.append(p.astype(BF16))
    outs = []
    for hd in range(A_HEADS):
        kv = hd // A_GROUP
        o = jnp.dot(vt2[kv * A_HEAD_DIM:(kv + 1) * A_HEAD_DIM, :], probs[hd],
                    preferred_element_type=F32)
        outs.append(o / dens[hd])
    o_ref[...] = jnp.concatenate(outs, axis=0).T


def _swa_prompt(qt, k, vt, sinks, *, nseq, nblk):
    n = k.shape[0]
    L = WINDOW
    cur = lambda b, i: (b * nblk + i, 0)
    prev = lambda b, i: (b * nblk + jnp.maximum(i - 1, 0), 0)
    cur_t = lambda b, i: (0, b * nblk + i)
    prev_t = lambda b, i: (0, b * nblk + jnp.maximum(i - 1, 0))
    blk = 4 * L * (2 * A_WIDTH + 4 * KV_W)
    tmp = 4 * 24 * 2 * L * L
    return pl.pallas_call(
        _swa_prompt_kernel,
        grid=(nseq, nblk),
        in_specs=[pl.BlockSpec((A_WIDTH, L), cur_t),
                  pl.BlockSpec((L, KV_W), cur), pl.BlockSpec((L, KV_W), prev),
                  pl.BlockSpec((KV_W, L), cur_t), pl.BlockSpec((KV_W, L), prev_t),
                  _resident((A_HEADS, LANES_V7X))],
        out_specs=pl.BlockSpec((L, A_WIDTH), cur),
        out_shape=jax.ShapeDtypeStruct((n, A_WIDTH), F32),
        compiler_params=_params(_vmem_limit(blk, 0, tmp), 2),
        name="swa_prompt",
    )(qt, k, k, vt, vt, sinks)


def _swa_sample_kernel(q_ref, kn_ref, vn_ref, kc_ref, vc_ref, sink_ref, o_ref, ko_ref, vo_ref,
                       *, bt, real):
    P = SAMPLE_PAD_T
    W = WINDOW
    R = A_HEADS * P
    row = lax.broadcasted_iota(jnp.int32, (R, 2 * W), 0)
    key = lax.broadcasted_iota(jnp.int32, (R, 2 * W), 1)
    step = row % P
    mask = ((key < W) & (key >= step)) | ((key >= W) & (key - W <= step) & (key - W < real))
    sink = sink_ref[:, 0:1]
    pad = jnp.zeros((W - P, KV_W), F32)
    scores = []
    for bi in range(bt):
        kext = jnp.concatenate([kc_ref[bi], kn_ref[bi * P:(bi + 1) * P, :], pad], axis=0)
        scores.append(lax.dot_general(q_ref[bi], kext, _NT, preferred_element_type=F32))
    probs = []
    dens = []
    for bi in range(bt):
        s = jnp.where(mask, scores[bi], -jnp.inf)
        m = jnp.maximum(jnp.max(s, axis=-1, keepdims=True), sink)
        p = jnp.exp(s - m)
        dens.append(jnp.sum(p, axis=-1, keepdims=True) + jnp.exp(sink - m))
        probs.append(p)
    for bi in range(bt):
        rs = slice(bi * P, (bi + 1) * P)
        vn = vn_ref[rs, :]
        vext = jnp.concatenate([vc_ref[bi], vn, pad], axis=0)
        o_ref[bi] = jnp.dot(probs[bi], vext, preferred_element_type=F32) / dens[bi]
        ko_ref[bi, 0:W - real, :] = kc_ref[bi, real:W, :]
        ko_ref[bi, W - real:W, :] = kn_ref[bi * P:bi * P + real, :]
        vo_ref[bi, 0:W - real, :] = vc_ref[bi, real:W, :]
        vo_ref[bi, W - real:W, :] = vn[0:real, :]


def _swa_sample(q_rows, kn, vn, kcache, vcache, sinks, *, layer, nseq, bt, real):
    rows = bt * SAMPLE_PAD_T
    R = A_HEADS * SAMPLE_PAD_T
    nb = nseq // bt
    rmap = lambda i: (i, 0)
    cmap = lambda i: (layer * nb + i, 0, 0)
    omap = lambda i: (i, 0, 0)
    cache = pl.BlockSpec((bt, WINDOW, KV_W), cmap)
    blk = 4 * (2 * bt * R * KV_W + 2 * rows * KV_W + 4 * bt * WINDOW * KV_W)
    tmp = 4 * 16 * R * 2 * WINDOW
    kern = functools.partial(_swa_sample_kernel, bt=bt, real=real)
    return pl.pallas_call(
        kern,
        grid=(nb,),
        in_specs=[pl.BlockSpec((bt, R, KV_W), omap), pl.BlockSpec((rows, KV_W), rmap),
                  pl.BlockSpec((rows, KV_W), rmap), cache, cache, _resident((R, LANES_V7X))],
        out_specs=[pl.BlockSpec((bt, R, KV_W), omap),
                   pl.BlockSpec((bt, WINDOW, KV_W), omap), pl.BlockSpec((bt, WINDOW, KV_W), omap)],
        out_shape=[jax.ShapeDtypeStruct((nseq, R, KV_W), F32),
                   jax.ShapeDtypeStruct((nseq, WINDOW, KV_W), F32),
                   jax.ShapeDtypeStruct((nseq, WINDOW, KV_W), F32)],
        compiler_params=_params(_vmem_limit(blk, 0, tmp), 1),
        name="swa_sample",
    )(q_rows, kn, vn, kcache, vcache, sinks)


def _merge_kernel(hm_ref, mo_ref, ha_ref, gt_ref, x_ref, gmh_ref, wa_ref, wb_ref, wo_ref, o_ref):
    hm = hm_ref[...]
    parts = []
    for h in range(M_HEADS):
        v = hm[:, h * M_V_DIM:(h + 1) * M_V_DIM]
        parts.append(v * lax.rsqrt(jnp.mean(v * v, axis=-1, keepdims=True) + EPS))
    hn = jnp.concatenate(parts, axis=1) * gmh_ref[...] * jax.nn.sigmoid(mo_ref[...])
    a = jnp.dot(hn.astype(BF16), wa_ref[...], preferred_element_type=F32)
    b = jnp.dot(ha_ref[...].astype(BF16), wb_ref[...], preferred_element_type=F32)
    gates = jax.nn.sigmoid(gt_ref[...])
    mixed = gates[:, :D_MODEL] * a + gates[:, D_MODEL:] * b
    o_ref[...] = x_ref[...] + jnp.dot(mixed.astype(BF16), wo_ref[...], preferred_element_type=F32)


def _merge(hm, mo, ha, gt, x2d, gmh, wa, wb, wo):
    n = x2d.shape[0]
    t = min(ROW_TILE, n)
    row = lambda w: pl.BlockSpec((t, w), lambda i: (i, 0))
    blk = 4 * t * (3 * M_WIDTH + GATE_W + 2 * D_MODEL)
    res = 2 * (2 * M_WIDTH * D_MODEL + D_MODEL * D_MODEL)
    tmp = 4 * t * (GATE_W + 3 * D_MODEL)
    return pl.pallas_call(
        _merge_kernel,
        grid=(n // t,),
        in_specs=[row(M_WIDTH), row(M_WIDTH), row(A_WIDTH), row(GATE_W), row(D_MODEL),
                  _resident((1, M_WIDTH)), _resident((M_WIDTH, D_MODEL)),
                  _resident((A_WIDTH, D_MODEL)), _resident((D_MODEL, D_MODEL))],
        out_specs=row(D_MODEL),
        out_shape=jax.ShapeDtypeStruct((n, D_MODEL), F32),
        compiler_params=_params(_vmem_limit(blk, res, tmp), 1),
        name="merge",
    )(hm, mo, ha, gt, x2d, gmh, wa, wb, wo)


def _ffn_kernel(x_ref, gn_ref, halo_ref, wup_ref, wc_ref, wdn_ref, gfin_ref, o_ref, halo_out_ref,
                xn_s, u_a, u_b, act_a, act_b, acc_s, halo_s, *, g, final_norm):
    t = x_ref.shape[0]
    H = u_a.shape[0] - t
    ti = pl.program_id(1)
    last = pl.num_programs(1) - 1

    @pl.when(ti == 0)
    def _():
        for j in range(N_FF_CHUNKS):
            halo_s[j, :, :FF_CHUNK] = halo_ref[0, :, j * FF_CHUNK:(j + 1) * FF_CHUNK]
            halo_s[j, :, FF_CHUNK:] = halo_ref[0, :, D_FF + j * FF_CHUNK:D_FF + (j + 1) * FF_CHUNK]

    x = x_ref[...]
    xn_s[...] = (x * lax.rsqrt(jnp.mean(x * x, axis=-1, keepdims=True) + EPS) * gn_ref[...]).astype(BF16)

    def up(j, u_s):
        u_s[H:H + t, :] = jnp.dot(xn_s[...], wup_ref[j], preferred_element_type=F32)

    def conv_act(j, u_s, act_s):
        u_s[H - 2 * g:H, :] = halo_s[j]
        u = u_s[H:H + t, :]
        u1 = u_s[H - g:H - g + t, :]
        u2 = u_s[H - 2 * g:H - 2 * g + t, :]
        halo_s[j] = u_s[H + t - 2 * g:H + t, :]
        wc = wc_ref[j]
        c = wc[3:4, :] + wc[0:1, :] * u2 + wc[1:2, :] * u1 + wc[2:3, :] * u
        a = c[:, :FF_CHUNK]
        act = 0.5 * a * (1.0 + lax.erf(a * np.float32(np.sqrt(0.5)))) * c[:, FF_CHUNK:]
        act_s[...] = act.astype(BF16)

    def down(j, act_s, first=False):
        d = jnp.dot(act_s[...], wdn_ref[j], preferred_element_type=F32)
        if first:
            acc_s[...] = d
        else:
            acc_s[...] += d

    n = N_FF_CHUNKS
    assert n % 2 == 1 and n >= 5
    up(0, u_a)
    up(1, u_b)
    conv_act(0, u_a, act_a)
    up(2, u_a)
    conv_act(1, u_b, act_b)
    down(0, act_a, first=True)

    def tick_pair(p, carry):
        i = 3 + 2 * p
        up(i, u_b)
        conv_act(i - 1, u_a, act_a)
        down(i - 2, act_b)
        up(i + 1, u_a)
        conv_act(i, u_b, act_b)
        down(i - 1, act_a)
        return carry

    lax.fori_loop(0, (n - 3) // 2, tick_pair, 0)
    conv_act(n - 1, u_a, act_a)
    down(n - 2, act_b)
    down(n - 1, act_a)
    y = x + acc_s[...]
    if final_norm:
        y = y * lax.rsqrt(jnp.mean(y * y, axis=-1, keepdims=True) + EPS) * gfin_ref[...]
    o_ref[...] = y

    @pl.when(ti == last)
    def _():
        for j in range(N_FF_CHUNKS):
            halo_out_ref[0, :, j * FF_CHUNK:(j + 1) * FF_CHUNK] = halo_s[j, :, :FF_CHUNK]
            halo_out_ref[0, :, D_FF + j * FF_CHUNK:D_FF + (j + 1) * FF_CHUNK] = halo_s[j, :, FF_CHUNK:]


def _ffn(x2d, gn, halo, wup, wc, wdn, gfin, *, nseq, t, g, final_norm):
    n = x2d.shape[0]
    tiles = n // (nseq * t)
    H = -(-2 * g // SUBLANES_V7X) * SUBLANES_V7X
    rmap = lambda s, i: (s * tiles + i, 0)
    hmap = lambda s, i: (s, 0, 0)
    blk = 4 * (2 * t * D_MODEL + 2 * 2 * g * 2 * D_FF)
    res = 2 * (D_MODEL * 2 * D_FF + D_FF * D_MODEL) + 4 * N_FF_CHUNKS * SUBLANES_V7X * 2 * FF_CHUNK
    scr = 2 * t * D_MODEL + 2 * 4 * (H + t) * 2 * FF_CHUNK + 2 * 2 * t * FF_CHUNK + 4 * t * D_MODEL \
        + 4 * N_FF_CHUNKS * 2 * g * 2 * FF_CHUNK
    tmp = 4 * t * (6 * FF_CHUNK + D_MODEL)
    kern = functools.partial(_ffn_kernel, g=g, final_norm=final_norm)
    return pl.pallas_call(
        kern,
        grid=(nseq, tiles),
        in_specs=[pl.BlockSpec((t, D_MODEL), rmap), _resident((1, D_MODEL)),
                  pl.BlockSpec((1, 2 * g, 2 * D_FF), hmap),
                  _resident((N_FF_CHUNKS, D_MODEL, 2 * FF_CHUNK)),
                  _resident((N_FF_CHUNKS, SUBLANES_V7X, 2 * FF_CHUNK)),
                  _resident((N_FF_CHUNKS, FF_CHUNK, D_MODEL)), _resident((1, D_MODEL))],
        out_specs=[pl.BlockSpec((t, D_MODEL), rmap), pl.BlockSpec((1, 2 * g, 2 * D_FF), hmap)],
        out_shape=[jax.ShapeDtypeStruct((n, D_MODEL), F32),
                   jax.ShapeDtypeStruct((nseq, 2 * g, 2 * D_FF), F32)],
        scratch_shapes=[pltpu.VMEM((t, D_MODEL), BF16),
                        pltpu.VMEM((H + t, 2 * FF_CHUNK), F32), pltpu.VMEM((H + t, 2 * FF_CHUNK), F32),
                        pltpu.VMEM((t, FF_CHUNK), BF16), pltpu.VMEM((t, FF_CHUNK), BF16),
                        pltpu.VMEM((t, D_MODEL), F32),
                        pltpu.VMEM((N_FF_CHUNKS, 2 * g, 2 * FF_CHUNK), F32)],
        compiler_params=_params(_vmem_limit(blk, res + scr, tmp), 2),
        name="convffn",
    )(x2d, gn, halo, wup, wc, wdn, gfin)


def _prep_layer(l, g_mix, w_in, b_igate, b_fgate, g_mhead, a_sinks, w_a, w_b, w_out, g_ffn, w_up,
                w_conv, b_conv, w_down):
    w = w_in[l]
    seg = lambda i: w[:, _OFF[i]:_OFF[i + 1]]
    w_main = jnp.concatenate([seg(1), seg(3), seg(7), seg(9)], axis=1).astype(BF16)
    w_flex = tuple(seg(i).astype(BF16) for i in (0, 2, 6, 8))
    w_gate8 = jnp.concatenate([seg(4), seg(5)], axis=1)
    w_gate = jnp.pad(w_gate8, ((0, 0), (0, LANES_V7X - 2 * M_HEADS))).astype(BF16)
    w_gate_t = jnp.pad(w_gate8.T, ((0, SUBLANES_V7X), (0, 0))).astype(BF16)
    bias8 = jnp.concatenate([b_igate[l], b_fgate[l]]).astype(F32)
    brow = jnp.pad(bias8, (0, LANES_V7X - 2 * M_HEADS)).reshape(1, LANES_V7X)
    bcol = jnp.broadcast_to(bias8[:, None], (SUBLANES_V7X, LANES_V7X))
    sink8 = a_sinks[l].reshape(A_HEADS, 1).astype(F32)
    sinks = jnp.broadcast_to(sink8, (A_HEADS, LANES_V7X))
    sink_rows = jnp.broadcast_to(jnp.repeat(sink8, SAMPLE_PAD_T, axis=0), (A_HEADS * SAMPLE_PAD_T, LANES_V7X))
    up = w_up[l]
    wup = jnp.stack([jnp.concatenate([up[:, j * FF_CHUNK:(j + 1) * FF_CHUNK],
                                      up[:, D_FF + j * FF_CHUNK:D_FF + (j + 1) * FF_CHUNK]], axis=1)
                     for j in range(N_FF_CHUNKS)]).astype(BF16)
    cw = jnp.concatenate([w_conv[l], b_conv[l][None, :]], axis=0)
    cw = jnp.pad(cw, ((0, SUBLANES_V7X - CONV_W - 1), (0, 0))).astype(F32)
    wc = jnp.stack([jnp.concatenate([cw[:, j * FF_CHUNK:(j + 1) * FF_CHUNK],
                                     cw[:, D_FF + j * FF_CHUNK:D_FF + (j + 1) * FF_CHUNK]], axis=1)
                    for j in range(N_FF_CHUNKS)])
    wdn = w_down[l].reshape(N_FF_CHUNKS, FF_CHUNK, D_MODEL).astype(BF16)
    return dict(g_mix=g_mix[l].reshape(1, D_MODEL), w_main=w_main, w_flex=w_flex,
                w_flex_t=tuple(w.T for w in w_flex), w_gate=w_gate, w_gate_t=w_gate_t,
                brow=brow, bcol=bcol, gmh=g_mhead[l].reshape(1, M_WIDTH), sinks=sinks, sink_rows=sink_rows,
                wa=w_a[l].astype(BF16), wb=w_b[l].astype(BF16), wo=w_out[l].astype(BF16),
                g_ffn=g_ffn[l].reshape(1, D_MODEL), wup=wup, wc=wc, wdn=wdn)


def _rope_tables(pos):
    half = A_HEAD_DIM // 2
    inv = ROPE_THETA ** (-jnp.arange(0, A_HEAD_DIM, 2, dtype=F32) / A_HEAD_DIM)
    ang = pos.astype(F32)[:, None] * inv[None, :]
    ang = jnp.tile(jnp.concatenate([ang, ang], axis=-1), (1, LANES_V7X // A_HEAD_DIM))
    sign = jnp.where((jnp.arange(LANES_V7X) % A_HEAD_DIM) < half, -1.0, 1.0).astype(F32)
    return jnp.cos(ang), jnp.sin(ang) * sign[None, :]


def _rope_tables_t(pos):
    inv = ROPE_THETA ** (-jnp.arange(0, A_HEAD_DIM, 2, dtype=F32) / A_HEAD_DIM)
    ang = inv[:, None] * pos.astype(F32)[None, :]
    return jnp.cos(ang), jnp.sin(ang)


def _prompt_trunk(x, layers, g_final):
    B, S, _ = x.shape
    n = B * S
    x2d = x.reshape(n, D_MODEL)
    cos, sin = _rope_tables(jnp.arange(S, dtype=jnp.int32))
    cos_t, sin_t = _rope_tables_t(jnp.arange(S, dtype=jnp.int32))
    nchunks = S // M_CHUNK
    c0 = jnp.zeros((B, 2, LANES_V7X, LANES_V7X), F32)
    n0 = jnp.zeros((B, 2, LANES_V7X), F32)
    m0 = jnp.zeros((B, SUBLANES_V7X, LANES_V7X), F32)
    halo0 = jnp.zeros((B, CONV_W - 1, 2 * D_FF), F32)
    outs = dict(C=[], n=[], m=[], k=[], v=[], conv=[])
    for l, p in enumerate(layers):
        mk, mo, ak, gt, mq_t, mv_t, aq_t, av_t, gc, gr = _inproj(
            x2d, p["g_mix"], p["w_main"], p["w_flex_t"], p["w_gate"], p["w_gate_t"],
            cos, sin, cos_t, sin_t, S // ROW_TILE, True)
        hm, c_new, n_new, m_new = _mlstm_t(mq_t, mk, mv_t, gc, gr, p["brow"], p["bcol"], c0, n0, m0,
                                           nseq=B, nchunks=nchunks)
        ha = _swa_prompt(aq_t, ak, av_t, p["sinks"], nseq=B, nblk=S // WINDOW)
        x2d = _merge(hm, mo, ha, gt, x2d, p["gmh"], p["wa"], p["wb"], p["wo"])
        x2d, conv = _ffn(x2d, p["g_ffn"], halo0, p["wup"], p["wc"], p["wdn"], g_final,
                         nseq=B, t=ROW_TILE, g=1, final_norm=(l == DEPTH - 1))
        c_new = jnp.transpose(c_new.reshape(B, M_HEADS // 2, M_V_DIM, 2, M_QK_DIM), (0, 1, 3, 4, 2))
        outs["C"].append(c_new.reshape(B, M_HEADS, M_QK_DIM, M_V_DIM))
        outs["n"].append(n_new.reshape(B, M_HEADS, M_QK_DIM))
        outs["m"].append(m_new[:, :M_HEADS, 0])
        outs["k"].append(ak.reshape(B, S, A_KV_HEADS, A_HEAD_DIM)[:, S - WINDOW:])
        v_last = av_t.reshape(KV_W, B, S)[:, :, S - WINDOW:]
        outs["v"].append(jnp.transpose(v_last, (1, 2, 0)).reshape(B, WINDOW, A_KV_HEADS, A_HEAD_DIM))
        outs["conv"].append(conv)
    return (x2d.reshape(B, S, D_MODEL),) + tuple(jnp.stack(outs[k]) for k in ("C", "n", "m", "k", "v", "conv"))


def _sample_trunk(x, st_c, st_n, st_m, st_k, st_v, st_conv, layers, g_final):
    B, T, _ = x.shape
    P = SAMPLE_PAD_T
    bt = LANES_V7X // P
    gb = SAMPLE_FFN_GROUP
    ng = B // gb
    pos =PAST_LEN + (jnp.arange(B * P, dtype=jnp.int32) % P)
    cos, sin = _rope_tables(pos)
    c_all = st_c.reshape(DEPTH, B, 2, LANES_V7X, LANES_V7X)
    n_all = st_n.reshape(DEPTH, B, 2, LANES_V7X)
    k_all = st_k.reshape(DEPTH * B, WINDOW, KV_W)
    v_all = st_v.reshape(DEPTH * B, WINDOW, KV_W)
    xpad = jnp.pad(x, ((0, 0), (0, P - T), (0, 0))).reshape(B * P, D_MODEL)
    outs = dict(C=[], n=[], m=[], k=[], v=[], conv=[])
    y_bm = None
    for l, p in enumerate(layers):
        mk, mo, ak, gt, mq, mv, aq, av, gc, gr = _inproj(
            xpad, p["g_mix"], p["w_main"], p["w_flex"], p["w_gate"], p["w_gate_t"],
            cos, sin, cos, sin, (B * P) // ROW_TILE, False)
        m0 = jnp.broadcast_to(jnp.pad(st_m[l], ((0, 0), (0, SUBLANES_V7X - M_HEADS)))[:, :, None],
                              (B, SUBLANES_V7X, LANES_V7X))
        hm, c_new, n_new, m_new = _mlstm(mq, mk, mv, gc, gr, p["brow"], p["bcol"], c_all[l], n_all[l], m0,
                                         nseq=B, nchunks=1, chunk=P, real=T, bt=bt, mm_dtype=F32)
        q5 = jnp.transpose(aq.reshape(B, P, A_KV_HEADS, A_GROUP, A_HEAD_DIM), (0, 2, 3, 1, 4))
        q5 = q5.reshape(B, A_KV_HEADS, A_GROUP * P, A_HEAD_DIM)
        zq = jnp.zeros_like(q5[:, 0])
        q_rows = jnp.concatenate([jnp.concatenate([q5[:, 0], zq], axis=-1),
                                  jnp.concatenate([zq, q5[:, 1]], axis=-1)], axis=1)
        o_rows, k_new, v_new = _swa_sample(q_rows, ak, av, k_all, v_all, p["sink_rows"],
                                           layer=l, nseq=B, bt=bt, real=T)
        o5 = jnp.stack([o_rows[:, :A_GROUP * P, :A_HEAD_DIM], o_rows[:, A_GROUP * P:, A_HEAD_DIM:]], axis=1)
        ha = jnp.transpose(o5.reshape(B, A_KV_HEADS, A_GROUP, P, A_HEAD_DIM),
                           (0, 3, 1, 2, 4)).reshape(B * P, A_WIDTH)
        xm = _merge(hm, mo, ha, gt, xpad, p["gmh"], p["wa"], p["wb"], p["wo"])
        x_tm = jnp.transpose(xm.reshape(ng, gb, P, D_MODEL)[:, :, :T], (0, 2, 1, 3)).reshape(T * B, D_MODEL)
        halo = jnp.transpose(st_conv[l].reshape(ng, gb, CONV_W - 1, 2 * D_FF),
                             (0, 2, 1, 3)).reshape(ng, (CONV_W - 1) * gb, 2 * D_FF)
        y_tm, conv = _ffn(x_tm, p["g_ffn"], halo, p["wup"], p["wc"], p["wdn"], g_final,
                          nseq=ng, t=T * gb, g=gb, final_norm=(l == DEPTH - 1))
        y_bm = jnp.transpose(y_tm.reshape(ng, T, gb, D_MODEL), (0, 2, 1, 3)).reshape(B, T, D_MODEL)
        xpad = jnp.pad(y_bm, ((0, 0), (0, P - T), (0, 0))).reshape(B * P, D_MODEL)
        outs["C"].append(c_new.reshape(B, M_HEADS, M_QK_DIM, M_V_DIM))
        outs["n"].append(n_new.reshape(B, M_HEADS, M_QK_DIM))
        outs["m"].append(m_new[:, :M_HEADS, 0])
        outs["k"].append(k_new.reshape(B, WINDOW, A_KV_HEADS, A_HEAD_DIM))
        outs["v"].append(v_new.reshape(B, WINDOW, A_KV_HEADS, A_HEAD_DIM))
        outs["conv"].append(jnp.transpose(conv.reshape(ng, CONV_W - 1, gb, 2 * D_FF),
                                          (0, 2, 1, 3)).reshape(B, CONV_W - 1, 2 * D_FF))
    return (y_bm,) + tuple(jnp.stack(outs[k]) for k in ("C", "n", "m", "k", "v", "conv"))


def kernel(x_prompt, x_sample, state_mlstm_C, state_mlstm_n, state_mlstm_m, cache_swa_k, cache_swa_v,
           state_ffn_conv, g_mix, w_in, b_igate, b_fgate, g_mhead, a_sinks, w_a, w_b, w_out, g_ffn,
           w_up, w_conv, b_conv, w_down, g_final):
    layers = [_prep_layer(l, g_mix, w_in, b_igate, b_fgate, g_mhead, a_sinks, w_a, w_b, w_out, g_ffn,
                          w_up, w_conv, b_conv, w_down) for l in range(DEPTH)]
    gfin = g_final.reshape(1, D_MODEL)
    yp, p_c, p_n, p_m, p_k, p_v, p_conv = _prompt_trunk(x_prompt, layers, gfin)
    ys, s_c, s_n, s_m, s_k, s_v, s_conv = _sample_trunk(
        x_sample, state_mlstm_C, state_mlstm_n, state_mlstm_m, cache_swa_k, cache_swa_v,
        state_ffn_conv, layers, gfin)
    return (yp, ys, p_c, p_n, p_m, p_k, p_v, p_conv, s_c, s_n, s_m, s_k, s_v, s_conv)
```

```python
import functools

import numpy as np
import jax
import jax.numpy as jnp
from jax import lax
from jax.experimental import pallas as pl
from jax.experimental.pallas import tpu as pltpu

F32 = jnp.float32
BF16 = jnp.bfloat16

D_MODEL = 1024
DEPTH = 2
PAST_LEN = 16384
M_HEADS = 4
M_QK_DIM = 64
M_V_DIM = 128
M_WIDTH = M_HEADS * M_V_DIM
M_CHUNK = 128
A_HEADS = 8
A_KV_HEADS = 2
A_GROUP = A_HEADS // A_KV_HEADS
A_HEAD_DIM = 64
A_WIDTH = A_HEADS * A_HEAD_DIM
WINDOW = 128
ROPE_THETA = 10000.0
D_FF = 2816
CONV_W = 3
EPS = 1e-6

QK_W = M_HEADS * M_QK_DIM
KV_W = A_KV_HEADS * A_HEAD_DIM
GATE_W = 2 * D_MODEL
_SPLIT = (QK_W, QK_W, M_WIDTH, M_WIDTH, M_HEADS, M_HEADS, A_WIDTH, KV_W, KV_W, GATE_W)
_OFF = np.concatenate([[0], np.cumsum(_SPLIT)]).tolist()
_MAIN_W = (QK_W, M_WIDTH, KV_W, GATE_W)
_MAIN_OFF = np.concatenate([[0], np.cumsum(_MAIN_W)]).tolist()
MAIN_COLS = _MAIN_OFF[-1]

LANES_V7X = 128
SUBLANES_V7X = 8
VMEM_BYTES_V7X = 64 * 1024 * 1024
VMEM_LIMIT_CAP_V7X = 60000 * 1024

SAMPLE_PAD_T = SUBLANES_V7X
FF_CHUNK = 2 * LANES_V7X
N_FF_CHUNKS = D_FF // FF_CHUNK
ROW_TILE = 512
SAMPLE_FFN_GROUP = 32


def _vmem_limit(block_bytes, scratch_bytes, temp_bytes):
    need = 2 * block_bytes + scratch_bytes + temp_bytes
    assert need <= VMEM_LIMIT_CAP_V7X, need
    return VMEM_LIMIT_CAP_V7X


def _params(vmem_bytes, n_axes):
    return pltpu.CompilerParams(dimension_semantics=("arbitrary",) * n_axes,
                                vmem_limit_bytes=vmem_bytes)


def _resident(shape):
    nd = len(shape)
    return pl.BlockSpec(shape, lambda *_: (0,) * nd, pipeline_mode=pl.Buffered(1))


def _log_sigmoid(x):
    return jnp.minimum(x, 0.0) - jnp.log1p(jnp.exp(-jnp.abs(x)))


_NT = (((1,), (1,)), ((), ()))


def _inproj_kernel(x_ref, g_ref, w_ref, wmq_ref, wmv_ref, wq_ref, wv_ref, wg_ref, wgt_ref, cos_ref,
                   sin_ref, cosq_ref, sinq_ref, mk_ref, mo_ref, ak_ref, gt_ref, mq_ref, mv_ref, aq_ref,
                   av_ref, gc_ref, gr_ref, *, tokens_on_lanes):
    x = x_ref[...]
    r = lax.rsqrt(jnp.mean(x * x, axis=-1, keepdims=True) + EPS)
    xn = (x * r * g_ref[...]).astype(BF16)

    def proj(idx):
        return jnp.dot(xn, w_ref[:, _MAIN_OFF[idx]:_MAIN_OFF[idx + 1]], preferred_element_type=F32)

    mk_ref[...] = proj(0)
    mo_ref[...] = proj(1)
    gt_ref[...] = proj(3)
    if tokens_on_lanes:
        mq_ref[...] = lax.dot_general(wmq_ref[...], xn, _NT, preferred_element_type=F32) * (M_QK_DIM ** -0.5)
        mv_ref[...] = lax.dot_general(wmv_ref[...], xn, _NT, preferred_element_type=F32)
    else:
        mq_ref[...] = jnp.dot(xn, wmq_ref[...], preferred_element_type=F32) * (M_QK_DIM ** -0.5)
        mv_ref[...] = jnp.dot(xn, wmv_ref[...], preferred_element_type=F32)

    cos = cos_ref[...]
    sin = sin_ref[...]
    rows = x.shape[0]
    lane = lax.broadcasted_iota(jnp.int32, (rows, LANES_V7X), 1)
    first_half = (lane % A_HEAD_DIM) < (A_HEAD_DIM // 2)

    def rope(v, c, s):
        rot = jnp.where(first_half,
                        pltpu.roll(v, LANES_V7X - A_HEAD_DIM // 2, 1),
                        pltpu.roll(v, A_HEAD_DIM // 2, 1))
        return v * c + rot * s

    ak_ref[...] = rope(proj(2), cos, sin)

    scale = A_HEAD_DIM ** -0.5
    if tokens_on_lanes:
        half = A_HEAD_DIM // 2
        cq = cosq_ref[...]
        sq = sinq_ref[...]
        qt = lax.dot_general(wq_ref[...], xn, _NT, preferred_element_type=F32)
        for h in range(A_HEADS):
            x1 = qt[h * A_HEAD_DIM:h * A_HEAD_DIM + half]
            x2 = qt[h * A_HEAD_DIM + half:(h + 1) * A_HEAD_DIM]
            aq_ref[h * A_HEAD_DIM:h * A_HEAD_DIM + half, :] = (x1 * cq - x2 * sq) * scale
            aq_ref[h * A_HEAD_DIM + half:(h + 1) * A_HEAD_DIM, :] = (x2 * cq + x1 * sq) * scale
        av_ref[...] = lax.dot_general(wv_ref[...], xn, _NT, preferred_element_type=F32)
    else:
        cq = cosq_ref[...]
        sq = sinq_ref[...]
        aq = jnp.dot(xn, wq_ref[...], preferred_element_type=F32)
        for c in range(A_WIDTH // LANES_V7X):
            sl = slice(c * LANES_V7X, (c + 1) * LANES_V7X)
            aq_ref[:, sl] = rope(aq[:, sl], cq, sq) * scale
        av_ref[...] = jnp.dot(xn, wv_ref[...], preferred_element_type=F32)

    gc_ref[...] = jnp.dot(xn, wg_ref[...], preferred_element_type=F32)
    grt = lax.dot_general(wgt_ref[...], xn, _NT, preferred_element_type=F32)
    gr_ref[...] = grt[:SUBLANES_V7X]


_FLEX_W = (QK_W, M_WIDTH, A_WIDTH, KV_W)


def _inproj(x2d, g, w_main, w_flex, w_gate, w_gate_t, cos, sin, cos_q, sin_q, tiles_per_seq,
            tokens_on_lanes):
    n = x2d.shape[0]
    t = min(ROW_TILE, n)
    grid = (n // t,)
    row = lambda w: pl.BlockSpec((t, w), lambda i: (i, 0))
    col = lambda h: pl.BlockSpec((h, t), lambda i: (0, i))
    tab = pl.BlockSpec((t, LANES_V7X), lambda i: (i % tiles_per_seq, 0))
    if tokens_on_lanes:
        qtab = pl.BlockSpec((A_HEAD_DIM // 2, t), lambda i: (0, i % tiles_per_seq))
        flex_specs = [col(w) for w in _FLEX_W]
        flex_shapes = [(w, n) for w in _FLEX_W]
    else:
        qtab = tab
        flex_specs = [row(w) for w in _FLEX_W]
        flex_shapes = [(n, w) for w in _FLEX_W]
    out_shapes = [jax.ShapeDtypeStruct((n, w), F32) for w in _MAIN_W]
    out_shapes += [jax.ShapeDtypeStruct(s, F32) for s in flex_shapes]
    out_shapes += [jax.ShapeDtypeStruct((n, LANES_V7X), F32), jax.ShapeDtypeStruct((SUBLANES_V7X, n), F32)]
    out_specs = [row(w) for w in _MAIN_W] + flex_specs + [row(LANES_V7X), col(SUBLANES_V7X)]
    flex_cols = sum(_FLEX_W)
    blk = 4 * t * (D_MODEL + MAIN_COLS + flex_cols + 5 * LANES_V7X + SUBLANES_V7X)
    res = 2 * D_MODEL * (MAIN_COLS + flex_cols + LANES_V7X + 2 * SUBLANES_V7X)
    tmp = 4 * t * (GATE_W + D_MODEL)
    kern = functools.partial(_inproj_kernel, tokens_on_lanes=tokens_on_lanes)
    return pl.pallas_call(
        kern,
        grid=grid,
        in_specs=[row(D_MODEL), _resident((1, D_MODEL)), _resident((D_MODEL, MAIN_COLS))]
        + [_resident(w.shape) for w in w_flex]
        + [_resident((D_MODEL, LANES_V7X)), _resident((2 * SUBLANES_V7X, D_MODEL)), tab, tab, qtab, qtab],
        out_specs=out_specs,
        out_shape=out_shapes,
        compiler_params=_params(_vmem_limit(blk, res, tmp), 1),
        name="inproj",
    )(x2d, g, w_main, *w_flex, w_gate, w_gate_t, cos, sin, cos_q, sin_q)


def _mlstm_kernel(q_ref, k_ref, v_ref, gc_ref, gr_ref, brow_ref, bcol_ref, c0_ref, n0_ref, m0_ref,
                  h_ref, c_ref, n_ref, m_ref, *, chunk, real, bt, mm_dtype):
    L = chunk
    ci = pl.program_id(1)

    @pl.when(ci == 0)
    def _():
        c_ref[...] = c0_ref[...]
        n_ref[...] = n0_ref[...]
        m_ref[...] = m0_ref[...]

    def mx(a):
        return a.astype(mm_dtype)

    def split3(a):
        hi = a.astype(BF16).astype(F32)
        r1 = a - hi
        mid = r1.astype(BF16).astype(F32)
        lo = (r1 - mid).astype(BF16).astype(F32)
        return hi, mid, lo

    ti = lax.broadcasted_iota(jnp.int32, (L, L), 0)
    si = lax.broadcasted_iota(jnp.int32, (L, L), 1)
    causal = si <= ti
    tri = causal.astype(F32)
    tri_t = (ti <= si).astype(F32)
    lane_g = lax.broadcasted_iota(jnp.int32, (L, LANES_V7X), 1)
    row_g = lax.broadcasted_iota(jnp.int32, (L, LANES_V7X), 0)
    is_f_col = (lane_g >= M_HEADS) & (lane_g < 2 * M_HEADS)
    row_r = lax.broadcasted_iota(jnp.int32, (SUBLANES_V7X, L), 0)
    col_r = lax.broadcasted_iota(jnp.int32, (SUBLANES_V7X, L), 1)
    is_f_row = row_r >= M_HEADS
    lane_p = lax.broadcasted_iota(jnp.int32, (L, LANES_V7X), 1)
    half_masks = (lane_p < M_QK_DIM, lane_p >= M_QK_DIM)
    rows_c = lax.broadcasted_iota(jnp.int32, (2 * M_QK_DIM, 1), 0)
    lanes_n = lax.broadcasted_iota(jnp.int32, (1, LANES_V7X), 1)
    neg_inf = jnp.float32(-jnp.inf)

    units = [(bi, h) for bi in range(bt) for h in range(M_HEADS)]
    gates = []
    for bi in range(bt):
        rs = slice(bi * L, (bi + 1) * L)
        gcol = gc_ref[rs, :] + brow_ref[...]
        gcol = jnp.where(is_f_col, _log_sigmoid(gcol), gcol)
        fcol = jnp.where(is_f_col, gcol, 0.0)
        grow = gr_ref[:, rs] + bcol_ref[:, :L]
        grow = jnp.where(is_f_row, _log_sigmoid(grow), grow)
        frow = jnp.where(is_f_row, grow, 0.0)
        if real < L:
            gcol = jnp.where(row_g < real, gcol, neg_inf)
            fcol = jnp.where(row_g < real, fcol, 0.0)
            grow = jnp.where(col_r < real, grow, neg_inf)
            frow = jnp.where(col_r < real, frow, 0.0)
        cum_c = sum(jnp.dot(tri, p, preferred_element_type=F32) for p in split3(fcol))
        cum_r = sum(jnp.dot(p, tri_t, preferred_element_type=F32) for p in split3(frow))
        gates.append((gcol, grow, cum_c, cum_r))

    qhs, kps, vhs, s_raw, q_c = {}, {}, {}, {}, {}
    for bi, h in units:
        rs = slice(bi * L, (bi + 1) * L)
        pair, half = divmod(h, 2)
        ps = slice(pair * LANES_V7X, (pair + 1) * LANES_V7X)
        kp = k_ref[rs, ps]
        qh = jnp.where(half_masks[half], q_ref[rs, ps], 0.0)
        qhs[bi, h], kps[bi, h] = qh, kp
        vhs[bi, h] = v_ref[rs, h * M_V_DIM:(h + 1) * M_V_DIM]
        s_raw[bi, h] = lax.dot_general(mx(qh), mx(kp), _NT, preferred_element_type=F32)
        q_c[bi, h] = jnp.dot(mx(qh), mx(c_ref[bi, pair]), preferred_element_type=F32)

    s_w, kws, iscales, dens_, mrows, decays, new_m = {}, {}, {}, {}, {}, {}, {}
    for bi, h in units:
        gcol, grow, cum_c, cum_r = gates[bi]
        pair, half = divmod(h, 2)
        b_col = cum_c[:, M_HEADS + h:M_HEADS + h + 1]
        li_col = gcol[:, h:h + 1]
        b_row = cum_r[M_HEADS + h:M_HEADS + h + 1, :]
        li_row = grow[h:h + 1, :]
        m0 = m_ref[bi, h:h + 1, 0:1]
        npair = n_ref[bi, pair:pair + 1, :]
        dmat = jnp.where(causal, b_col - b_row + li_row, neg_inf)
        inter = b_col + m0
        mrow = jnp.maximum(inter, jnp.max(dmat, axis=-1, keepdims=True))
        iscale = jnp.exp(inter - mrow)
        s = s_raw[bi, h] * jnp.exp(dmat - mrow)
        qn = jnp.sum(qhs[bi, h] * npair, axis=-1, keepdims=True)
        dens_[bi, h] = iscale * qn + jnp.sum(s, axis=-1, keepdims=True)
        m_new = mrow[L - 1:L, :]
        b_last = b_col[L - 1:L, :]
        w_col = jnp.exp(b_last - b_col + li_col - m_new)
        decays[bi, h] = jnp.exp(b_last + m0 - m_new)
        kws[bi, h] = jnp.where(half_masks[half], kps[bi, h] * w_col, 0.0)
        s_w[bi, h], iscales[bi, h], mrows[bi, h], new_m[bi, h] = s, iscale, mrow, m_new

    d_c = {}
    for bi, h in units:
        rs = slice(bi * L, (bi + 1) * L)
        vh = vhs[bi, h]
        num = iscales[bi, h] * q_c[bi, h] + jnp.dot(mx(s_w[bi, h]), mx(vh), preferred_element_type=F32)
        h_ref[rs, h * M_V_DIM:(h + 1) * M_V_DIM] = \
            num / jnp.maximum(jnp.abs(dens_[bi, h]), jnp.exp(-mrows[bi, h]))
        d_c[bi, h] = lax.dot_general(mx(kws[bi, h]), mx(vh), (((0,), (0,)), ((), ())),
                                     preferred_element_type=F32)

    for bi in range(bt):
        for pair in range(M_HEADS // 2):
            h0, h1 = 2 * pair, 2 * pair + 1
            npair = n_ref[bi, pair:pair + 1, :]
            d_n = jnp.sum(kws[bi, h0], axis=0, keepdims=True) + jnp.sum(kws[bi, h1], axis=0, keepdims=True)
            c_ref[bi, pair] = jnp.where(rows_c < M_QK_DIM, decays[bi, h0], decays[bi, h1]) * c_ref[bi, pair] \
                + d_c[bi, h0] + d_c[bi, h1]
            n_ref[bi, pair:pair + 1, :] = \
                jnp.where(lanes_n < M_QK_DIM, decays[bi, h0], decays[bi, h1]) * npair + d_n
        for h in range(M_HEADS):
            m_ref[bi, h:h + 1, :] = jnp.broadcast_to(new_m[bi, h], (1, LANES_V7X))


def _mlstm(q, k, v, gc, gr, brow, bcol, c0, n0, m0, *, layer, nseq, nchunks, chunk, real, bt, mm_dtype):
    rows = bt * chunk
    grid = (nseq // bt, nchunks)
    nb = nseq // bt
    rmap = lambda b, c: (b * nchunks + c, 0)
    smap3 = lambda b, c: (b, 0, 0)
    smap4 = lambda b, c: (b, 0, 0, 0)
    c_spec = pl.BlockSpec((bt, 2, LANES_V7X, LANES_V7X), smap4)
    n_spec = pl.BlockSpec((bt, 2, LANES_V7X), smap3)
    m_spec = pl.BlockSpec((bt, SUBLANES_V7X, LANES_V7X), smap3)
    c_in = pl.BlockSpec((bt, 2, LANES_V7X, LANES_V7X), lambda b, c: (layer * nb + b, 0, 0, 0))
    n_in = pl.BlockSpec((bt, 2, LANES_V7X), lambda b, c: (layer * nb + b, 0, 0))
    n = q.shape[0]
    blk = 4 * (rows * (2 * QK_W + 2 * M_WIDTH + LANES_V7X + SUBLANES_V7X)
               + 2 * bt * (2 * LANES_V7X * LANES_V7X + 2 * LANES_V7X + SUBLANES_V7X * LANES_V7X))
    tmp = 4 * 16 * chunk * max(chunk, LANES_V7X)
    kern = functools.partial(_mlstm_kernel, chunk=chunk, real=real, bt=bt, mm_dtype=mm_dtype)
    return pl.pallas_call(
        kern,
        grid=grid,
        in_specs=[pl.BlockSpec((rows, QK_W), rmap), pl.BlockSpec((rows, QK_W), rmap),
                  pl.BlockSpec((rows, M_WIDTH), rmap), pl.BlockSpec((rows, LANES_V7X), rmap),
                  pl.BlockSpec((SUBLANES_V7X, rows), lambda b, c: (0, b * nchunks + c)),
                  _resident((1, LANES_V7X)), _resident((SUBLANES_V7X, LANES_V7X)),
                  c_in, n_in, m_spec],
        out_specs=[pl.BlockSpec((rows, M_WIDTH), rmap), c_spec, n_spec, m_spec],
        out_shape=[jax.ShapeDtypeStruct((n, M_WIDTH), F32),
                   jax.ShapeDtypeStruct((nseq, 2, LANES_V7X, LANES_V7X), F32),
                   jax.ShapeDtypeStruct((nseq, 2, LANES_V7X), F32),
                   jax.ShapeDtypeStruct((nseq, SUBLANES_V7X, LANES_V7X), F32)],
        compiler_params=_params(_vmem_limit(blk, 0, tmp), 2),
        name="mlstm",
    )(q, k, v, gc, gr, brow, bcol, c0, n0, m0)


def _mlstm_t_kernel(qt_ref, k_ref, vt_ref, gc_ref, gr_ref, brow_ref, bcol_ref, c0_ref, n0_ref, m0_ref,
                    h_ref, c_ref, n_ref, m_ref):
    L = M_CHUNK
    ci = pl.program_id(1)

    @pl.when(ci == 0)
    def _():
        c_ref[...] = c0_ref[...]
        n_ref[...] = n0_ref[...]
        m_ref[...] = m0_ref[...]

    def split3(a):
        hi = a.astype(BF16).astype(F32)
        r1 = a - hi
        mid = r1.astype(BF16).astype(F32)
        lo = (r1 - mid).astype(BF16).astype(F32)
        return hi, mid, lo

    si = lax.broadcasted_iota(jnp.int32, (L, L), 0)
    ti = lax.broadcasted_iota(jnp.int32, (L, L), 1)
    causal = si <= ti
    tri = (ti <= si).astype(F32)
    tri_t = causal.astype(F32)
    lane_g = lax.broadcasted_iota(jnp.int32, (L, LANES_V7X), 1)
    is_f_col = (lane_g >= M_HEADS) & (lane_g < 2 * M_HEADS)
    row_r = lax.broadcasted_iota(jnp.int32, (SUBLANES_V7X, L), 0)
    is_f_row = row_r >= M_HEADS
    half_masks = (lane_g < M_QK_DIM, lane_g >= M_QK_DIM)
    lanes_n = lax.broadcasted_iota(jnp.int32, (1, LANES_V7X), 1)
    neg_inf = jnp.float32(-jnp.inf)
    PADR = 2 * SUBLANES_V7X

    gcol = gc_ref[...] + brow_ref[...]
    gcol = jnp.where(is_f_col, _log_sigmoid(gcol), gcol)
    grow = gr_ref[...] + bcol_ref[...]
    grow = jnp.where(is_f_row, _log_sigmoid(grow), grow)
    cum_c = sum(jnp.dot(tri, p, preferred_element_type=F32)
                for p in split3(jnp.where(is_f_col, gcol, 0.0)))
    cum_r = sum(jnp.dot(p, tri_t, preferred_element_type=F32)
                for p in split3(jnp.where(is_f_row, grow, 0.0)))

    zq = jnp.zeros((M_QK_DIM, L), BF16)
    zpad = jnp.zeros((PADR - 1, LANES_V7X), F32)
    kps, s_raw, inter = [], [], []
    for h in range(M_HEADS):
        pair, half = divmod(h, 2)
        kp = k_ref[:, pair * LANES_V7X:(pair + 1) * LANES_V7X]
        qh = qt_ref[h * M_QK_DIM:(h + 1) * M_QK_DIM, :].astype(BF16)
        qz = jnp.concatenate([qh, zq] if half == 0 else [zq, qh], axis=0)
        caug = jnp.concatenate([c_ref[0, pair], n_ref[0, pair:pair + 1, :], zpad], axis=0)
        kps.append(kp)
        s_raw.append(jnp.dot(kp.astype(BF16), qz, preferred_element_type=F32))
        inter.append(jnp.dot(caug.astype(BF16), qz, preferred_element_type=F32))
    s_w, iscales, dens_, m_rows, w_rows, decays = [], [], [], [], [], []
    for h in range(M_HEADS):
        b_row = cum_r[M_HEADS + h:M_HEADS + h + 1, :]
        li_row = grow[h:h + 1, :]
        d_col = gcol[:, h:h + 1] - cum_c[:, M_HEADS + h:M_HEADS + h + 1]
        m0 = m_ref[0, h:h + 1, 0:1]
        dmat = jnp.where(causal, b_row + d_col, neg_inf)
        inter_m = b_row + m0
        m_row = jnp.maximum(inter_m, jnp.max(dmat, axis=0, keepdims=True))
        s = s_raw[h] * jnp.exp(dmat - m_row)
        iscale = jnp.exp(inter_m - m_row)
        dens_.append(iscale * inter[h][M_V_DIM:M_V_DIM + 1, :] + jnp.sum(s, axis=0, keepdims=True))
        m_new = m_row[:, L - 1:L]
        b_last = b_row[:, L - 1:L]
        w_rows.append(jnp.exp(b_last - b_row + li_row - m_new))
        decays.append(jnp.exp(b_last + m0 - m_new))
        s_w.append(s.astype(BF16))
        iscales.append(iscale)
        m_rows.append(m_row)
    outs, upds = [], []
    for h in range(M_HEADS):
        pair, half = divmod(h, 2)
        vt = vt_ref[h * M_V_DIM:(h + 1) * M_V_DIM, :]
        num = iscales[h] * inter[h][:M_V_DIM, :] + jnp.dot(vt.astype(BF16), s_w[h], preferred_element_type=F32)
        outs.append(num / jnp.maximum(jnp.abs(dens_[h]), jnp.exp(-m_rows[h])))
        vaug = jnp.concatenate([vt * w_rows[h], w_rows[h], jnp.zeros((PADR - 1, L), F32)], axis=0)
        kmask = jnp.where(half_masks[half], kps[h], 0.0)
        upds.append(jnp.dot(vaug.astype(BF16), kmask.astype(BF16), preferred_element_type=F32))
    for pair in range(M_HEADS // 2):
        dl = jnp.where(lanes_n < M_QK_DIM, decays[2 * pair], decays[2 * pair + 1])
        u0, u1 = upds[2 * pair], upds[2 * pair + 1]
        n_old = n_ref[0, pair:pair + 1, :]
        c_ref[0, pair] = dl * c_ref[0, pair] + u0[:M_V_DIM] + u1[:M_V_DIM]
        n_ref[0, pair:pair + 1, :] = dl * n_old + u0[M_V_DIM:M_V_DIM + 1] + u1[M_V_DIM:M_V_DIM + 1]
    for h in range(M_HEADS):
        m_ref[0, h:h + 1, :] = jnp.broadcast_to(m_rows[h][:, L - 1:L], (1, LANES_V7X))
    h_ref[...] = jnp.concatenate(outs, axis=0).T


def _mlstm_t(qt, k, vt, gc, gr, brow, bcol, c0, n0, m0, *, nseq, nchunks):
    L = M_CHUNK
    n = k.shape[0]
    rmap = lambda b, c: (b * nchunks + c, 0)
    cmap = lambda b, c: (0, b * nchunks + c)
    smap3 = lambda b, c: (b, 0, 0)
    smap4 = lambda b, c: (b, 0, 0, 0)
    c_spec = pl.BlockSpec((1, 2, LANES_V7X, LANES_V7X), smap4)
    n_spec = pl.BlockSpec((1, 2, LANES_V7X), smap3)
    m_spec = pl.BlockSpec((1, SUBLANES_V7X, LANES_V7X), smap3)
    blk = 4 * (L * (2 * QK_W + 2 * M_WIDTH + LANES_V7X + SUBLANES_V7X)
               + 2 * (2 * LANES_V7X * LANES_V7X + 2 * LANES_V7X + SUBLANES_V7X * LANES_V7X))
    tmp = 4 * 48 * L * L
    return pl.pallas_call(
        _mlstm_t_kernel,
        grid=(nseq, nchunks),
        in_specs=[pl.BlockSpec((QK_W, L), cmap), pl.BlockSpec((L, QK_W), rmap),
                  pl.BlockSpec((M_WIDTH, L), cmap), pl.BlockSpec((L, LANES_V7X), rmap),
                  pl.BlockSpec((SUBLANES_V7X, L), cmap),
                  _resident((1, LANES_V7X)), _resident((SUBLANES_V7X, LANES_V7X)),
                  c_spec, n_spec, m_spec],
        out_specs=[pl.BlockSpec((L, M_WIDTH), rmap), c_spec, n_spec, m_spec],
        out_shape=[jax.ShapeDtypeStruct((n, M_WIDTH), F32),
                   jax.ShapeDtypeStruct((nseq, 2, LANES_V7X, LANES_V7X), F32),
                   jax.ShapeDtypeStruct((nseq, 2, LANES_V7X), F32),
                   jax.ShapeDtypeStruct((nseq, SUBLANES_V7X, LANES_V7X), F32)],
        compiler_params=_params(_vmem_limit(blk, 0, tmp), 2),
        name="mlstm_t",
    )(qt, k, vt, gc, gr, brow, bcol, c0, n0, m0)


def _swa_prompt_kernel(qt_ref, kc_ref, kp_ref, vtc_ref, vtp_ref, sink_ref, o_ref, vlast_ref):
    blk = pl.program_id(1)
    L = WINDOW

    @pl.when(blk == pl.num_programs(1) - 1)
    def _():
        vlast_ref[0] = vtc_ref[...].T

    key = lax.broadcasted_iota(jnp.int32, (2 * L, L), 0)
    qry = lax.broadcasted_iota(jnp.int32, (2 * L, L), 1)
    prev_off = jnp.where(blk > 0, 0, L)
    mask = ((key < L) & (key >= qry + prev_off)) | ((key >= L) & (key - L <= qry))
    k2 = jnp.concatenate([kp_ref[...], kc_ref[...]], axis=0).astype(BF16)
    vt2 = jnp.concatenate([vtp_ref[...], vtc_ref[...]], axis=1).astype(BF16)
    zeros = jnp.zeros((A_HEAD_DIM, L), BF16)
    scores = []
    for hd in range(A_HEADS):
        qh = qt_ref[hd * A_HEAD_DIM:(hd + 1) * A_HEAD_DIM, :].astype(BF16)
        qz = jnp.concatenate([qh, zeros] if hd < A_GROUP else [zeros, qh], axis=0)
        scores.append(jnp.dot(k2, qz, preferred_element_type=F32))
    probs = []
    dens = []
    for hd in range(A_HEADS):
        s = jnp.where(mask, scores[hd], -jnp.inf)
        sink = sink_ref[hd:hd + 1, 0:1]
        m = jnp.maximum(jnp.max(s, axis=0, keepdims=True), sink)
        p = jnp.exp(s - m)
        dens.append(jnp.sum(p, axis=0, keepdims=True) + jnp.exp(sink - m))
        probs.append(p.astype(BF16))
    outs = []
    for hd in range(A_HEADS):
        kv = hd // A_GROUP
        o = jnp.dot(vt2[kv * A_HEAD_DIM:(kv + 1) * A_HEAD_DIM, :], probs[hd],
                    preferred_element_type=F32)
        outs.append(o / dens[hd])
    o_ref[...] = jnp.concatenate(outs, axis=0).T


def _swa_prompt(qt, k, vt, sinks, *, nseq, nblk):
    n = k.shape[0]
    L = WINDOW
    cur = lambda b, i: (b * nblk + i, 0)
    prev = lambda b, i: (b * nblk + jnp.maximum(i - 1, 0), 0)
    cur_t = lambda b, i: (0, b * nblk + i)
    prev_t = lambda b, i: (0, b * nblk + jnp.maximum(i - 1, 0))
    blk = 4 * L * (2 * A_WIDTH + 4 * KV_W)
    tmp = 4 * 24 * 2 * L * L
    return pl.pallas_call(
        _swa_prompt_kernel,
        grid=(nseq, nblk),
        in_specs=[pl.BlockSpec((A_WIDTH, L), cur_t),
                  pl.BlockSpec((L, KV_W), cur), pl.BlockSpec((L, KV_W), prev),
                  pl.BlockSpec((KV_W, L), cur_t), pl.BlockSpec((KV_W, L), prev_t),
                  _resident((A_HEADS, LANES_V7X))],
        out_specs=[pl.BlockSpec((L, A_WIDTH), cur), pl.BlockSpec((1, L, KV_W), lambda b, i: (b, 0, 0))],
        out_shape=[jax.ShapeDtypeStruct((n, A_WIDTH), F32), jax.ShapeDtypeStruct((nseq, L, KV_W), F32)],
        compiler_params=_params(_vmem_limit(blk, 0, tmp), 2),
        name="swa_prompt",
    )(qt, k, k, vt, vt, sinks)


def _swa_sample_kernel(q_ref, kn_ref, vn_ref, kc_ref, vc_ref, sink_ref, o_ref, ko_ref, vo_ref,
                       *, bt, real):
    P = SAMPLE_PAD_T
    W = WINDOW
    R = A_HEADS * P
    row = lax.broadcasted_iota(jnp.int32, (R, 2 * W), 0)
    key = lax.broadcasted_iota(jnp.int32, (R, 2 * W), 1)
    step = row % P
    mask = ((key < W) & (key >= step)) | ((key >= W) & (key - W <= step) & (key - W < real))
    sink = sink_ref[:, 0:1]
    pad = jnp.zeros((W - P, KV_W), F32)
    scores = []
    for bi in range(bt):
        kext = jnp.concatenate([kc_ref[bi], kn_ref[bi * P:(bi + 1) * P, :], pad], axis=0)
        scores.append(lax.dot_general(q_ref[bi], kext, _NT, preferred_element_type=F32))
    probs = []
    dens = []
    for bi in range(bt):
        s = jnp.where(mask, scores[bi], -jnp.inf)
        m = jnp.maximum(jnp.max(s, axis=-1, keepdims=True), sink)
        p = jnp.exp(s - m)
        dens.append(jnp.sum(p, axis=-1, keepdims=True) + jnp.exp(sink - m))
        probs.append(p)
    for bi in range(bt):
        rs = slice(bi * P, (bi + 1) * P)
        vn = vn_ref[rs, :]
        vext = jnp.concatenate([vc_ref[bi], vn, pad], axis=0)
        o_ref[bi] = jnp.dot(probs[bi], vext, preferred_element_type=F32) / dens[bi]
        ko_ref[bi, 0:W - real, :] = kc_ref[bi, real:W, :]
        ko_ref[bi, W - real:W, :] = kn_ref[bi * P:bi * P + real, :]
        vo_ref[bi, 0:W - real, :] = vc_ref[bi, real:W, :]
        vo_ref[bi, W - real:W, :] = vn[0:real, :]


def _swa_sample(q_rows, kn, vn, kcache, vcache, sinks, *, layer, nseq, bt, real):
    rows = bt * SAMPLE_PAD_T
    R = A_HEADS * SAMPLE_PAD_T
    nb = nseq // bt
    rmap = lambda i: (i, 0)
    cmap = lambda i: (layer * nb + i, 0, 0)
    omap = lambda i: (i, 0, 0)
    cache = pl.BlockSpec((bt, WINDOW, KV_W), cmap)
    blk = 4 * (2 * bt * R * KV_W + 2 * rows * KV_W + 4 * bt * WINDOW * KV_W)
    tmp = 4 * 16 * R * 2 * WINDOW
    kern = functools.partial(_swa_sample_kernel, bt=bt, real=real)
    return pl.pallas_call(
        kern,
        grid=(nb,),
        in_specs=[pl.BlockSpec((bt, R, KV_W), omap), pl.BlockSpec((rows, KV_W), rmap),
                  pl.BlockSpec((rows, KV_W), rmap), cache, cache, _resident((R, LANES_V7X))],
        out_specs=[pl.BlockSpec((bt, R, KV_W), omap),
                   pl.BlockSpec((bt, WINDOW, KV_W), omap), pl.BlockSpec((bt, WINDOW, KV_W), omap)],
        out_shape=[jax.ShapeDtypeStruct((nseq, R, KV_W), F32),
                   jax.ShapeDtypeStruct((nseq, WINDOW, KV_W), F32),
                   jax.ShapeDtypeStruct((nseq, WINDOW, KV_W), F32)],
        compiler_params=_params(_vmem_limit(blk, 0, tmp), 1),
        name="swa_sample",
    )(q_rows, kn, vn, kcache, vcache, sinks)


def _merge_kernel(hm_ref, mo_ref, ha_ref, gt_ref, x_ref, gmh_ref, wa_ref, wb_ref, wo_ref, o_ref):
    hm = hm_ref[...]
    parts = []
    for h in range(M_HEADS):
        v = hm[:, h * M_V_DIM:(h + 1) * M_V_DIM]
        parts.append(v * lax.rsqrt(jnp.mean(v * v, axis=-1, keepdims=True) + EPS))
    hn = jnp.concatenate(parts, axis=1) * gmh_ref[...] * jax.nn.sigmoid(mo_ref[...])
    a = jnp.dot(hn.astype(BF16), wa_ref[...], preferred_element_type=F32)
    b = jnp.dot(ha_ref[...].astype(BF16), wb_ref[...], preferred_element_type=F32)
    gates = jax.nn.sigmoid(gt_ref[...])
    mixed = gates[:, :D_MODEL] * a + gates[:, D_MODEL:] * b
    o_ref[...] = x_ref[...] + jnp.dot(mixed.astype(BF16), wo_ref[...], preferred_element_type=F32)


def _merge(hm, mo, ha, gt, x2d, gmh, wa, wb, wo):
    n = x2d.shape[0]
    t = min(ROW_TILE, n)
    row = lambda w: pl.BlockSpec((t, w), lambda i: (i, 0))
    blk = 4 * t * (3 * M_WIDTH + GATE_W + 2 * D_MODEL)
    res = 2 * (2 * M_WIDTH * D_MODEL + D_MODEL * D_MODEL)
    tmp = 4 * t * (GATE_W + 3 * D_MODEL)
    return pl.pallas_call(
        _merge_kernel,
        grid=(n // t,),
        in_specs=[row(M_WIDTH), row(M_WIDTH), row(A_WIDTH), row(GATE_W), row(D_MODEL),
                  _resident((1, M_WIDTH)), _resident((M_WIDTH, D_MODEL)),
                  _resident((A_WIDTH, D_MODEL)), _resident((D_MODEL, D_MODEL))],
        out_specs=row(D_MODEL),
        out_shape=jax.ShapeDtypeStruct((n, D_MODEL), F32),
        compiler_params=_params(_vmem_limit(blk, res, tmp), 1),
        name="merge",
    )(hm, mo, ha, gt, x2d, gmh, wa, wb, wo)


def _ffn_kernel(x_ref, gn_ref, halo_ref, wup_ref, wc_ref, wdn_ref, gfin_ref, o_ref, halo_out_ref,
                xn_s, u_a, u_b, act_a, act_b, acc_s, halo_s, *, g, final_norm):
    t = x_ref.shape[0]
    H = u_a.shape[0] - t
    ti = pl.program_id(1)
    last = pl.num_programs(1) - 1

    @pl.when(ti == 0)
    def _():
        for j in range(N_FF_CHUNKS):
            halo_s[j, :, :FF_CHUNK] = halo_ref[0, :, j * FF_CHUNK:(j + 1) * FF_CHUNK]
            halo_s[j, :, FF_CHUNK:] = halo_ref[0, :, D_FF + j * FF_CHUNK:D_FF + (j + 1) * FF_CHUNK]

    x = x_ref[...]
    xn_s[...] = (x * lax.rsqrt(jnp.mean(x * x, axis=-1, keepdims=True) + EPS) * gn_ref[...]).astype(BF16)

    def up(j, u_s):
        u_s[H:H + t, :] = jnp.dot(xn_s[...], wup_ref[j], preferred_element_type=F32)

    def conv_act(j, u_s, act_s):
        u_s[H - 2 * g:H, :] = halo_s[j]
        u = u_s[H:H + t, :]
        u1 = u_s[H - g:H - g + t, :]
        u2 = u_s[H - 2 * g:H - 2 * g + t, :]
        halo_s[j] = u_s[H + t - 2 * g:H + t, :]
        wc = wc_ref[j]
        c = wc[3:4, :] + wc[0:1, :] * u2 + wc[1:2, :] * u1 + wc[2:3, :] * u
        a = c[:, :FF_CHUNK]
        act = 0.5 * a * (1.0 + lax.erf(a * np.float32(np.sqrt(0.5)))) * c[:, FF_CHUNK:]
        act_s[...] = act.astype(BF16)

    def down(j, act_s, first=False):
        d = jnp.dot(act_s[...], wdn_ref[j], preferred_element_type=F32)
        if first:
            acc_s[...] = d
        else:
            acc_s[...] += d

    n = N_FF_CHUNKS
    assert n % 2 == 1 and n >= 5
    up(0, u_a)
    up(1, u_b)
    conv_act(0, u_a, act_a)
    up(2, u_a)
    conv_act(1, u_b, act_b)
    down(0, act_a, first=True)

    def tick_pair(p, carry):
        i = 3 + 2 * p
        conv_act(i - 1, u_a, act_a)
        up(i, u_b)
        down(i - 2, act_b)
        conv_act(i, u_b, act_b)
        up(i + 1, u_a)
        down(i - 1, act_a)
        return carry

    lax.fori_loop(0, (n - 3) // 2, tick_pair, 0)
    conv_act(n - 1, u_a, act_a)
    down(n - 2, act_b)
    down(n - 1, act_a)
    y = x + acc_s[...]
    if final_norm:
        y = y * lax.rsqrt(jnp.mean(y * y, axis=-1, keepdims=True) + EPS) * gfin_ref[...]
    o_ref[...] = y

    @pl.when(ti == last)
    def _():
        for j in range(N_FF_CHUNKS):
            halo_out_ref[0, :, j * FF_CHUNK:(j + 1) * FF_CHUNK] = halo_s[j, :, :FF_CHUNK]
            halo_out_ref[0, :, D_FF + j * FF_CHUNK:D_FF + (j + 1) * FF_CHUNK] = halo_s[j, :, FF_CHUNK:]


def _ffn(x2d, gn, halo, wup, wc, wdn, gfin, *, nseq, t, g, final_norm):
    n = x2d.shape[0]
    tiles = n // (nseq * t)
    H = -(-2 * g // SUBLANES_V7X) * SUBLANES_V7X
    rmap = lambda s, i: (s * tiles + i, 0)
    hmap = lambda s, i: (s, 0, 0)
    blk = 4 * (2 * t * D_MODEL + 2 * 2 * g * 2 * D_FF)
    res = 2 * (D_MODEL * 2 * D_FF + D_FF * D_MODEL) + 4 * N_FF_CHUNKS * SUBLANES_V7X * 2 * FF_CHUNK
    scr = 2 * t * D_MODEL + 2 * 4 * (H + t) * 2 * FF_CHUNK + 2 * 2 * t * FF_CHUNK + 4 * t * D_MODEL \
        + 4 * N_FF_CHUNKS * 2 * g * 2 * FF_CHUNK
    tmp = 4 * t * (6 * FF_CHUNK + D_MODEL)
    kern = functools.partial(_ffn_kernel, g=g, final_norm=final_norm)
    return pl.pallas_call(
        kern,
        grid=(nseq, tiles),
        in_specs=[pl.BlockSpec((t, D_MODEL), rmap), _resident((1, D_MODEL)),
                  pl.BlockSpec((1, 2 * g, 2 * D_FF), hmap),
                  _resident((N_FF_CHUNKS, D_MODEL, 2 * FF_CHUNK)),
                  _resident((N_FF_CHUNKS, SUBLANES_V7X, 2 * FF_CHUNK)),
                  _resident((N_FF_CHUNKS, FF_CHUNK, D_MODEL)), _resident((1, D_MODEL))],
        out_specs=[pl.BlockSpec((t, D_MODEL), rmap), pl.BlockSpec((1, 2 * g, 2 * D_FF), hmap)],
        out_shape=[jax.ShapeDtypeStruct((n, D_MODEL), F32),
                   jax.ShapeDtypeStruct((nseq, 2 * g, 2 * D_FF), F32)],
        scratch_shapes=[pltpu.VMEM((t, D_MODEL), BF16),
                        pltpu.VMEM((H + t, 2 * FF_CHUNK), F32), pltpu.VMEM((H + t, 2 * FF_CHUNK), F32),
                        pltpu.VMEM((t, FF_CHUNK), BF16), pltpu.VMEM((t, FF_CHUNK), BF16),
                        pltpu.VMEM((t, D_MODEL), F32),
                        pltpu.VMEM((N_FF_CHUNKS, 2 * g, 2 * FF_CHUNK), F32)],
        compiler_params=_params(_vmem_limit(blk, res + scr, tmp), 2),
        name="convffn",
    )(x2d, gn, halo, wup, wc, wdn, gfin)


def _prep_layer(l, g_mix, w_in, b_igate, b_fgate, g_mhead, a_sinks, w_a, w_b, w_out, g_ffn, w_up,
                w_conv, b_conv, w_down):
    w = w_in[l]
    seg = lambda i: w[:, _OFF[i]:_OFF[i + 1]]
    w_main = jnp.concatenate([seg(1), seg(3), seg(7), seg(9)], axis=1).astype(BF16)
    w_flex = tuple(seg(i).astype(BF16) for i in (0, 2, 6, 8))
    w_gate8 = jnp.concatenate([seg(4), seg(5)], axis=1)
    w_gate = jnp.pad(w_gate8, ((0, 0), (0, LANES_V7X - 2 * M_HEADS))).astype(BF16)
    w_gate_t = jnp.pad(w_gate8.T, ((0, SUBLANES_V7X), (0, 0))).astype(BF16)
    bias8 = jnp.concatenate([b_igate[l], b_fgate[l]]).astype(F32)
    brow = jnp.pad(bias8, (0, LANES_V7X - 2 * M_HEADS)).reshape(1, LANES_V7X)
    bcol = jnp.broadcast_to(bias8[:, None], (SUBLANES_V7X, LANES_V7X))
    sink8 = a_sinks[l].reshape(A_HEADS, 1).astype(F32)
    sinks = jnp.broadcast_to(sink8, (A_HEADS, LANES_V7X))
    sink_rows = jnp.broadcast_to(jnp.repeat(sink8, SAMPLE_PAD_T, axis=0), (A_HEADS * SAMPLE_PAD_T, LANES_V7X))
    up = w_up[l]
    wup = jnp.stack([jnp.concatenate([up[:, j * FF_CHUNK:(j + 1) * FF_CHUNK],
                                      up[:, D_FF + j * FF_CHUNK:D_FF + (j + 1) * FF_CHUNK]], axis=1)
                     for j in range(N_FF_CHUNKS)]).astype(BF16)
    cw = jnp.concatenate([w_conv[l], b_conv[l][None, :]], axis=0)
    cw = jnp.pad(cw, ((0, SUBLANES_V7X - CONV_W - 1), (0, 0))).astype(F32)
    wc = jnp.stack([jnp.concatenate([cw[:, j * FF_CHUNK:(j + 1) * FF_CHUNK],
                                     cw[:, D_FF + j * FF_CHUNK:D_FF + (j + 1) * FF_CHUNK]], axis=1)
                    for j in range(N_FF_CHUNKS)])
    wdn = w_down[l].reshape(N_FF_CHUNKS, FF_CHUNK, D_MODEL).astype(BF16)
    return dict(g_mix=g_mix[l].reshape(1, D_MODEL), w_main=w_main, w_flex=w_flex,
                w_flex_t=tuple(w.T for w in w_flex), w_gate=w_gate, w_gate_t=w_gate_t,
                brow=brow, bcol=bcol, gmh=g_mhead[l].reshape(1, M_WIDTH), sinks=sinks, sink_rows=sink_rows,
                wa=w_a[l].astype(BF16), wb=w_b[l].astype(BF16), wo=w_out[l].astype(BF16),
                g_ffn=g_ffn[l].reshape(1, D_MODEL), wup=wup, wc=wc, wdn=wdn)


def _rope_tables(pos):
    half = A_HEAD_DIM // 2
    inv = ROPE_THETA ** (-jnp.arange(0, A_HEAD_DIM, 2, dtype=F32) / A_HEAD_DIM)
    ang = pos.astype(F32)[:, None] * inv[None, :]
    ang = jnp.tile(jnp.concatenate([ang, ang], axis=-1), (1, LANES_V7X // A_HEAD_DIM))
    sign = jnp.where((jnp.arange(LANES_V7X) % A_HEAD_DIM) < half, -1.0, 1.0).astype(F32)
    return jnp.cos(ang), jnp.sin(ang) * sign[None, :]


def _rope_tables_t(pos):
    inv = ROPE_THETA ** (-jnp.arange(0, A_HEAD_DIM, 2, dtype=F32) / A_HEAD_DIM)
    ang = inv[:, None] * pos.astype(F32)[None, :]
    return jnp.cos(ang), jnp.sin(ang)


def _prompt_trunk(x, layers, g_final):
    B, S, _ = x.shape
    n = B * S
    x2d = x.reshape(n, D_MODEL)
    cos, sin = _rope_tables(jnp.arange(S, dtype=jnp.int32))
    cos_t, sin_t = _rope_tables_t(jnp.arange(S, dtype=jnp.int32))
    nchunks = S // M_CHUNK
    c0 = jnp.zeros((B, 2, LANES_V7X, LANES_V7X), F32)
    n0 = jnp.zeros((B, 2, LANES_V7X), F32)
    m0 = jnp.zeros((B, SUBLANES_V7X, LANES_V7X), F32)
    halo0 = jnp.zeros((B, CONV_W - 1, 2 * D_FF), F32)
    outs = dict(C=[], n=[], m=[], k=[], v=[], conv=[])
    for l, p in enumerate(layers):
        mk, mo, ak, gt, mq_t, mv_t, aq_t, av_t, gc, gr = _inproj(
            x2d, p["g_mix"], p["w_main"], p["w_flex_t"], p["w_gate"], p["w_gate_t"],
            cos, sin, cos_t, sin_t, S // ROW_TILE, True)
        hm, c_new, n_new, m_new = _mlstm_t(mq_t, mk, mv_t, gc, gr, p["brow"], p["bcol"], c0, n0, m0,
                                           nseq=B, nchunks=nchunks)
        ha, v_last = _swa_prompt(aq_t, ak, av_t, p["sinks"], nseq=B, nblk=S // WINDOW)
        x2d = _merge(hm, mo, ha, gt, x2d, p["gmh"], p["wa"], p["wb"], p["wo"])
        x2d, conv = _ffn(x2d, p["g_ffn"], halo0, p["wup"], p["wc"], p["wdn"], g_final,
                         nseq=B, t=ROW_TILE, g=1, final_norm=(l == DEPTH - 1))
        c_new = jnp.transpose(c_new.reshape(B, M_HEADS // 2, M_V_DIM, 2, M_QK_DIM), (0, 1, 3, 4, 2))
        outs["C"].append(c_new.reshape(B, M_HEADS, M_QK_DIM, M_V_DIM))
        outs["n"].append(n_new.reshape(B, M_HEADS, M_QK_DIM))
        outs["m"].append(m_new[:, :M_HEADS, 0])
        outs["k"].append(ak.reshape(B, S, A_KV_HEADS, A_HEAD_DIM)[:, S - WINDOW:])
        outs["v"].append(v_last.reshape(B, WINDOW, A_KV_HEADS, A_HEAD_DIM))
        outs["conv"].append(conv)
    return (x2d.reshape(B, S, D_MODEL),) + tuple(jnp.stack(outs[k]) for k in ("C", "n", "m", "k", "v", "conv"))


def _sample_trunk(x, st_c, st_n, st_m, st_k, st_v, st_conv, layers, g_final):
    B, T, _ = x.shape
    P = SAMPLE_PAD_T
    bt = LANES_V7X // P
    gb = SAMPLE_FFN_GROUP
    ng = B // gb
    pos =PAST_LEN + (jnp.arange(B * P, dtype=jnp.int32) % P)
    cos, sin = _rope_tables(pos)
    c_all = st_c.reshape(DEPTH * B, 2, LANES_V7X, LANES_V7X)
    n_all = st_n.reshape(DEPTH * B, 2, LANES_V7X)
    k_all = st_k.reshape(DEPTH * B, WINDOW, KV_W)
    v_all = st_v.reshape(DEPTH * B, WINDOW, KV_W)
    xpad = jnp.pad(x, ((0, 0), (0, P - T), (0, 0))).reshape(B * P, D_MODEL)
    outs = dict(C=[], n=[], m=[], k=[], v=[], conv=[])
    y_bm = None
    for l, p in enumerate(layers):
        mk, mo, ak, gt, mq, mv, aq, av, gc, gr = _inproj(
            xpad, p["g_mix"], p["w_main"], p["w_flex"], p["w_gate"], p["w_gate_t"],
            cos, sin, cos, sin, (B * P) // ROW_TILE, False)
        m0 = jnp.broadcast_to(jnp.pad(st_m[l], ((0, 0), (0, SUBLANES_V7X - M_HEADS)))[:, :, None],
                              (B, SUBLANES_V7X, LANES_V7X))
        hm, c_new, n_new, m_new = _mlstm(mq, mk, mv, gc, gr, p["brow"], p["bcol"], c_all, n_all, m0,
                                         layer=l, nseq=B, nchunks=1, chunk=P, real=T, bt=bt, mm_dtype=F32)
        q5 = jnp.transpose(aq.reshape(B, P, A_KV_HEADS, A_GROUP, A_HEAD_DIM), (0, 2, 3, 1, 4))
        q5 = q5.reshape(B, A_KV_HEADS, A_GROUP * P, A_HEAD_DIM)
        zq = jnp.zeros_like(q5[:, 0])
        q_rows = jnp.concatenate([jnp.concatenate([q5[:, 0], zq], axis=-1),
                                  jnp.concatenate([zq, q5[:, 1]], axis=-1)], axis=1)
        o_rows, k_new, v_new = _swa_sample(q_rows, ak, av, k_all, v_all, p["sink_rows"],
                                           layer=l, nseq=B, bt=bt, real=T)
        o5 = jnp.stack([o_rows[:, :A_GROUP * P, :A_HEAD_DIM], o_rows[:, A_GROUP * P:, A_HEAD_DIM:]], axis=1)
        ha = jnp.transpose(o5.reshape(B, A_KV_HEADS, A_GROUP, P, A_HEAD_DIM),
                           (0, 3, 1, 2, 4)).reshape(B * P, A_WIDTH)
        xm = _merge(hm, mo, ha, gt, xpad, p["gmh"], p["wa"], p["wb"], p["wo"])
        x_tm = jnp.transpose(xm.reshape(ng, gb, P, D_MODEL)[:, :, :T], (0, 2, 1, 3)).reshape(T * B, D_MODEL)
        halo = jnp.transpose(st_conv[l].reshape(ng, gb, CONV_W - 1, 2 * D_FF),
                             (0, 2, 1, 3)).reshape(ng, (CONV_W - 1) * gb, 2 * D_FF)
        y_tm, conv = _ffn(x_tm, p["g_ffn"], halo, p["wup"], p["wc"], p["wdn"], g_final,
                          nseq=ng, t=T * gb, g=gb, final_norm=(l == DEPTH - 1))
        y_bm = jnp.transpose(y_tm.reshape(ng, T, gb, D_MODEL), (0, 2, 1, 3)).reshape(B, T, D_MODEL)
        xpad = jnp.pad(y_bm, ((0, 0), (0, P - T), (0, 0))).reshape(B * P, D_MODEL)
        outs["C"].append(c_new.reshape(B, M_HEADS, M_QK_DIM, M_V_DIM))
        outs["n"].append(n_new.reshape(B, M_HEADS, M_QK_DIM))
        outs["m"].append(m_new[:, :M_HEADS, 0])
        outs["k"].append(k_new.reshape(B, WINDOW, A_KV_HEADS, A_HEAD_DIM))
        outs["v"].append(v_new.reshape(B, WINDOW, A_KV_HEADS, A_HEAD_DIM))
        outs["conv"].append(jnp.transpose(conv.reshape(ng, CONV_W - 1, gb, 2 * D_FF),
                                          (0, 2, 1, 3)).reshape(B, CONV_W - 1, 2 * D_FF))
    return (y_bm,) + tuple(jnp.stack(outs[k]) for k in ("C", "n", "m", "k", "v", "conv"))


def kernel(x_prompt, x_sample, state_mlstm_C, state_mlstm_n, state_mlstm_m, cache_swa_k, cache_swa_v,
           state_ffn_conv, g_mix, w_in, b_igate, b_fgate, g_mhead, a_sinks, w_a, w_b, w_out, g_ffn,
           w_up, w_conv, b_conv, w_down, g_final):
    layers = [_prep_layer(l, g_mix, w_in, b_igate, b_fgate, g_mhead, a_sinks, w_a, w_b, w_out, g_ffn,
                          w_up, w_conv, b_conv, w_down) for l in range(DEPTH)]
    gfin = g_final.reshape(1, D_MODEL)
    yp, p_c, p_n, p_m, p_k, p_v, p_conv = _prompt_trunk(x_prompt, layers, gfin)
    ys, s_c, s_n, s_m, s_k, s_v, s_conv = _sample_trunk(
        x_sample, state_mlstm_C, state_mlstm_n, state_mlstm_m, cache_swa_k, cache_swa_v,
        state_ffn_conv, layers, gfin)
    return (yp, ys, p_c, p_n, p_m, p_k, p_v, p_conv, s_c, s_n, s_m, s_k, s_v, s_conv)
```

```python
import functools

import numpy as np
import jax
import jax.numpy as jnp
from jax import lax
from jax.experimental import pallas as pl
from jax.experimental.pallas import tpu as pltpu

F32 = jnp.float32
BF16 = jnp.bfloat16

D_MODEL = 1024
DEPTH = 2
PAST_LEN = 16384
M_HEADS = 4
M_QK_DIM = 64
M_V_DIM = 128
M_WIDTH = M_HEADS * M_V_DIM
M_CHUNK = 128
A_HEADS = 8
A_KV_HEADS = 2
A_GROUP = A_HEADS // A_KV_HEADS
A_HEAD_DIM = 64
A_WIDTH = A_HEADS * A_HEAD_DIM
WINDOW = 128
ROPE_THETA = 10000.0
D_FF = 2816
CONV_W = 3
EPS = 1e-6

QK_W = M_HEADS * M_QK_DIM
KV_W = A_KV_HEADS * A_HEAD_DIM
GATE_W = 2 * D_MODEL
_SPLIT = (QK_W, QK_W, M_WIDTH, M_WIDTH, M_HEADS, M_HEADS, A_WIDTH, KV_W, KV_W, GATE_W)
_OFF = np.concatenate([[0], np.cumsum(_SPLIT)]).tolist()
_MAIN_W = (QK_W, M_WIDTH, KV_W, GATE_W)
_MAIN_OFF = np.concatenate([[0], np.cumsum(_MAIN_W)]).tolist()
MAIN_COLS = _MAIN_OFF[-1]

LANES_V7X = 128
SUBLANES_V7X = 8
VMEM_BYTES_V7X = 64 * 1024 * 1024
VMEM_LIMIT_CAP_V7X = 60000 * 1024

SAMPLE_PAD_T = SUBLANES_V7X
FF_CHUNK = 2 * LANES_V7X
N_FF_CHUNKS = D_FF // FF_CHUNK
ROW_TILE = 512
SAMPLE_FFN_GROUP = 32
FFN_ROWS = 128


def _vmem_limit(block_bytes, scratch_bytes, temp_bytes):
    need = 2 * block_bytes + scratch_bytes + temp_bytes
    assert need <= VMEM_LIMIT_CAP_V7X, need
    return VMEM_LIMIT_CAP_V7X


def _params(vmem_bytes, n_axes):
    return pltpu.CompilerParams(dimension_semantics=("arbitrary",) * n_axes,
                                vmem_limit_bytes=vmem_bytes)


def _resident(shape):
    nd = len(shape)
    return pl.BlockSpec(shape, lambda *_: (0,) * nd, pipeline_mode=pl.Buffered(1))


def _log_sigmoid(x):
    return jnp.minimum(x, 0.0) - jnp.log1p(jnp.exp(-jnp.abs(x)))


_NT = (((1,), (1,)), ((), ()))


def _inproj_kernel(x_ref, g_ref, w_ref, wmq_ref, wmv_ref, wq_ref, wv_ref, wg_ref, wgt_ref, cos_ref,
                   sin_ref, cosq_ref, sinq_ref, mk_ref, mo_ref, ak_ref, gt_ref, mq_ref, mv_ref, aq_ref,
                   av_ref, gc_ref, gr_ref, *, tokens_on_lanes):
    x = x_ref[...]
    r = lax.rsqrt(jnp.mean(x * x, axis=-1, keepdims=True) + EPS)
    xn = (x * r * g_ref[...]).astype(BF16)

    def proj(idx):
        return jnp.dot(xn, w_ref[:, _MAIN_OFF[idx]:_MAIN_OFF[idx + 1]], preferred_element_type=F32)

    mk_ref[...] = proj(0)
    mo_ref[...] = proj(1)
    gt_ref[...] = proj(3)
    if tokens_on_lanes:
        mq_ref[...] = lax.dot_general(wmq_ref[...], xn, _NT, preferred_element_type=F32) * (M_QK_DIM ** -0.5)
        mv_ref[...] = lax.dot_general(wmv_ref[...], xn, _NT, preferred_element_type=F32)
    else:
        mq_ref[...] = jnp.dot(xn, wmq_ref[...], preferred_element_type=F32) * (M_QK_DIM ** -0.5)
        mv_ref[...] = jnp.dot(xn, wmv_ref[...], preferred_element_type=F32)

    cos = cos_ref[...]
    sin = sin_ref[...]
    rows = x.shape[0]
    lane = lax.broadcasted_iota(jnp.int32, (rows, LANES_V7X), 1)
    first_half = (lane % A_HEAD_DIM) < (A_HEAD_DIM // 2)

    def rope(v, c, s):
        rot = jnp.where(first_half,
                        pltpu.roll(v, LANES_V7X - A_HEAD_DIM // 2, 1),
                        pltpu.roll(v, A_HEAD_DIM // 2, 1))
        return v * c + rot * s

    ak_ref[...] = rope(proj(2), cos, sin)

    scale = A_HEAD_DIM ** -0.5
    if tokens_on_lanes:
        half = A_HEAD_DIM // 2
        cq = cosq_ref[...]
        sq = sinq_ref[...]
        qt = lax.dot_general(wq_ref[...], xn, _NT, preferred_element_type=F32)
        for h in range(A_HEADS):
            x1 = qt[h * A_HEAD_DIM:h * A_HEAD_DIM + half]
            x2 = qt[h * A_HEAD_DIM + half:(h + 1) * A_HEAD_DIM]
            aq_ref[h * A_HEAD_DIM:h * A_HEAD_DIM + half, :] = (x1 * cq - x2 * sq) * scale
            aq_ref[h * A_HEAD_DIM + half:(h + 1) * A_HEAD_DIM, :] = (x2 * cq + x1 * sq) * scale
        av_ref[...] = lax.dot_general(wv_ref[...], xn, _NT, preferred_element_type=F32)
    else:
        cq = cosq_ref[...]
        sq = sinq_ref[...]
        aq = jnp.dot(xn, wq_ref[...], preferred_element_type=F32)
        for c in range(A_WIDTH // LANES_V7X):
            sl = slice(c * LANES_V7X, (c + 1) * LANES_V7X)
            aq_ref[:, sl] = rope(aq[:, sl], cq, sq) * scale
        av_ref[...] = jnp.dot(xn, wv_ref[...], preferred_element_type=F32)

    gc_ref[...] = jnp.dot(xn, wg_ref[...], preferred_element_type=F32)
    grt = lax.dot_general(wgt_ref[...], xn, _NT, preferred_element_type=F32)
    gr_ref[...] = grt[:SUBLANES_V7X]


_FLEX_W = (QK_W, M_WIDTH, A_WIDTH, KV_W)


def _inproj(x2d, g, w_main, w_flex, w_gate, w_gate_t, cos, sin, cos_q, sin_q, tiles_per_seq,
            tokens_on_lanes):
    n = x2d.shape[0]
    t = min(ROW_TILE, n)
    grid = (n // t,)
    row = lambda w: pl.BlockSpec((t, w), lambda i: (i, 0))
    col = lambda h: pl.BlockSpec((h, t), lambda i: (0, i))
    tab = pl.BlockSpec((t, LANES_V7X), lambda i: (i % tiles_per_seq, 0))
    if tokens_on_lanes:
        qtab = pl.BlockSpec((A_HEAD_DIM // 2, t), lambda i: (0, i % tiles_per_seq))
        flex_specs = [col(w) for w in _FLEX_W]
        flex_shapes = [(w, n) for w in _FLEX_W]
    else:
        qtab = tab
        flex_specs = [row(w) for w in _FLEX_W]
        flex_shapes = [(n, w) for w in _FLEX_W]
    out_shapes = [jax.ShapeDtypeStruct((n, w), F32) for w in _MAIN_W]
    out_shapes += [jax.ShapeDtypeStruct(s, F32) for s in flex_shapes]
    out_shapes += [jax.ShapeDtypeStruct((n, LANES_V7X), F32), jax.ShapeDtypeStruct((SUBLANES_V7X, n), F32)]
    out_specs = [row(w) for w in _MAIN_W] + flex_specs + [row(LANES_V7X), col(SUBLANES_V7X)]
    flex_cols = sum(_FLEX_W)
    blk = 4 * t * (D_MODEL + MAIN_COLS + flex_cols + 5 * LANES_V7X + SUBLANES_V7X)
    res = 2 * D_MODEL * (MAIN_COLS + flex_cols + LANES_V7X + 2 * SUBLANES_V7X)
    tmp = 4 * t * (GATE_W + D_MODEL)
    kern = functools.partial(_inproj_kernel, tokens_on_lanes=tokens_on_lanes)
    return pl.pallas_call(
        kern,
        grid=grid,
        in_specs=[row(D_MODEL), _resident((1, D_MODEL)), _resident((D_MODEL, MAIN_COLS))]
        + [_resident(w.shape) for w in w_flex]
        + [_resident((D_MODEL, LANES_V7X)), _resident((2 * SUBLANES_V7X, D_MODEL)), tab, tab, qtab, qtab],
        out_specs=out_specs,
        out_shape=out_shapes,
        compiler_params=_params(_vmem_limit(blk, res, tmp), 1),
        name="inproj",
    )(x2d, g, w_main, *w_flex, w_gate, w_gate_t, cos, sin, cos_q, sin_q)


def _mlstm_kernel(q_ref, k_ref, v_ref, gc_ref, gr_ref, brow_ref, bcol_ref, c0_ref, n0_ref, m0_ref,
                  h_ref, c_ref, n_ref, m_ref, *, chunk, real, bt, mm_dtype):
    L = chunk
    ci = pl.program_id(1)

    @pl.when(ci == 0)
    def _():
        c_ref[...] = c0_ref[...]
        n_ref[...] = n0_ref[...]
        m_ref[...] = m0_ref[...]

    def mx(a):
        return a.astype(mm_dtype)

    def split3(a):
        hi = a.astype(BF16).astype(F32)
        r1 = a - hi
        mid = r1.astype(BF16).astype(F32)
        lo = (r1 - mid).astype(BF16).astype(F32)
        return hi, mid, lo

    ti = lax.broadcasted_iota(jnp.int32, (L, L), 0)
    si = lax.broadcasted_iota(jnp.int32, (L, L), 1)
    causal = si <= ti
    tri = causal.astype(F32)
    tri_t = (ti <= si).astype(F32)
    lane_g = lax.broadcasted_iota(jnp.int32, (L, LANES_V7X), 1)
    row_g = lax.broadcasted_iota(jnp.int32, (L, LANES_V7X), 0)
    is_f_col = (lane_g >= M_HEADS) & (lane_g < 2 * M_HEADS)
    row_r = lax.broadcasted_iota(jnp.int32, (SUBLANES_V7X, L), 0)
    col_r = lax.broadcasted_iota(jnp.int32, (SUBLANES_V7X, L), 1)
    is_f_row = row_r >= M_HEADS
    lane_p = lax.broadcasted_iota(jnp.int32, (L, LANES_V7X), 1)
    half_masks = (lane_p < M_QK_DIM, lane_p >= M_QK_DIM)
    rows_c = lax.broadcasted_iota(jnp.int32, (2 * M_QK_DIM, 1), 0)
    lanes_n = lax.broadcasted_iota(jnp.int32, (1, LANES_V7X), 1)
    neg_inf = jnp.float32(-jnp.inf)

    units = [(bi, h) for bi in range(bt) for h in range(M_HEADS)]
    gates = []
    for bi in range(bt):
        rs = slice(bi * L, (bi + 1) * L)
        gcol = gc_ref[rs, :] + brow_ref[...]
        gcol = jnp.where(is_f_col, _log_sigmoid(gcol), gcol)
        fcol = jnp.where(is_f_col, gcol, 0.0)
        grow = gr_ref[:, rs] + bcol_ref[:, :L]
        grow = jnp.where(is_f_row, _log_sigmoid(grow), grow)
        frow = jnp.where(is_f_row, grow, 0.0)
        if real < L:
            gcol = jnp.where(row_g < real, gcol, neg_inf)
            fcol = jnp.where(row_g < real, fcol, 0.0)
            grow = jnp.where(col_r < real, grow, neg_inf)
            frow = jnp.where(col_r < real, frow, 0.0)
        cum_c = sum(jnp.dot(tri, p, preferred_element_type=F32) for p in split3(fcol))
        cum_r = sum(jnp.dot(p, tri_t, preferred_element_type=F32) for p in split3(frow))
        gates.append((gcol, grow, cum_c, cum_r))

    qhs, kps, vhs, s_raw, q_c = {}, {}, {}, {}, {}
    for bi, h in units:
        rs = slice(bi * L, (bi + 1) * L)
        pair, half = divmod(h, 2)
        ps = slice(pair * LANES_V7X, (pair + 1) * LANES_V7X)
        kp = k_ref[rs, ps]
        qh = jnp.where(half_masks[half], q_ref[rs, ps], 0.0)
        qhs[bi, h], kps[bi, h] = qh, kp
        vhs[bi, h] = v_ref[rs, h * M_V_DIM:(h + 1) * M_V_DIM]
        s_raw[bi, h] = lax.dot_general(mx(qh), mx(kp), _NT, preferred_element_type=F32)
        q_c[bi, h] = jnp.dot(mx(qh), mx(c_ref[bi, pair]), preferred_element_type=F32)

    s_w, kws, iscales, dens_, mrows, decays, new_m = {}, {}, {}, {}, {}, {}, {}
    for bi, h in units:
        gcol, grow, cum_c, cum_r = gates[bi]
        pair, half = divmod(h, 2)
        b_col = cum_c[:, M_HEADS + h:M_HEADS + h + 1]
        li_col = gcol[:, h:h + 1]
        b_row = cum_r[M_HEADS + h:M_HEADS + h + 1, :]
        li_row = grow[h:h + 1, :]
        m0 = m_ref[bi, h:h + 1, 0:1]
        npair = n_ref[bi, pair:pair + 1, :]
        dmat = jnp.where(causal, b_col - b_row + li_row, neg_inf)
        inter = b_col + m0
        mrow = jnp.maximum(inter, jnp.max(dmat, axis=-1, keepdims=True))
        iscale = jnp.exp(inter - mrow)
        s = s_raw[bi, h] * jnp.exp(dmat - mrow)
        qn = jnp.sum(qhs[bi, h] * npair, axis=-1, keepdims=True)
        dens_[bi, h] = iscale * qn + jnp.sum(s, axis=-1, keepdims=True)
        m_new = mrow[L - 1:L, :]
        b_last = b_col[L - 1:L, :]
        w_col = jnp.exp(b_last - b_col + li_col - m_new)
        decays[bi, h] = jnp.exp(b_last + m0 - m_new)
        kws[bi, h] = jnp.where(half_masks[half], kps[bi, h] * w_col, 0.0)
        s_w[bi, h], iscales[bi, h], mrows[bi, h], new_m[bi, h] = s, iscale, mrow, m_new

    d_c = {}
    for bi, h in units:
        rs = slice(bi * L, (bi + 1) * L)
        vh = vhs[bi, h]
        num = iscales[bi, h] * q_c[bi, h] + jnp.dot(mx(s_w[bi, h]), mx(vh), preferred_element_type=F32)
        h_ref[rs, h * M_V_DIM:(h + 1) * M_V_DIM] = \
            num / jnp.maximum(jnp.abs(dens_[bi, h]), jnp.exp(-mrows[bi, h]))
        d_c[bi, h] = lax.dot_general(mx(kws[bi, h]), mx(vh), (((0,), (0,)), ((), ())),
                                     preferred_element_type=F32)

    for bi in range(bt):
        for pair in range(M_HEADS // 2):
            h0, h1 = 2 * pair, 2 * pair + 1
            npair = n_ref[bi, pair:pair + 1, :]
            d_n = jnp.sum(kws[bi, h0], axis=0, keepdims=True) + jnp.sum(kws[bi, h1], axis=0, keepdims=True)
            c_ref[bi, pair] = jnp.where(rows_c < M_QK_DIM, decays[bi, h0], decays[bi, h1]) * c_ref[bi, pair] \
                + d_c[bi, h0] + d_c[bi, h1]
            n_ref[bi, pair:pair + 1, :] = \
                jnp.where(lanes_n < M_QK_DIM, decays[bi, h0], decays[bi, h1]) * npair + d_n
        for h in range(M_HEADS):
            m_ref[bi, h:h + 1, :] = jnp.broadcast_to(new_m[bi, h], (1, LANES_V7X))


def _mlstm(q, k, v, gc, gr, brow, bcol, c0, n0, m0, *, layer, nseq, nchunks, chunk, real, bt, mm_dtype):
    rows = bt * chunk
    grid = (nseq // bt, nchunks)
    nb = nseq // bt
    rmap = lambda b, c: (b * nchunks + c, 0)
    smap3 = lambda b, c: (b, 0, 0)
    smap4 = lambda b, c: (b, 0, 0, 0)
    c_spec = pl.BlockSpec((bt, 2, LANES_V7X, LANES_V7X), smap4)
    n_spec = pl.BlockSpec((bt, 2, LANES_V7X), smap3)
    m_spec = pl.BlockSpec((bt, SUBLANES_V7X, LANES_V7X), smap3)
    c_in = pl.BlockSpec((bt, 2, LANES_V7X, LANES_V7X), lambda b, c: (layer * nb + b, 0, 0, 0))
    n_in = pl.BlockSpec((bt, 2, LANES_V7X), lambda b, c: (layer * nb + b, 0, 0))
    n = q.shape[0]
    blk = 4 * (rows * (2 * QK_W + 2 * M_WIDTH + LANES_V7X + SUBLANES_V7X)
               + 2 * bt * (2 * LANES_V7X * LANES_V7X + 2 * LANES_V7X + SUBLANES_V7X * LANES_V7X))
    tmp = 4 * 16 * chunk * max(chunk, LANES_V7X)
    kern = functools.partial(_mlstm_kernel, chunk=chunk, real=real, bt=bt, mm_dtype=mm_dtype)
    return pl.pallas_call(
        kern,
        grid=grid,
        in_specs=[pl.BlockSpec((rows, QK_W), rmap), pl.BlockSpec((rows, QK_W), rmap),
                  pl.BlockSpec((rows, M_WIDTH), rmap), pl.BlockSpec((rows, LANES_V7X), rmap),
                  pl.BlockSpec((SUBLANES_V7X, rows), lambda b, c: (0, b * nchunks + c)),
                  _resident((1, LANES_V7X)), _resident((SUBLANES_V7X, LANES_V7X)),
                  c_in, n_in, m_spec],
        out_specs=[pl.BlockSpec((rows, M_WIDTH), rmap), c_spec, n_spec, m_spec],
        out_shape=[jax.ShapeDtypeStruct((n, M_WIDTH), F32),
                   jax.ShapeDtypeStruct((nseq, 2, LANES_V7X, LANES_V7X), F32),
                   jax.ShapeDtypeStruct((nseq, 2, LANES_V7X), F32),
                   jax.ShapeDtypeStruct((nseq, SUBLANES_V7X, LANES_V7X), F32)],
        compiler_params=_params(_vmem_limit(blk, 0, tmp), 2),
        name="mlstm",
    )(q, k, v, gc, gr, brow, bcol, c0, n0, m0)


def _mlstm_t_kernel(qt_ref, k_ref, vt_ref, gc_ref, gr_ref, brow_ref, bcol_ref, c0_ref, n0_ref, m0_ref,
                    h_ref, c_ref, n_ref, m_ref):
    L = M_CHUNK
    ci = pl.program_id(1)

    @pl.when(ci == 0)
    def _():
        c_ref[...] = c0_ref[...]
        n_ref[...] = n0_ref[...]
        m_ref[...] = m0_ref[...]

    def split3(a):
        hi = a.astype(BF16).astype(F32)
        r1 = a - hi
        mid = r1.astype(BF16).astype(F32)
        lo = (r1 - mid).astype(BF16).astype(F32)
        return hi, mid, lo

    si = lax.broadcasted_iota(jnp.int32, (L, L), 0)
    ti = lax.broadcasted_iota(jnp.int32, (L, L), 1)
    causal = si <= ti
    tri = (ti <= si).astype(F32)
    tri_t = causal.astype(F32)
    lane_g = lax.broadcasted_iota(jnp.int32, (L, LANES_V7X), 1)
    is_f_col = (lane_g >= M_HEADS) & (lane_g < 2 * M_HEADS)
    row_r = lax.broadcasted_iota(jnp.int32, (SUBLANES_V7X, L), 0)
    is_f_row = row_r >= M_HEADS
    half_masks = (lane_g < M_QK_DIM, lane_g >= M_QK_DIM)
    lanes_n = lax.broadcasted_iota(jnp.int32, (1, LANES_V7X), 1)
    neg_inf = jnp.float32(-jnp.inf)
    PADR = 2 * SUBLANES_V7X

    gcol = gc_ref[...] + brow_ref[...]
    gcol = jnp.where(is_f_col, _log_sigmoid(gcol), gcol)
    grow = gr_ref[...] + bcol_ref[...]
    grow = jnp.where(is_f_row, _log_sigmoid(grow), grow)
    cum_c = sum(jnp.dot(tri, p, preferred_element_type=F32)
                for p in split3(jnp.where(is_f_col, gcol, 0.0)))
    cum_r = sum(jnp.dot(p, tri_t, preferred_element_type=F32)
                for p in split3(jnp.where(is_f_row, grow, 0.0)))

    zq = jnp.zeros((M_QK_DIM, L), BF16)
    zpad = jnp.zeros((PADR - 1, LANES_V7X), F32)
    kps, s_raw, inter = [], [], []
    for h in range(M_HEADS):
        pair, half = divmod(h, 2)
        kp = k_ref[:, pair * LANES_V7X:(pair + 1) * LANES_V7X]
        qh = qt_ref[h * M_QK_DIM:(h + 1) * M_QK_DIM, :].astype(BF16)
        qz = jnp.concatenate([qh, zq] if half == 0 else [zq, qh], axis=0)
        caug = jnp.concatenate([c_ref[0, pair], n_ref[0, pair:pair + 1, :], zpad], axis=0)
        kps.append(kp)
        s_raw.append(jnp.dot(kp.astype(BF16), qz, preferred_element_type=F32))
        inter.append(jnp.dot(caug.astype(BF16), qz, preferred_element_type=F32))
    s_w, iscales, dens_, m_rows, w_rows, decays = [], [], [], [], [], []
    for h in range(M_HEADS):
        b_row = cum_r[M_HEADS + h:M_HEADS + h + 1, :]
        li_row = grow[h:h + 1, :]
        d_col = gcol[:, h:h + 1] - cum_c[:, M_HEADS + h:M_HEADS + h + 1]
        m0 = m_ref[0, h:h + 1, 0:1]
        dmat = jnp.where(causal, b_row + d_col, neg_inf)
        inter_m = b_row + m0
        m_row = jnp.maximum(inter_m, jnp.max(dmat, axis=0, keepdims=True))
        s = s_raw[h] * jnp.exp(dmat - m_row)
        iscale = jnp.exp(inter_m - m_row)
        dens_.append(iscale * inter[h][M_V_DIM:M_V_DIM + 1, :] + jnp.sum(s, axis=0, keepdims=True))
        m_new = m_row[:, L - 1:L]
        b_last = b_row[:, L - 1:L]
        w_rows.append(jnp.exp(b_last - b_row + li_row - m_new))
        decays.append(jnp.exp(b_last + m0 - m_new))
        s_w.append(s.astype(BF16))
        iscales.append(iscale)
        m_rows.append(m_row)
    outs, upds = [], []
    for h in range(M_HEADS):
        pair, half = divmod(h, 2)
        vt = vt_ref[h * M_V_DIM:(h + 1) * M_V_DIM, :]
        num = iscales[h] * inter[h][:M_V_DIM, :] + jnp.dot(vt.astype(BF16), s_w[h], preferred_element_type=F32)
        outs.append(num / jnp.maximum(jnp.abs(dens_[h]), jnp.exp(-m_rows[h])))
        vaug = jnp.concatenate([vt * w_rows[h], w_rows[h], jnp.zeros((PADR - 1, L), F32)], axis=0)
        kmask = jnp.where(half_masks[half], kps[h], 0.0)
        upds.append(jnp.dot(vaug.astype(BF16), kmask.astype(BF16), preferred_element_type=F32))
    for pair in range(M_HEADS // 2):
        dl = jnp.where(lanes_n < M_QK_DIM, decays[2 * pair], decays[2 * pair + 1])
        u0, u1 = upds[2 * pair], upds[2 * pair + 1]
        n_old = n_ref[0, pair:pair + 1, :]
        c_ref[0, pair] = dl * c_ref[0, pair] + u0[:M_V_DIM] + u1[:M_V_DIM]
        n_ref[0, pair:pair + 1, :] = dl * n_old + u0[M_V_DIM:M_V_DIM + 1] + u1[M_V_DIM:M_V_DIM + 1]
    for h in range(M_HEADS):
        m_ref[0, h:h + 1, :] = jnp.broadcast_to(m_rows[h][:, L - 1:L], (1, LANES_V7X))
    h_ref[...] = jnp.concatenate(outs, axis=0).T


def _mlstm_t(qt, k, vt, gc, gr, brow, bcol, c0, n0, m0, *, nseq, nchunks):
    L = M_CHUNK
    n = k.shape[0]
    rmap = lambda b, c: (b * nchunks + c, 0)
    cmap = lambda b, c: (0, b * nchunks + c)
    smap3 = lambda b, c: (b, 0, 0)
    smap4 = lambda b, c: (b, 0, 0, 0)
    c_spec = pl.BlockSpec((1, 2, LANES_V7X, LANES_V7X), smap4)
    n_spec = pl.BlockSpec((1, 2, LANES_V7X), smap3)
    m_spec = pl.BlockSpec((1, SUBLANES_V7X, LANES_V7X), smap3)
    blk = 4 * (L * (2 * QK_W + 2 * M_WIDTH + LANES_V7X + SUBLANES_V7X)
               + 2 * (2 * LANES_V7X * LANES_V7X + 2 * LANES_V7X + SUBLANES_V7X * LANES_V7X))
    tmp = 4 * 48 * L * L
    return pl.pallas_call(
        _mlstm_t_kernel,
        grid=(nseq, nchunks),
        in_specs=[pl.BlockSpec((QK_W, L), cmap), pl.BlockSpec((L, QK_W), rmap),
                  pl.BlockSpec((M_WIDTH, L), cmap), pl.BlockSpec((L, LANES_V7X), rmap),
                  pl.BlockSpec((SUBLANES_V7X, L), cmap),
                  _resident((1, LANES_V7X)), _resident((SUBLANES_V7X, LANES_V7X)),
                  c_spec, n_spec, m_spec],
        out_specs=[pl.BlockSpec((L, M_WIDTH), rmap), c_spec, n_spec, m_spec],
        out_shape=[jax.ShapeDtypeStruct((n, M_WIDTH), F32),
                   jax.ShapeDtypeStruct((nseq, 2, LANES_V7X, LANES_V7X), F32),
                   jax.ShapeDtypeStruct((nseq, 2, LANES_V7X), F32),
                   jax.ShapeDtypeStruct((nseq, SUBLANES_V7X, LANES_V7X), F32)],
        compiler_params=_params(_vmem_limit(blk, 0, tmp), 2),
        name="mlstm_t",
    )(qt, k, vt, gc, gr, brow, bcol, c0, n0, m0)


def _swa_prompt_kernel(qt_ref, kc_ref, kp_ref, vtc_ref, vtp_ref, sink_ref, o_ref, vlast_ref):
    blk = pl.program_id(1)
    L = WINDOW

    @pl.when(blk == pl.num_programs(1) - 1)
    def _():
        vlast_ref[0] = vtc_ref[...].T

    key = lax.broadcasted_iota(jnp.int32, (2 * L, L), 0)
    qry = lax.broadcasted_iota(jnp.int32, (2 * L, L), 1)
    prev_off = jnp.where(blk > 0, 0, L)
    mask = ((key < L) & (key >= qry + prev_off)) | ((key >= L) & (key - L <= qry))
    k2 = jnp.concatenate([kp_ref[...], kc_ref[...]], axis=0).astype(BF16)
    vt2 = jnp.concatenate([vtp_ref[...], vtc_ref[...]], axis=1).astype(BF16)
    zeros = jnp.zeros((A_HEAD_DIM, L), BF16)
    scores = []
    for hd in range(A_HEADS):
        qh = qt_ref[hd * A_HEAD_DIM:(hd + 1) * A_HEAD_DIM, :].astype(BF16)
        qz = jnp.concatenate([qh, zeros] if hd < A_GROUP else [zeros, qh], axis=0)
        scores.append(jnp.dot(k2, qz, preferred_element_type=F32))
    probs = []
    dens = []
    for hd in range(A_HEADS):
        s = jnp.where(mask, scores[hd], -jnp.inf)
        sink = sink_ref[hd:hd + 1, 0:1]
        m = jnp.maximum(jnp.max(s, axis=0, keepdims=True), sink)
        p = jnp.exp(s - m)
        dens.append(jnp.sum(p, axis=0, keepdims=True) + jnp.exp(sink - m))
        probs.append(p.astype(BF16))
    outs = []
    for hd in range(A_HEADS):
        kv = hd // A_GROUP
        o = jnp.dot(vt2[kv * A_HEAD_DIM:(kv + 1) * A_HEAD_DIM, :], probs[hd],
                    preferred_element_type=F32)
        outs.append(o / dens[hd])
    o_ref[...] = jnp.concatenate(outs, axis=0).T


def _swa_prompt(qt, k, vt, sinks, *, nseq, nblk):
    n = k.shape[0]
    L = WINDOW
    cur = lambda b, i: (b * nblk + i, 0)
    prev = lambda b, i: (b * nblk + jnp.maximum(i - 1, 0), 0)
    cur_t = lambda b, i: (0, b * nblk + i)
    prev_t = lambda b, i: (0, b * nblk + jnp.maximum(i - 1, 0))
    blk = 4 * L * (2 * A_WIDTH + 4 * KV_W)
    tmp = 4 * 24 * 2 * L * L
    return pl.pallas_call(
        _swa_prompt_kernel,
        grid=(nseq, nblk),
        in_specs=[pl.BlockSpec((A_WIDTH, L), cur_t),
                  pl.BlockSpec((L, KV_W), cur), pl.BlockSpec((L, KV_W), prev),
                  pl.BlockSpec((KV_W, L), cur_t), pl.BlockSpec((KV_W, L), prev_t),
                  _resident((A_HEADS, LANES_V7X))],
        out_specs=[pl.BlockSpec((L, A_WIDTH), cur), pl.BlockSpec((1, L, KV_W), lambda b, i: (b, 0, 0))],
        out_shape=[jax.ShapeDtypeStruct((n, A_WIDTH), F32), jax.ShapeDtypeStruct((nseq, L, KV_W), F32)],
        compiler_params=_params(_vmem_limit(blk, 0, tmp), 2),
        name="swa_prompt",
    )(qt, k, k, vt, vt, sinks)


def _swa_sample_kernel(q_ref, kn_ref, vn_ref, kc_ref, vc_ref, sink_ref, o_ref, ko_ref, vo_ref,
                       *, bt, real):
    P = SAMPLE_PAD_T
    W = WINDOW
    R = A_HEADS * P
    row = lax.broadcasted_iota(jnp.int32, (R, 2 * W), 0)
    key = lax.broadcasted_iota(jnp.int32, (R, 2 * W), 1)
    step = row % P
    mask = ((key < W) & (key >= step)) | ((key >= W) & (key - W <= step) & (key - W < real))
    sink = sink_ref[:, 0:1]
    pad = jnp.zeros((W - P, KV_W), F32)
    scores = []
    for bi in range(bt):
        kext = jnp.concatenate([kc_ref[bi], kn_ref[bi * P:(bi + 1) * P, :], pad], axis=0)
        scores.append(lax.dot_general(q_ref[bi], kext, _NT, preferred_element_type=F32))
    probs = []
    dens = []
    for bi in range(bt):
        s = jnp.where(mask, scores[bi], -jnp.inf)
        m = jnp.maximum(jnp.max(s, axis=-1, keepdims=True), sink)
        p = jnp.exp(s - m)
        dens.append(jnp.sum(p, axis=-1, keepdims=True) + jnp.exp(sink - m))
        probs.append(p)
    for bi in range(bt):
        rs = slice(bi * P, (bi + 1) * P)
        vn = vn_ref[rs, :]
        vext = jnp.concatenate([vc_ref[bi], vn, pad], axis=0)
        o_ref[bi] = jnp.dot(probs[bi], vext, preferred_element_type=F32) / dens[bi]
        ko_ref[bi, 0:W - real, :] = kc_ref[bi, real:W, :]
        ko_ref[bi, W - real:W, :] = kn_ref[bi * P:bi * P + real, :]
        vo_ref[bi, 0:W - real, :] = vc_ref[bi, real:W, :]
        vo_ref[bi, W - real:W, :] = vn[0:real, :]


def _swa_sample(q_rows, kn, vn, kcache, vcache, sinks, *, layer, nseq, bt, real):
    rows = bt * SAMPLE_PAD_T
    R = A_HEADS * SAMPLE_PAD_T
    nb = nseq // bt
    rmap = lambda i: (i, 0)
    cmap = lambda i: (layer * nb + i, 0, 0)
    omap = lambda i: (i, 0, 0)
    cache = pl.BlockSpec((bt, WINDOW, KV_W), cmap)
    blk = 4 * (2 * bt * R * KV_W + 2 * rows * KV_W + 4 * bt * WINDOW * KV_W)
    tmp = 4 * 16 * R * 2 * WINDOW
    kern = functools.partial(_swa_sample_kernel, bt=bt, real=real)
    return pl.pallas_call(
        kern,
        grid=(nb,),
        in_specs=[pl.BlockSpec((bt, R, KV_W), omap), pl.BlockSpec((rows, KV_W), rmap),
                  pl.BlockSpec((rows, KV_W), rmap), cache, cache, _resident((R, LANES_V7X))],
        out_specs=[pl.BlockSpec((bt, R, KV_W), omap),
                   pl.BlockSpec((bt, WINDOW, KV_W), omap), pl.BlockSpec((bt, WINDOW, KV_W), omap)],
        out_shape=[jax.ShapeDtypeStruct((nseq, R, KV_W), F32),
                   jax.ShapeDtypeStruct((nseq, WINDOW, KV_W), F32),
                   jax.ShapeDtypeStruct((nseq, WINDOW, KV_W), F32)],
        compiler_params=_params(_vmem_limit(blk, 0, tmp), 1),
        name="swa_sample",
    )(q_rows, kn, vn, kcache, vcache, sinks)


def _merge_kernel(hm_ref, mo_ref, ha_ref, gt_ref, x_ref, gmh_ref, wa_ref, wb_ref, wo_ref, o_ref):
    hm = hm_ref[...]
    parts = []
    for h in range(M_HEADS):
        v = hm[:, h * M_V_DIM:(h + 1) * M_V_DIM]
        parts.append(v * lax.rsqrt(jnp.mean(v * v, axis=-1, keepdims=True) + EPS))
    hn = jnp.concatenate(parts, axis=1) * gmh_ref[...] * jax.nn.sigmoid(mo_ref[...])
    a = jnp.dot(hn.astype(BF16), wa_ref[...], preferred_element_type=F32)
    b = jnp.dot(ha_ref[...].astype(BF16), wb_ref[...], preferred_element_type=F32)
    gates = jax.nn.sigmoid(gt_ref[...])
    mixed = gates[:, :D_MODEL] * a + gates[:, D_MODEL:] * b
    o_ref[...] = x_ref[...] + jnp.dot(mixed.astype(BF16), wo_ref[...], preferred_element_type=F32)


def _merge(hm, mo, ha, gt, x2d, gmh, wa, wb, wo):
    n = x2d.shape[0]
    t = min(ROW_TILE, n)
    row = lambda w: pl.BlockSpec((t, w), lambda i: (i, 0))
    blk = 4 * t * (3 * M_WIDTH + GATE_W + 2 * D_MODEL)
    res = 2 * (2 * M_WIDTH * D_MODEL + D_MODEL * D_MODEL)
    tmp = 4 * t * (GATE_W + 3 * D_MODEL)
    return pl.pallas_call(
        _merge_kernel,
        grid=(n // t,),
        in_specs=[row(M_WIDTH), row(M_WIDTH), row(A_WIDTH), row(GATE_W), row(D_MODEL),
                  _resident((1, M_WIDTH)), _resident((M_WIDTH, D_MODEL)),
                  _resident((A_WIDTH, D_MODEL)), _resident((D_MODEL, D_MODEL))],
        out_specs=row(D_MODEL),
        out_shape=jax.ShapeDtypeStruct((n, D_MODEL), F32),
        compiler_params=_params(_vmem_limit(blk, res, tmp), 1),
        name="merge",
    )(hm, mo, ha, gt, x2d, gmh, wa, wb, wo)


def _ffn_kernel(x_ref, gn_ref, halo_ref, wup_ref, wc_ref, wdn_ref, gfin_ref, o_ref, halo_out_ref,
                xn_s, acc_s, halo_s, *, g, final_norm):
    t = x_ref.shape[0]
    rb = min(FFN_ROWS, t)
    hp = halo_s.shape[1]
    ti = pl.program_id(1)
    last = pl.num_programs(1) - 1

    @pl.when(ti == 0)
    def _():
        halo_s[...] = jnp.zeros_like(halo_s)
        for j in range(N_FF_CHUNKS):
            halo_s[j, hp - 2 * g:hp, :FF_CHUNK] = halo_ref[0, :, j * FF_CHUNK:(j + 1) * FF_CHUNK]
            halo_s[j, hp - 2 * g:hp, FF_CHUNK:] = halo_ref[0, :, D_FF + j * FF_CHUNK:D_FF + (j + 1) * FF_CHUNK]

    x = x_ref[...]
    xn_s[...] = (x * lax.rsqrt(jnp.mean(x * x, axis=-1, keepdims=True) + EPS) * gn_ref[...]).astype(BF16)

    for j in range(N_FF_CHUNKS):
        w_up = wup_ref[j]
        us = [jnp.dot(xn_s[r:r + rb, :], w_up, preferred_element_type=F32) for r in range(0, t, rb)]
        wc = wc_ref[j]
        prev = halo_s[j]
        acts = []
        for u in us:
            ext = jnp.concatenate([prev, u], axis=0)
            u1 = ext[hp - g:hp - g + rb]
            u2 = ext[hp - 2 * g:hp - 2 * g + rb]
            prev = ext[rb:rb + hp]
            c = wc[3:4, :] + wc[0:1, :] * u2 + wc[1:2, :] * u1 + wc[2:3, :] * u
            a = c[:, :FF_CHUNK]
            act = 0.5 * a * (1.0 + lax.erf(a * np.float32(np.sqrt(0.5)))) * c[:, FF_CHUNK:]
            acts.append(act.astype(BF16))
        halo_s[j] = prev
        w_dn = wdn_ref[j]
        for i, act in enumerate(acts):
            d = jnp.dot(act, w_dn, preferred_element_type=F32)
            if j == 0:
                acc_s[i * rb:(i + 1) * rb, :] = d
            else:
                acc_s[i * rb:(i + 1) * rb, :] += d

    y = x + acc_s[...]
    if final_norm:
        y = y * lax.rsqrt(jnp.mean(y * y, axis=-1, keepdims=True) + EPS) * gfin_ref[...]
    o_ref[...] = y

    @pl.when(ti == last)
    def _():
        for j in range(N_FF_CHUNKS):
            halo_out_ref[0, :, j * FF_CHUNK:(j + 1) * FF_CHUNK] = halo_s[j, hp - 2 * g:hp, :FF_CHUNK]
            halo_out_ref[0, :, D_FF + j * FF_CHUNK:D_FF + (j + 1) * FF_CHUNK] = halo_s[j, hp - 2 * g:hp, FF_CHUNK:]


def _ffn(x2d, gn, halo, wup, wc, wdn, gfin, *, nseq, t, g, final_norm):
    n = x2d.shape[0]
    tiles = n // (nseq * t)
    hp = -(-2 * g // SUBLANES_V7X) * SUBLANES_V7X
    rmap = lambda s, i: (s * tiles + i, 0)
    hmap = lambda s, i: (s, 0, 0)
    blk = 4 * (2 * t * D_MODEL + 2 * 2 * g * 2 * D_FF)
    res = 2 * (D_MODEL * 2 * D_FF + D_FF * D_MODEL) + 4 * N_FF_CHUNKS * SUBLANES_V7X * 2 * FF_CHUNK
    scr = 2 * t * D_MODEL + 4 * t * D_MODEL + 4 * N_FF_CHUNKS * hp * 2 * FF_CHUNK
    tmp = 4 * t * (8 * FF_CHUNK + D_MODEL)
    kern = functools.partial(_ffn_kernel, g=g, final_norm=final_norm)
    return pl.pallas_call(
        kern,
        grid=(nseq, tiles),
        in_specs=[pl.BlockSpec((t, D_MODEL), rmap), _resident((1, D_MODEL)),
                  pl.BlockSpec((1, 2 * g, 2 * D_FF), hmap),
                  _resident((N_FF_CHUNKS, D_MODEL, 2 * FF_CHUNK)),
                  _resident((N_FF_CHUNKS, SUBLANES_V7X, 2 * FF_CHUNK)),
                  _resident((N_FF_CHUNKS, FF_CHUNK, D_MODEL)), _resident((1, D_MODEL))],
        out_specs=[pl.BlockSpec((t, D_MODEL), rmap), pl.BlockSpec((1, 2 * g, 2 * D_FF), hmap)],
        out_shape=[jax.ShapeDtypeStruct((n, D_MODEL), F32),
                   jax.ShapeDtypeStruct((nseq, 2 * g, 2 * D_FF), F32)],
        scratch_shapes=[pltpu.VMEM((t, D_MODEL), BF16), pltpu.VMEM((t, D_MODEL), F32),
                        pltpu.VMEM((N_FF_CHUNKS, hp, 2 * FF_CHUNK), F32)],
        compiler_params=_params(_vmem_limit(blk, res + scr, tmp), 2),
        name="convffn",
    )(x2d, gn, halo, wup, wc, wdn, gfin)


def _prep_layer(l, g_mix, w_in, b_igate, b_fgate, g_mhead, a_sinks, w_a, w_b, w_out, g_ffn, w_up,
                w_conv, b_conv, w_down):
    w = w_in[l]
    seg = lambda i: w[:, _OFF[i]:_OFF[i + 1]]
    w_main = jnp.concatenate([seg(1), seg(3), seg(7), seg(9)], axis=1).astype(BF16)
    w_flex = tuple(seg(i).astype(BF16) for i in (0, 2, 6, 8))
    w_gate8 = jnp.concatenate([seg(4), seg(5)], axis=1)
    w_gate = jnp.pad(w_gate8, ((0, 0), (0, LANES_V7X - 2 * M_HEADS))).astype(BF16)
    w_gate_t = jnp.pad(w_gate8.T, ((0, SUBLANES_V7X), (0, 0))).astype(BF16)
    bias8 = jnp.concatenate([b_igate[l], b_fgate[l]]).astype(F32)
    brow = jnp.pad(bias8, (0, LANES_V7X - 2 * M_HEADS)).reshape(1, LANES_V7X)
    bcol = jnp.broadcast_to(bias8[:, None], (SUBLANES_V7X, LANES_V7X))
    sink8 = a_sinks[l].reshape(A_HEADS, 1).astype(F32)
    sinks = jnp.broadcast_to(sink8, (A_HEADS, LANES_V7X))
    sink_rows = jnp.broadcast_to(jnp.repeat(sink8, SAMPLE_PAD_T, axis=0), (A_HEADS * SAMPLE_PAD_T, LANES_V7X))
    up = w_up[l]
    wup = jnp.stack([jnp.concatenate([up[:, j * FF_CHUNK:(j + 1) * FF_CHUNK],
                                      up[:, D_FF + j * FF_CHUNK:D_FF + (j + 1) * FF_CHUNK]], axis=1)
                     for j in range(N_FF_CHUNKS)]).astype(BF16)
    cw = jnp.concatenate([w_conv[l], b_conv[l][None, :]], axis=0)
    cw = jnp.pad(cw, ((0, SUBLANES_V7X - CONV_W - 1), (0, 0))).astype(F32)
    wc = jnp.stack([jnp.concatenate([cw[:, j * FF_CHUNK:(j + 1) * FF_CHUNK],
                                     cw[:, D_FF + j * FF_CHUNK:D_FF + (j + 1) * FF_CHUNK]], axis=1)
                    for j in range(N_FF_CHUNKS)])
    wdn = w_down[l].reshape(N_FF_CHUNKS, FF_CHUNK, D_MODEL).astype(BF16)
    return dict(g_mix=g_mix[l].reshape(1, D_MODEL), w_main=w_main, w_flex=w_flex,
                w_flex_t=tuple(w.T for w in w_flex), w_gate=w_gate, w_gate_t=w_gate_t,
                brow=brow, bcol=bcol, gmh=g_mhead[l].reshape(1, M_WIDTH), sinks=sinks, sink_rows=sink_rows,
                wa=w_a[l].astype(BF16), wb=w_b[l].astype(BF16), wo=w_out[l].astype(BF16),
                g_ffn=g_ffn[l].reshape(1, D_MODEL), wup=wup, wc=wc, wdn=wdn)


def _rope_tables(pos):
    half = A_HEAD_DIM // 2
    inv = ROPE_THETA ** (-jnp.arange(0, A_HEAD_DIM, 2, dtype=F32) / A_HEAD_DIM)
    ang = pos.astype(F32)[:, None] * inv[None, :]
    ang = jnp.tile(jnp.concatenate([ang, ang], axis=-1), (1, LANES_V7X // A_HEAD_DIM))
    sign = jnp.where((jnp.arange(LANES_V7X) % A_HEAD_DIM) < half, -1.0, 1.0).astype(F32)
    return jnp.cos(ang), jnp.sin(ang) * sign[None, :]


def _rope_tables_t(pos):
    inv = ROPE_THETA ** (-jnp.arange(0, A_HEAD_DIM, 2, dtype=F32) / A_HEAD_DIM)
    ang = inv[:, None] * pos.astype(F32)[None, :]
    return jnp.cos(ang), jnp.sin(ang)


def _prompt_trunk(x, layers, g_final):
    B, S, _ = x.shape
    n = B * S
    x2d = x.reshape(n, D_MODEL)
    cos, sin = _rope_tables(jnp.arange(S, dtype=jnp.int32))
    cos_t, sin_t = _rope_tables_t(jnp.arange(S, dtype=jnp.int32))
    nchunks = S // M_CHUNK
    c0 = jnp.zeros((B, 2, LANES_V7X, LANES_V7X), F32)
    n0 = jnp.zeros((B, 2, LANES_V7X), F32)
    m0 = jnp.zeros((B, SUBLANES_V7X, LANES_V7X), F32)
    halo0 = jnp.zeros((B, CONV_W - 1, 2 * D_FF), F32)
    outs = dict(C=[], n=[], m=[], k=[], v=[], conv=[])
    for l, p in enumerate(layers):
        mk, mo, ak, gt, mq_t, mv_t, aq_t, av_t, gc, gr = _inproj(
            x2d, p["g_mix"], p["w_main"], p["w_flex_t"], p["w_gate"], p["w_gate_t"],
            cos, sin, cos_t, sin_t, S // ROW_TILE, True)
        hm, c_new, n_new, m_new = _mlstm_t(mq_t, mk, mv_t, gc, gr, p["brow"], p["bcol"], c0, n0, m0,
                                           nseq=B, nchunks=nchunks)
        ha, v_last = _swa_prompt(aq_t, ak, av_t, p["sinks"], nseq=B, nblk=S // WINDOW)
        x2d = _merge(hm, mo, ha, gt, x2d, p["gmh"], p["wa"], p["wb"], p["wo"])
        x2d, conv = _ffn(x2d, p["g_ffn"], halo0, p["wup"], p["wc"], p["wdn"], g_final,
                         nseq=B, t=ROW_TILE, g=1, final_norm=(l == DEPTH - 1))
        c_new = jnp.transpose(c_new.reshape(B, M_HEADS // 2, M_V_DIM, 2, M_QK_DIM), (0, 1, 3, 4, 2))
        outs["C"].append(c_new.reshape(B, M_HEADS, M_QK_DIM, M_V_DIM))
        outs["n"].append(n_new.reshape(B, M_HEADS, M_QK_DIM))
        outs["m"].append(m_new[:, :M_HEADS, 0])
        outs["k"].append(ak.reshape(B, S, A_KV_HEADS, A_HEAD_DIM)[:, S - WINDOW:])
        outs["v"].append(v_last.reshape(B, WINDOW, A_KV_HEADS, A_HEAD_DIM))
        outs["conv"].append(conv)
    return (x2d.reshape(B, S, D_MODEL),) + tuple(jnp.stack(outs[k]) for k in ("C", "n", "m", "k", "v", "conv"))


def _sample_trunk(x, st_c, st_n, st_m, st_k, st_v, st_conv, layers, g_final):
    B, T, _ = x.shape
    P = SAMPLE_PAD_T
    bt = LANES_V7X // P
    gb = SAMPLE_FFN_GROUP
    ng = B // gb
    pos =PAST_LEN + (jnp.arange(B * P, dtype=jnp.int32) % P)
    cos, sin = _rope_tables(pos)
    c_all = st_c.reshape(DEPTH * B, 2, LANES_V7X, LANES_V7X)
    n_all = st_n.reshape(DEPTH * B, 2, LANES_V7X)
    k_all = st_k.reshape(DEPTH * B, WINDOW, KV_W)
    v_all = st_v.reshape(DEPTH * B, WINDOW, KV_W)
    xpad = jnp.pad(x, ((0, 0), (0, P - T), (0, 0))).reshape(B * P, D_MODEL)
    outs = dict(C=[], n=[], m=[], k=[], v=[], conv=[])
    y_bm = None
    for l, p in enumerate(layers):
        mk, mo, ak, gt, mq, mv, aq, av, gc, gr = _inproj(
            xpad, p["g_mix"], p["w_main"], p["w_flex"], p["w_gate"], p["w_gate_t"],
            cos, sin, cos, sin, (B * P) // ROW_TILE, False)
        m0 = jnp.broadcast_to(jnp.pad(st_m[l], ((0, 0), (0, SUBLANES_V7X - M_HEADS)))[:, :, None],
                              (B, SUBLANES_V7X, LANES_V7X))
        hm, c_new, n_new, m_new = _mlstm(mq, mk, mv, gc, gr, p["brow"], p["bcol"], c_all, n_all, m0,
                                         layer=l, nseq=B, nchunks=1, chunk=P, real=T, bt=bt, mm_dtype=F32)
        q5 = jnp.transpose(aq.reshape(B, P, A_KV_HEADS, A_GROUP, A_HEAD_DIM), (0, 2, 3, 1, 4))
        q5 = q5.reshape(B, A_KV_HEADS, A_GROUP * P, A_HEAD_DIM)
        zq = jnp.zeros_like(q5[:, 0])
        q_rows = jnp.concatenate([jnp.concatenate([q5[:, 0], zq], axis=-1),
                                  jnp.concatenate([zq, q5[:, 1]], axis=-1)], axis=1)
        o_rows, k_new, v_new = _swa_sample(q_rows, ak, av, k_all, v_all, p["sink_rows"],
                                           layer=l, nseq=B, bt=bt, real=T)
        o5 = jnp.stack([o_rows[:, :A_GROUP * P, :A_HEAD_DIM], o_rows[:, A_GROUP * P:, A_HEAD_DIM:]], axis=1)
        ha = jnp.transpose(o5.reshape(B, A_KV_HEADS, A_GROUP, P, A_HEAD_DIM),
                           (0, 3, 1, 2, 4)).reshape(B * P, A_WIDTH)
        xm = _merge(hm, mo, ha, gt, xpad, p["gmh"], p["wa"], p["wb"], p["wo"])
        x_tm = jnp.transpose(xm.reshape(ng, gb, P, D_MODEL)[:, :, :T], (0, 2, 1, 3)).reshape(T * B, D_MODEL)
        halo = jnp.transpose(st_conv[l].reshape(ng, gb, CONV_W - 1, 2 * D_FF),
                             (0, 2, 1, 3)).reshape(ng, (CONV_W - 1) * gb, 2 * D_FF)
        y_tm, conv = _ffn(x_tm, p["g_ffn"], halo, p["wup"], p["wc"], p["wdn"], g_final,
                          nseq=ng, t=T * gb, g=gb, final_norm=(l == DEPTH - 1))
        y_bm = jnp.transpose(y_tm.reshape(ng, T, gb, D_MODEL), (0, 2, 1, 3)).reshape(B, T, D_MODEL)
        xpad = jnp.pad(y_bm, ((0, 0), (0, P - T), (0, 0))).reshape(B * P, D_MODEL)
        outs["C"].append(c_new.reshape(B, M_HEADS, M_QK_DIM, M_V_DIM))
        outs["n"].append(n_new.reshape(B, M_HEADS, M_QK_DIM))
        outs["m"].append(m_new[:, :M_HEADS, 0])
        outs["k"].append(k_new.reshape(B, WINDOW, A_KV_HEADS, A_HEAD_DIM))
        outs["v"].append(v_new.reshape(B, WINDOW, A_KV_HEADS, A_HEAD_DIM))
        outs["conv"].append(jnp.transpose(conv.reshape(ng, CONV_W - 1, gb, 2 * D_FF),
                                          (0, 2, 1, 3)).reshape(B, CONV_W - 1, 2 * D_FF))
    return (y_bm,) + tuple(jnp.stack(outs[k]) for k in ("C", "n", "m", "k", "v", "conv"))


def kernel(x_prompt, x_sample, state_mlstm_C, state_mlstm_n, state_mlstm_m, cache_swa_k, cache_swa_v,
           state_ffn_conv, g_mix, w_in, b_igate, b_fgate, g_mhead, a_sinks, w_a, w_b, w_out, g_ffn,
           w_up, w_conv, b_conv, w_down, g_final):
    layers = [_prep_layer(l, g_mix, w_in, b_igate, b_fgate, g_mhead, a_sinks, w_a, w_b, w_out, g_ffn,
                          w_up, w_conv, b_conv, w_down) for l in range(DEPTH)]
    gfin = g_final.reshape(1, D_MODEL)
    yp, p_c, p_n, p_m, p_k, p_v, p_conv = _prompt_trunk(x_prompt, layers, gfin)
    ys, s_c, s_n, s_m, s_k, s_v, s_conv = _sample_trunk(
        x_sample, state_mlstm_C, state_mlstm_n, state_mlstm_m, cache_swa_k, cache_swa_v,
        state_ffn_conv, layers, gfin)
    return (yp, ys, p_c, p_n, p_m, p_k, p_v, p_conv, s_c, s_n, s_m, s_k, s_v, s_conv)
```

```python
import functools

import numpy as np
import jax
import jax.numpy as jnp
from jax import lax
from jax.experimental import pallas as pl
from jax.experimental.pallas import tpu as pltpu

F32 = jnp.float32
BF16 = jnp.bfloat16

D_MODEL = 1024
DEPTH = 2
PAST_LEN = 16384
M_HEADS = 4
M_QK_DIM = 64
M_V_DIM = 128
M_WIDTH = M_HEADS * M_V_DIM
M_CHUNK = 128
A_HEADS = 8
A_KV_HEADS = 2
A_GROUP = A_HEADS // A_KV_HEADS
A_HEAD_DIM = 64
A_WIDTH = A_HEADS * A_HEAD_DIM
WINDOW = 128
ROPE_THETA = 10000.0
D_FF = 2816
CONV_W = 3
EPS = 1e-6

QK_W = M_HEADS * M_QK_DIM
KV_W = A_KV_HEADS * A_HEAD_DIM
GATE_W = 2 * D_MODEL
_SPLIT = (QK_W, QK_W, M_WIDTH, M_WIDTH, M_HEADS, M_HEADS, A_WIDTH, KV_W, KV_W, GATE_W)
_OFF = np.concatenate([[0], np.cumsum(_SPLIT)]).tolist()
_MAIN_W = (QK_W, M_WIDTH, KV_W, GATE_W)
_MAIN_OFF = np.concatenate([[0], np.cumsum(_MAIN_W)]).tolist()
MAIN_COLS = _MAIN_OFF[-1]

LANES_V7X = 128
SUBLANES_V7X = 8
VMEM_BYTES_V7X = 64 * 1024 * 1024
VMEM_LIMIT_CAP_V7X = 60000 * 1024

SAMPLE_PAD_T = SUBLANES_V7X
FF_CHUNK = 2 * LANES_V7X
N_FF_CHUNKS = D_FF // FF_CHUNK
ROW_TILE = 512
SAMPLE_FFN_GROUP = 32
FFN_TILE = 1024
FFN_ROWS = 128
INPROJ_ROWS = 256


def _vmem_limit(block_bytes, scratch_bytes, temp_bytes):
    need = 2 * block_bytes + scratch_bytes + temp_bytes
    assert need <= VMEM_LIMIT_CAP_V7X, need
    return VMEM_LIMIT_CAP_V7X


def _params(vmem_bytes, n_axes):
    return pltpu.CompilerParams(dimension_semantics=("arbitrary",) * n_axes,
                                vmem_limit_bytes=vmem_bytes)


def _resident(shape):
    nd = len(shape)
    return pl.BlockSpec(shape, lambda *_: (0,) * nd, pipeline_mode=pl.Buffered(1))


def _log_sigmoid(x):
    return jnp.minimum(x, 0.0) - jnp.log1p(jnp.exp(-jnp.abs(x)))


_NT = (((1,), (1,)), ((), ()))


def _inproj_kernel(x_ref, g_ref, w_ref, wmq_ref, wmv_ref, wq_ref, wv_ref, wg_ref, wgt_ref, cos_ref,
                   sin_ref, cosq_ref, sinq_ref, mk_ref, mo_ref, ak_ref, gt_ref, mq_ref, mv_ref, aq_ref,
                   av_ref, gc_ref, gr_ref, *, tokens_on_lanes):
    t = x_ref.shape[0]
    rb = min(INPROJ_ROWS, t)
    lane = lax.broadcasted_iota(jnp.int32, (rb, LANES_V7X), 1)
    first_half = (lane % A_HEAD_DIM) < (A_HEAD_DIM // 2)
    half = A_HEAD_DIM // 2
    scale = A_HEAD_DIM ** -0.5

    def rope(v, c, s):
        rot = jnp.where(first_half,
                        pltpu.roll(v, LANES_V7X - A_HEAD_DIM // 2, 1),
                        pltpu.roll(v, A_HEAD_DIM // 2, 1))
        return v * c + rot * s

    for r0 in range(0, t, rb):
        rows = slice(r0, r0 + rb)
        x = x_ref[rows, :]
        xn = (x * lax.rsqrt(jnp.mean(x * x, axis=-1, keepdims=True) + EPS) * g_ref[...]).astype(BF16)

        def proj(idx):
            return jnp.dot(xn, w_ref[:, _MAIN_OFF[idx]:_MAIN_OFF[idx + 1]], preferred_element_type=F32)

        def flex(w_r):
            if tokens_on_lanes:
                return lax.dot_general(w_r[...], xn, _NT, preferred_element_type=F32)
            return jnp.dot(xn, w_r[...], preferred_element_type=F32)

        mk, mo, ak, gt = proj(0), proj(1), proj(2), proj(3)
        mq, mv, aq, av = flex(wmq_ref), flex(wmv_ref), flex(wq_ref), flex(wv_ref)
        gc = jnp.dot(xn, wg_ref[...], preferred_element_type=F32)
        gr = lax.dot_general(wgt_ref[...], xn, _NT, preferred_element_type=F32)

        mk_ref[rows, :] = mk
        mo_ref[rows, :] = mo
        gt_ref[rows, :] = gt
        ak_ref[rows, :] = rope(ak, cos_ref[rows, :], sin_ref[rows, :])
        gc_ref[rows, :] = gc
        gr_ref[:, rows] = gr[:SUBLANES_V7X]
        if tokens_on_lanes:
            cq = cosq_ref[:, rows]
            sq = sinq_ref[:, rows]
            mq_ref[:, rows] = mq * (M_QK_DIM ** -0.5)
            mv_ref[:, rows] = mv
            av_ref[:, rows] = av
            for h in range(A_HEADS):
                x1 = aq[h * A_HEAD_DIM:h * A_HEAD_DIM + half]
                x2 = aq[h * A_HEAD_DIM + half:(h + 1) * A_HEAD_DIM]
                aq_ref[h * A_HEAD_DIM:h * A_HEAD_DIM + half, rows] = (x1 * cq - x2 * sq) * scale
                aq_ref[h * A_HEAD_DIM + half:(h + 1) * A_HEAD_DIM, rows] = (x2 * cq + x1 * sq) * scale
        else:
            cq = cosq_ref[rows, :]
            sq = sinq_ref[rows, :]
            mq_ref[rows, :] = mq * (M_QK_DIM ** -0.5)
            mv_ref[rows, :] = mv
            av_ref[rows, :] = av
            for c in range(A_WIDTH // LANES_V7X):
                sl = slice(c * LANES_V7X, (c + 1) * LANES_V7X)
                aq_ref[rows, sl] = rope(aq[:, sl], cq, sq) * scale


_FLEX_W = (QK_W, M_WIDTH, A_WIDTH, KV_W)


def _inproj(x2d, g, w_main, w_flex, w_gate, w_gate_t, cos, sin, cos_q, sin_q, tiles_per_seq,
            tokens_on_lanes):
    n = x2d.shape[0]
    t = min(ROW_TILE, n)
    grid = (n // t,)
    row = lambda w: pl.BlockSpec((t, w), lambda i: (i, 0))
    col = lambda h: pl.BlockSpec((h, t), lambda i: (0, i))
    tab = pl.BlockSpec((t, LANES_V7X), lambda i: (i % tiles_per_seq, 0))
    if tokens_on_lanes:
        qtab = pl.BlockSpec((A_HEAD_DIM // 2, t), lambda i: (0, i % tiles_per_seq))
        flex_specs = [col(w) for w in _FLEX_W]
        flex_shapes = [(w, n) for w in _FLEX_W]
    else:
        qtab = tab
        flex_specs = [row(w) for w in _FLEX_W]
        flex_shapes = [(n, w) for w in _FLEX_W]
    out_shapes = [jax.ShapeDtypeStruct((n, w), F32) for w in _MAIN_W]
    out_shapes += [jax.ShapeDtypeStruct(s, F32) for s in flex_shapes]
    out_shapes += [jax.ShapeDtypeStruct((n, LANES_V7X), F32), jax.ShapeDtypeStruct((SUBLANES_V7X, n), F32)]
    out_specs = [row(w) for w in _MAIN_W] + flex_specs + [row(LANES_V7X), col(SUBLANES_V7X)]
    flex_cols = sum(_FLEX_W)
    blk = 4 * t * (D_MODEL + MAIN_COLS + flex_cols + 5 * LANES_V7X + SUBLANES_V7X)
    res = 2 * D_MODEL * (MAIN_COLS + flex_cols + LANES_V7X + 2 * SUBLANES_V7X)
    tmp = 4 * t * (GATE_W + D_MODEL)
    kern = functools.partial(_inproj_kernel, tokens_on_lanes=tokens_on_lanes)
    return pl.pallas_call(
        kern,
        grid=grid,
        in_specs=[row(D_MODEL), _resident((1, D_MODEL)), _resident((D_MODEL, MAIN_COLS))]
        + [_resident(w.shape) for w in w_flex]
        + [_resident((D_MODEL, LANES_V7X)), _resident((2 * SUBLANES_V7X, D_MODEL)), tab, tab, qtab, qtab],
        out_specs=out_specs,
        out_shape=out_shapes,
        compiler_params=_params(_vmem_limit(blk, res, tmp), 1),
        name="inproj",
    )(x2d, g, w_main, *w_flex, w_gate, w_gate_t, cos, sin, cos_q, sin_q)


def _mlstm_kernel(q_ref, k_ref, v_ref, gc_ref, gr_ref, brow_ref, bcol_ref, c0_ref, n0_ref, m0_ref,
                  h_ref, c_ref, n_ref, m_ref, *, chunk, real, bt, mm_dtype):
    L = chunk
    ci = pl.program_id(1)

    @pl.when(ci == 0)
    def _():
        c_ref[...] = c0_ref[...]
        n_ref[...] = n0_ref[...]
        m_ref[...] = m0_ref[...]

    def mx(a):
        return a.astype(mm_dtype)

    def split3(a):
        hi = a.astype(BF16).astype(F32)
        r1 = a - hi
        mid = r1.astype(BF16).astype(F32)
        lo = (r1 - mid).astype(BF16).astype(F32)
        return hi, mid, lo

    ti = lax.broadcasted_iota(jnp.int32, (L, L), 0)
    si = lax.broadcasted_iota(jnp.int32, (L, L), 1)
    causal = si <= ti
    tri = causal.astype(F32)
    tri_t = (ti <= si).astype(F32)
    lane_g = lax.broadcasted_iota(jnp.int32, (L, LANES_V7X), 1)
    row_g = lax.broadcasted_iota(jnp.int32, (L, LANES_V7X), 0)
    is_f_col = (lane_g >= M_HEADS) & (lane_g < 2 * M_HEADS)
    row_r = lax.broadcasted_iota(jnp.int32, (SUBLANES_V7X, L), 0)
    col_r = lax.broadcasted_iota(jnp.int32, (SUBLANES_V7X, L), 1)
    is_f_row = row_r >= M_HEADS
    lane_p = lax.broadcasted_iota(jnp.int32, (L, LANES_V7X), 1)
    half_masks = (lane_p < M_QK_DIM, lane_p >= M_QK_DIM)
    rows_c = lax.broadcasted_iota(jnp.int32, (2 * M_QK_DIM, 1), 0)
    lanes_n = lax.broadcasted_iota(jnp.int32, (1, LANES_V7X), 1)
    neg_inf = jnp.float32(-jnp.inf)

    units = [(bi, h) for bi in range(bt) for h in range(M_HEADS)]
    gates = []
    for bi in range(bt):
        rs = slice(bi * L, (bi + 1) * L)
        gcol = gc_ref[rs, :] + brow_ref[...]
        gcol = jnp.where(is_f_col, _log_sigmoid(gcol), gcol)
        fcol = jnp.where(is_f_col, gcol, 0.0)
        grow = gr_ref[:, rs] + bcol_ref[:, :L]
        grow = jnp.where(is_f_row, _log_sigmoid(grow), grow)
        frow = jnp.where(is_f_row, grow, 0.0)
        if real < L:
            gcol = jnp.where(row_g < real, gcol, neg_inf)
            fcol = jnp.where(row_g < real, fcol, 0.0)
            grow = jnp.where(col_r < real, grow, neg_inf)
            frow = jnp.where(col_r < real, frow, 0.0)
        cum_c = sum(jnp.dot(tri, p, preferred_element_type=F32) for p in split3(fcol))
        cum_r = sum(jnp.dot(p, tri_t, preferred_element_type=F32) for p in split3(frow))
        gates.append((gcol, grow, cum_c, cum_r))

    qhs, kps, vhs, s_raw, q_c = {}, {}, {}, {}, {}
    for bi, h in units:
        rs = slice(bi * L, (bi + 1) * L)
        pair, half = divmod(h, 2)
        ps = slice(pair * LANES_V7X, (pair + 1) * LANES_V7X)
        kp = k_ref[rs, ps]
        qh = jnp.where(half_masks[half], q_ref[rs, ps], 0.0)
        qhs[bi, h], kps[bi, h] = qh, kp
        vhs[bi, h] = v_ref[rs, h * M_V_DIM:(h + 1) * M_V_DIM]
        s_raw[bi, h] = lax.dot_general(mx(qh), mx(kp), _NT, preferred_element_type=F32)
        q_c[bi, h] = jnp.dot(mx(qh), mx(c_ref[bi, pair]), preferred_element_type=F32)

    s_w, kws, iscales, dens_, mrows, decays, new_m = {}, {}, {}, {}, {}, {}, {}
    for bi, h in units:
        gcol, grow, cum_c, cum_r = gates[bi]
        pair, half = divmod(h, 2)
        b_col = cum_c[:, M_HEADS + h:M_HEADS + h + 1]
        li_col = gcol[:, h:h + 1]
        b_row = cum_r[M_HEADS + h:M_HEADS + h + 1, :]
        li_row = grow[h:h + 1, :]
        m0 = m_ref[bi, h:h + 1, 0:1]
        npair = n_ref[bi, pair:pair + 1, :]
        dmat = jnp.where(causal, b_col - b_row + li_row, neg_inf)
        inter = b_col + m0
        mrow = jnp.maximum(inter, jnp.max(dmat, axis=-1, keepdims=True))
        iscale = jnp.exp(inter - mrow)
        s = s_raw[bi, h] * jnp.exp(dmat - mrow)
        qn = jnp.sum(qhs[bi, h] * npair, axis=-1, keepdims=True)
        dens_[bi, h] = iscale * qn + jnp.sum(s, axis=-1, keepdims=True)
        m_new = mrow[L - 1:L, :]
        b_last = b_col[L - 1:L, :]
        w_col = jnp.exp(b_last - b_col + li_col - m_new)
        decays[bi, h] = jnp.exp(b_last + m0 - m_new)
        kws[bi, h] = jnp.where(half_masks[half], kps[bi, h] * w_col, 0.0)
        s_w[bi, h], iscales[bi, h], mrows[bi, h], new_m[bi, h] = s, iscale, mrow, m_new

    d_c = {}
    for bi, h in units:
        rs = slice(bi * L, (bi + 1) * L)
        vh = vhs[bi, h]
        num = iscales[bi, h] * q_c[bi, h] + jnp.dot(mx(s_w[bi, h]), mx(vh), preferred_element_type=F32)
        h_ref[rs, h * M_V_DIM:(h + 1) * M_V_DIM] = \
            num / jnp.maximum(jnp.abs(dens_[bi, h]), jnp.exp(-mrows[bi, h]))
        d_c[bi, h] = lax.dot_general(mx(kws[bi, h]), mx(vh), (((0,), (0,)), ((), ())),
                                     preferred_element_type=F32)

    for bi in range(bt):
        for pair in range(M_HEADS // 2):
            h0, h1 = 2 * pair, 2 * pair + 1
            npair = n_ref[bi, pair:pair + 1, :]
            d_n = jnp.sum(kws[bi, h0], axis=0, keepdims=True) + jnp.sum(kws[bi, h1], axis=0, keepdims=True)
            c_ref[bi, pair] = jnp.where(rows_c < M_QK_DIM, decays[bi, h0], decays[bi, h1]) * c_ref[bi, pair] \
                + d_c[bi, h0] + d_c[bi, h1]
            n_ref[bi, pair:pair + 1, :] = \
                jnp.where(lanes_n < M_QK_DIM, decays[bi, h0], decays[bi, h1]) * npair + d_n
        for h in range(M_HEADS):
            m_ref[bi, h:h + 1, :] = jnp.broadcast_to(new_m[bi, h], (1, LANES_V7X))


def _mlstm(q, k, v, gc, gr, brow, bcol, c0, n0, m0, *, layer, nseq, nchunks, chunk, real, bt, mm_dtype):
    rows = bt * chunk
    grid = (nseq // bt, nchunks)
    nb = nseq // bt
    rmap = lambda b, c: (b * nchunks + c, 0)
    smap3 = lambda b, c: (b, 0, 0)
    smap4 = lambda b, c: (b, 0, 0, 0)
    c_spec = pl.BlockSpec((bt, 2, LANES_V7X, LANES_V7X), smap4)
    n_spec = pl.BlockSpec((bt, 2, LANES_V7X), smap3)
    m_spec = pl.BlockSpec((bt, SUBLANES_V7X, LANES_V7X), smap3)
    c_in = pl.BlockSpec((bt, 2, LANES_V7X, LANES_V7X), lambda b, c: (layer * nb + b, 0, 0, 0))
    n_in = pl.BlockSpec((bt, 2, LANES_V7X), lambda b, c: (layer * nb + b, 0, 0))
    n = q.shape[0]
    blk = 4 * (rows * (2 * QK_W + 2 * M_WIDTH + LANES_V7X + SUBLANES_V7X)
               + 2 * bt * (2 * LANES_V7X * LANES_V7X + 2 * LANES_V7X + SUBLANES_V7X * LANES_V7X))
    tmp = 4 * 16 * chunk * max(chunk, LANES_V7X)
    kern = functools.partial(_mlstm_kernel, chunk=chunk, real=real, bt=bt, mm_dtype=mm_dtype)
    return pl.pallas_call(
        kern,
        grid=grid,
        in_specs=[pl.BlockSpec((rows, QK_W), rmap), pl.BlockSpec((rows, QK_W), rmap),
                  pl.BlockSpec((rows, M_WIDTH), rmap), pl.BlockSpec((rows, LANES_V7X), rmap),
                  pl.BlockSpec((SUBLANES_V7X, rows), lambda b, c: (0, b * nchunks + c)),
                  _resident((1, LANES_V7X)), _resident((SUBLANES_V7X, LANES_V7X)),
                  c_in, n_in, m_spec],
        out_specs=[pl.BlockSpec((rows, M_WIDTH), rmap), c_spec, n_spec, m_spec],
        out_shape=[jax.ShapeDtypeStruct((n, M_WIDTH), F32),
                   jax.ShapeDtypeStruct((nseq, 2, LANES_V7X, LANES_V7X), F32),
                   jax.ShapeDtypeStruct((nseq, 2, LANES_V7X), F32),
                   jax.ShapeDtypeStruct((nseq, SUBLANES_V7X, LANES_V7X), F32)],
        compiler_params=_params(_vmem_limit(blk, 0, tmp), 2),
        name="mlstm",
    )(q, k, v, gc, gr, brow, bcol, c0, n0, m0)


def _mlstm_t_kernel(qt_ref, k_ref, vt_ref, gc_ref, gr_ref, brow_ref, bcol_ref, c0_ref, n0_ref, m0_ref,
                    h_ref, c_ref, n_ref, m_ref):
    L = M_CHUNK
    ci = pl.program_id(1)

    @pl.when(ci == 0)
    def _():
        c_ref[...] = c0_ref[...]
        n_ref[...] = n0_ref[...]
        m_ref[...] = m0_ref[...]

    def split3(a):
        hi = a.astype(BF16).astype(F32)
        r1 = a - hi
        mid = r1.astype(BF16).astype(F32)
        lo = (r1 - mid).astype(BF16).astype(F32)
        return hi, mid, lo

    si = lax.broadcasted_iota(jnp.int32, (L, L), 0)
    ti = lax.broadcasted_iota(jnp.int32, (L, L), 1)
    causal = si <= ti
    tri = (ti <= si).astype(F32)
    tri_t = causal.astype(F32)
    lane_g = lax.broadcasted_iota(jnp.int32, (L, LANES_V7X), 1)
    is_f_col = (lane_g >= M_HEADS) & (lane_g < 2 * M_HEADS)
    row_r = lax.broadcasted_iota(jnp.int32, (SUBLANES_V7X, L), 0)
    is_f_row = row_r >= M_HEADS
    half_masks = (lane_g < M_QK_DIM, lane_g >= M_QK_DIM)
    lanes_n = lax.broadcasted_iota(jnp.int32, (1, LANES_V7X), 1)
    neg_inf = jnp.float32(-jnp.inf)
    PADR = 2 * SUBLANES_V7X

    gcol = gc_ref[...] + brow_ref[...]
    gcol = jnp.where(is_f_col, _log_sigmoid(gcol), gcol)
    grow = gr_ref[...] + bcol_ref[...]
    grow = jnp.where(is_f_row, _log_sigmoid(grow), grow)
    cum_c = sum(jnp.dot(tri, p, preferred_element_type=F32)
                for p in split3(jnp.where(is_f_col, gcol, 0.0)))
    cum_r = sum(jnp.dot(p, tri_t, preferred_element_type=F32)
                for p in split3(jnp.where(is_f_row, grow, 0.0)))

    zq = jnp.zeros((M_QK_DIM, L), BF16)
    zpad = jnp.zeros((PADR - 1, LANES_V7X), F32)
    kps, s_raw, inter = [], [], []
    for h in range(M_HEADS):
        pair, half = divmod(h, 2)
        kp = k_ref[:, pair * LANES_V7X:(pair + 1) * LANES_V7X]
        qh = qt_ref[h * M_QK_DIM:(h + 1) * M_QK_DIM, :].astype(BF16)
        qz = jnp.concatenate([qh, zq] if half == 0 else [zq, qh], axis=0)
        caug = jnp.concatenate([c_ref[0, pair], n_ref[0, pair:pair + 1, :], zpad], axis=0)
        kps.append(kp)
        s_raw.append(jnp.dot(kp.astype(BF16), qz, preferred_element_type=F32))
        inter.append(jnp.dot(caug.astype(BF16), qz, preferred_element_type=F32))
    s_w, iscales, dens_, m_rows, w_rows, decays = [], [], [], [], [], []
    for h in range(M_HEADS):
        b_row = cum_r[M_HEADS + h:M_HEADS + h + 1, :]
        li_row = grow[h:h + 1, :]
        d_col = gcol[:, h:h + 1] - cum_c[:, M_HEADS + h:M_HEADS + h + 1]
        m0 = m_ref[0, h:h + 1, 0:1]
        dmat = jnp.where(causal, b_row + d_col, neg_inf)
        inter_m = b_row + m0
        m_row = jnp.maximum(inter_m, jnp.max(dmat, axis=0, keepdims=True))
        s = s_raw[h] * jnp.exp(dmat - m_row)
        iscale = jnp.exp(inter_m - m_row)
        dens_.append(iscale * inter[h][M_V_DIM:M_V_DIM + 1, :] + jnp.sum(s, axis=0, keepdims=True))
        m_new = m_row[:, L - 1:L]
        b_last = b_row[:, L - 1:L]
        w_rows.append(jnp.exp(b_last - b_row + li_row - m_new))
        decays.append(jnp.exp(b_last + m0 - m_new))
        s_w.append(s.astype(BF16))
        iscales.append(iscale)
        m_rows.append(m_row)
    outs, upds = [], []
    for h in range(M_HEADS):
        pair, half = divmod(h, 2)
        vt = vt_ref[h * M_V_DIM:(h + 1) * M_V_DIM, :]
        num = iscales[h] * inter[h][:M_V_DIM, :] + jnp.dot(vt.astype(BF16), s_w[h], preferred_element_type=F32)
        outs.append(num / jnp.maximum(jnp.abs(dens_[h]), jnp.exp(-m_rows[h])))
        vaug = jnp.concatenate([vt * w_rows[h], w_rows[h], jnp.zeros((PADR - 1, L), F32)], axis=0)
        kmask = jnp.where(half_masks[half], kps[h], 0.0)
        upds.append(jnp.dot(vaug.astype(BF16), kmask.astype(BF16), preferred_element_type=F32))
    for pair in range(M_HEADS // 2):
        dl = jnp.where(lanes_n < M_QK_DIM, decays[2 * pair], decays[2 * pair + 1])
        u0, u1 = upds[2 * pair], upds[2 * pair + 1]
        n_old = n_ref[0, pair:pair + 1, :]
        c_ref[0, pair] = dl * c_ref[0, pair] + u0[:M_V_DIM] + u1[:M_V_DIM]
        n_ref[0, pair:pair + 1, :] = dl * n_old + u0[M_V_DIM:M_V_DIM + 1] + u1[M_V_DIM:M_V_DIM + 1]
    for h in range(M_HEADS):
        m_ref[0, h:h + 1, :] = jnp.broadcast_to(m_rows[h][:, L - 1:L], (1, LANES_V7X))
    h_ref[...] = jnp.concatenate(outs, axis=0).T


def _mlstm_t(qt, k, vt, gc, gr, brow, bcol, c0, n0, m0, *, nseq, nchunks):
    L = M_CHUNK
    n = k.shape[0]
    rmap = lambda b, c: (b * nchunks + c, 0)
    cmap = lambda b, c: (0, b * nchunks + c)
    smap3 = lambda b, c: (b, 0, 0)
    smap4 = lambda b, c: (b, 0, 0, 0)
    c_spec = pl.BlockSpec((1, 2, LANES_V7X, LANES_V7X), smap4)
    n_spec = pl.BlockSpec((1, 2, LANES_V7X), smap3)
    m_spec = pl.BlockSpec((1, SUBLANES_V7X, LANES_V7X), smap3)
    blk = 4 * (L * (2 * QK_W + 2 * M_WIDTH + LANES_V7X + SUBLANES_V7X)
               + 2 * (2 * LANES_V7X * LANES_V7X + 2 * LANES_V7X + SUBLANES_V7X * LANES_V7X))
    tmp = 4 * 48 * L * L
    return pl.pallas_call(
        _mlstm_t_kernel,
        grid=(nseq, nchunks),
        in_specs=[pl.BlockSpec((QK_W, L), cmap), pl.BlockSpec((L, QK_W), rmap),
                  pl.BlockSpec((M_WIDTH, L), cmap), pl.BlockSpec((L, LANES_V7X), rmap),
                  pl.BlockSpec((SUBLANES_V7X, L), cmap),
                  _resident((1, LANES_V7X)), _resident((SUBLANES_V7X, LANES_V7X)),
                  c_spec, n_spec, m_spec],
        out_specs=[pl.BlockSpec((L, M_WIDTH), rmap), c_spec, n_spec, m_spec],
        out_shape=[jax.ShapeDtypeStruct((n, M_WIDTH), F32),
                   jax.ShapeDtypeStruct((nseq, 2, LANES_V7X, LANES_V7X), F32),
                   jax.ShapeDtypeStruct((nseq, 2, LANES_V7X), F32),
                   jax.ShapeDtypeStruct((nseq, SUBLANES_V7X, LANES_V7X), F32)],
        compiler_params=_params(_vmem_limit(blk, 0, tmp), 2),
        name="mlstm_t",
    )(qt, k, vt, gc, gr, brow, bcol, c0, n0, m0)


def _swa_prompt_kernel(qt_ref, kc_ref, kp_ref, vtc_ref, vtp_ref, sink_ref, o_ref, vlast_ref):
    blk = pl.program_id(1)
    L = WINDOW

    @pl.when(blk == pl.num_programs(1) - 1)
    def _():
        vlast_ref[0] = vtc_ref[...].T

    key = lax.broadcasted_iota(jnp.int32, (2 * L, L), 0)
    qry = lax.broadcasted_iota(jnp.int32, (2 * L, L), 1)
    prev_off = jnp.where(blk > 0, 0, L)
    mask = ((key < L) & (key >= qry + prev_off)) | ((key >= L) & (key - L <= qry))
    k2 = jnp.concatenate([kp_ref[...], kc_ref[...]], axis=0).astype(BF16)
    vt2 = jnp.concatenate([vtp_ref[...], vtc_ref[...]], axis=1).astype(BF16)
    zeros = jnp.zeros((A_HEAD_DIM, L), BF16)
    scores = []
    for hd in range(A_HEADS):
        qh = qt_ref[hd * A_HEAD_DIM:(hd + 1) * A_HEAD_DIM, :].astype(BF16)
        qz = jnp.concatenate([qh, zeros] if hd < A_GROUP else [zeros, qh], axis=0)
        scores.append(jnp.dot(k2, qz, preferred_element_type=F32))
    probs = []
    dens = []
    for hd in range(A_HEADS):
        s = jnp.where(mask, scores[hd], -jnp.inf)
        sink = sink_ref[hd:hd + 1, 0:1]
        m = jnp.maximum(jnp.max(s, axis=0, keepdims=True), sink)
        p = jnp.exp(s - m)
        dens.append(jnp.sum(p, axis=0, keepdims=True) + jnp.exp(sink - m))
        probs.append(p.astype(BF16))
    outs = []
    for hd in range(A_HEADS):
        kv = hd // A_GROUP
        o = jnp.dot(vt2[kv * A_HEAD_DIM:(kv + 1) * A_HEAD_DIM, :], probs[hd],
                    preferred_element_type=F32)
        outs.append(o / dens[hd])
    o_ref[...] = jnp.concatenate(outs, axis=0).T


def _swa_prompt(qt, k, vt, sinks, *, nseq, nblk):
    n = k.shape[0]
    L = WINDOW
    cur = lambda b, i: (b * nblk + i, 0)
    prev = lambda b, i: (b * nblk + jnp.maximum(i - 1, 0), 0)
    cur_t = lambda b, i: (0, b * nblk + i)
    prev_t = lambda b, i: (0, b * nblk + jnp.maximum(i - 1, 0))
    blk = 4 * L * (2 * A_WIDTH + 4 * KV_W)
    tmp = 4 * 24 * 2 * L * L
    return pl.pallas_call(
        _swa_prompt_kernel,
        grid=(nseq, nblk),
        in_specs=[pl.BlockSpec((A_WIDTH, L), cur_t),
                  pl.BlockSpec((L, KV_W), cur), pl.BlockSpec((L, KV_W), prev),
                  pl.BlockSpec((KV_W, L), cur_t), pl.BlockSpec((KV_W, L), prev_t),
                  _resident((A_HEADS, LANES_V7X))],
        out_specs=[pl.BlockSpec((L, A_WIDTH), cur), pl.BlockSpec((1, L, KV_W), lambda b, i: (b, 0, 0))],
        out_shape=[jax.ShapeDtypeStruct((n, A_WIDTH), F32), jax.ShapeDtypeStruct((nseq, L, KV_W), F32)],
        compiler_params=_params(_vmem_limit(blk, 0, tmp), 2),
        name="swa_prompt",
    )(qt, k, k, vt, vt, sinks)


def _swa_sample_kernel(q_ref, kn_ref, vn_ref, kc_ref, vc_ref, sink_ref, o_ref, ko_ref, vo_ref,
                       *, bt, real):
    P = SAMPLE_PAD_T
    W = WINDOW
    R = A_HEADS * P
    row = lax.broadcasted_iota(jnp.int32, (R, 2 * W), 0)
    key = lax.broadcasted_iota(jnp.int32, (R, 2 * W), 1)
    step = row % P
    mask = ((key < W) & (key >= step)) | ((key >= W) & (key - W <= step) & (key - W < real))
    sink = sink_ref[:, 0:1]
    pad = jnp.zeros((W - P, KV_W), F32)
    scores = []
    for bi in range(bt):
        kext = jnp.concatenate([kc_ref[bi], kn_ref[bi * P:(bi + 1) * P, :], pad], axis=0)
        scores.append(lax.dot_general(q_ref[bi], kext, _NT, preferred_element_type=F32))
    probs = []
    dens = []
    for bi in range(bt):
        s = jnp.where(mask, scores[bi], -jnp.inf)
        m = jnp.maximum(jnp.max(s, axis=-1, keepdims=True), sink)
        p = jnp.exp(s - m)
        dens.append(jnp.sum(p, axis=-1, keepdims=True) + jnp.exp(sink - m))
        probs.append(p)
    for bi in range(bt):
        rs = slice(bi * P, (bi + 1) * P)
        vn = vn_ref[rs, :]
        vext = jnp.concatenate([vc_ref[bi], vn, pad], axis=0)
        o_ref[bi] = jnp.dot(probs[bi], vext, preferred_element_type=F32) / dens[bi]
        ko_ref[bi, 0:W - real, :] = kc_ref[bi, real:W, :]
        ko_ref[bi, W - real:W, :] = kn_ref[bi * P:bi * P + real, :]
        vo_ref[bi, 0:W - real, :] = vc_ref[bi, real:W, :]
        vo_ref[bi, W - real:W, :] = vn[0:real, :]


def _swa_sample(q_rows, kn, vn, kcache, vcache, sinks, *, layer, nseq, bt, real):
    rows = bt * SAMPLE_PAD_T
    R = A_HEADS * SAMPLE_PAD_T
    nb = nseq // bt
    rmap = lambda i: (i, 0)
    cmap = lambda i: (layer * nb + i, 0, 0)
    omap = lambda i: (i, 0, 0)
    cache = pl.BlockSpec((bt, WINDOW, KV_W), cmap)
    blk = 4 * (2 * bt * R * KV_W + 2 * rows * KV_W + 4 * bt * WINDOW * KV_W)
    tmp = 4 * 16 * R * 2 * WINDOW
    kern = functools.partial(_swa_sample_kernel, bt=bt, real=real)
    return pl.pallas_call(
        kern,
        grid=(nb,),
        in_specs=[pl.BlockSpec((bt, R, KV_W), omap), pl.BlockSpec((rows, KV_W), rmap),
                  pl.BlockSpec((rows, KV_W), rmap), cache, cache, _resident((R, LANES_V7X))],
        out_specs=[pl.BlockSpec((bt, R, KV_W), omap),
                   pl.BlockSpec((bt, WINDOW, KV_W), omap), pl.BlockSpec((bt, WINDOW, KV_W), omap)],
        out_shape=[jax.ShapeDtypeStruct((nseq, R, KV_W), F32),
                   jax.ShapeDtypeStruct((nseq, WINDOW, KV_W), F32),
                   jax.ShapeDtypeStruct((nseq, WINDOW, KV_W), F32)],
        compiler_params=_params(_vmem_limit(blk, 0, tmp), 1),
        name="swa_sample",
    )(q_rows, kn, vn, kcache, vcache, sinks)


def _merge_kernel(hm_ref, mo_ref, ha_ref, gt_ref, x_ref, gmh_ref, wa_ref, wb_ref, wo_ref, o_ref):
    hm = hm_ref[...]
    parts = []
    for h in range(M_HEADS):
        v = hm[:, h * M_V_DIM:(h + 1) * M_V_DIM]
        parts.append(v * lax.rsqrt(jnp.mean(v * v, axis=-1, keepdims=True) + EPS))
    hn = jnp.concatenate(parts, axis=1) * gmh_ref[...] * jax.nn.sigmoid(mo_ref[...])
    a = jnp.dot(hn.astype(BF16), wa_ref[...], preferred_element_type=F32)
    b = jnp.dot(ha_ref[...].astype(BF16), wb_ref[...], preferred_element_type=F32)
    gates = jax.nn.sigmoid(gt_ref[...])
    mixed = gates[:, :D_MODEL] * a + gates[:, D_MODEL:] * b
    o_ref[...] = x_ref[...] + jnp.dot(mixed.astype(BF16), wo_ref[...], preferred_element_type=F32)


def _merge(hm, mo, ha, gt, x2d, gmh, wa, wb, wo):
    n = x2d.shape[0]
    t = min(ROW_TILE, n)
    row = lambda w: pl.BlockSpec((t, w), lambda i: (i, 0))
    blk = 4 * t * (3 * M_WIDTH + GATE_W + 2 * D_MODEL)
    res = 2 * (2 * M_WIDTH * D_MODEL + D_MODEL * D_MODEL)
    tmp = 4 * t * (GATE_W + 3 * D_MODEL)
    return pl.pallas_call(
        _merge_kernel,
        grid=(n // t,),
        in_specs=[row(M_WIDTH), row(M_WIDTH), row(A_WIDTH), row(GATE_W), row(D_MODEL),
                  _resident((1, M_WIDTH)), _resident((M_WIDTH, D_MODEL)),
                  _resident((A_WIDTH, D_MODEL)), _resident((D_MODEL, D_MODEL))],
        out_specs=row(D_MODEL),
        out_shape=jax.ShapeDtypeStruct((n, D_MODEL), F32),
        compiler_params=_params(_vmem_limit(blk, res, tmp), 1),
        name="merge",
    )(hm, mo, ha, gt, x2d, gmh, wa, wb, wo)


def _ffn_kernel(x_ref, gn_ref, halo_ref, wup_ref, wc_ref, wdn_ref, gfin_ref, o_ref, halo_out_ref,
                xn_s, acc_s, halo_s, *, g, final_norm):
    t = x_ref.shape[0]
    rb = min(FFN_ROWS, t)
    hp = halo_s.shape[1]
    ti = pl.program_id(1)
    last = pl.num_programs(1) - 1

    @pl.when(ti == 0)
    def _():
        halo_s[...] = jnp.zeros_like(halo_s)
        for j in range(N_FF_CHUNKS):
            halo_s[j, hp - 2 * g:hp, :FF_CHUNK] = halo_ref[0, :, j * FF_CHUNK:(j + 1) * FF_CHUNK]
            halo_s[j, hp - 2 * g:hp, FF_CHUNK:] = halo_ref[0, :, D_FF + j * FF_CHUNK:D_FF + (j + 1) * FF_CHUNK]

    x = x_ref[...]
    xn_s[...] = (x * lax.rsqrt(jnp.mean(x * x, axis=-1, keepdims=True) + EPS) * gn_ref[...]).astype(BF16)

    for j in range(N_FF_CHUNKS):
        w_up = wup_ref[j]
        us = [jnp.dot(xn_s[r:r + rb, :], w_up, preferred_element_type=F32) for r in range(0, t, rb)]
        wc = wc_ref[j]
        prev = halo_s[j]
        acts = []
        for u in us:
            ext = jnp.concatenate([prev, u], axis=0)
            u1 = ext[hp - g:hp - g + rb]
            u2 = ext[hp - 2 * g:hp - 2 * g + rb]
            prev = ext[rb:rb + hp]
            c = wc[3:4, :] + wc[0:1, :] * u2 + wc[1:2, :] * u1 + wc[2:3, :] * u
            a = c[:, :FF_CHUNK]
            act = 0.5 * a * (1.0 + lax.erf(a * np.float32(np.sqrt(0.5)))) * c[:, FF_CHUNK:]
            acts.append(act.astype(BF16))
        halo_s[j] = prev
        w_dn = wdn_ref[j]
        for i, act in enumerate(acts):
            d = jnp.dot(act, w_dn, preferred_element_type=F32)
            if j == 0:
                acc_s[i * rb:(i + 1) * rb, :] = d
            else:
                acc_s[i * rb:(i + 1) * rb, :] += d

    y = x + acc_s[...]
    if final_norm:
        y = y * lax.rsqrt(jnp.mean(y * y, axis=-1, keepdims=True) + EPS) * gfin_ref[...]
    o_ref[...] = y

    @pl.when(ti == last)
    def _():
        for j in range(N_FF_CHUNKS):
            halo_out_ref[0, :, j * FF_CHUNK:(j + 1) * FF_CHUNK] = halo_s[j, hp - 2 * g:hp, :FF_CHUNK]
            halo_out_ref[0, :, D_FF + j * FF_CHUNK:D_FF + (j + 1) * FF_CHUNK] = halo_s[j, hp - 2 * g:hp, FF_CHUNK:]


def _ffn(x2d, gn, halo, wup, wc, wdn, gfin, *, nseq, t, g, final_norm):
    n = x2d.shape[0]
    tiles = n // (nseq * t)
    hp = -(-2 * g // SUBLANES_V7X) * SUBLANES_V7X
    rmap = lambda s, i: (s * tiles + i, 0)
    hmap = lambda s, i: (s, 0, 0)
    blk = 4 * (2 * t * D_MODEL + 2 * 2 * g * 2 * D_FF)
    res = 2 * (D_MODEL * 2 * D_FF + D_FF * D_MODEL) + 4 * N_FF_CHUNKS * SUBLANES_V7X * 2 * FF_CHUNK
    scr = 2 * t * D_MODEL + 4 * t * D_MODEL + 4 * N_FF_CHUNKS * hp * 2 * FF_CHUNK
    tmp = 4 * t * (8 * FF_CHUNK + D_MODEL)
    kern = functools.partial(_ffn_kernel, g=g, final_norm=final_norm)
    return pl.pallas_call(
        kern,
        grid=(nseq, tiles),
        in_specs=[pl.BlockSpec((t, D_MODEL), rmap), _resident((1, D_MODEL)),
                  pl.BlockSpec((1, 2 * g, 2 * D_FF), hmap),
                  _resident((N_FF_CHUNKS, D_MODEL, 2 * FF_CHUNK)),
                  _resident((N_FF_CHUNKS, SUBLANES_V7X, 2 * FF_CHUNK)),
                  _resident((N_FF_CHUNKS, FF_CHUNK, D_MODEL)), _resident((1, D_MODEL))],
        out_specs=[pl.BlockSpec((t, D_MODEL), rmap), pl.BlockSpec((1, 2 * g, 2 * D_FF), hmap)],
        out_shape=[jax.ShapeDtypeStruct((n, D_MODEL), F32),
                   jax.ShapeDtypeStruct((nseq, 2 * g, 2 * D_FF), F32)],
        scratch_shapes=[pltpu.VMEM((t, D_MODEL), BF16), pltpu.VMEM((t, D_MODEL), F32),
                        pltpu.VMEM((N_FF_CHUNKS, hp, 2 * FF_CHUNK), F32)],
        compiler_params=_params(_vmem_limit(blk, res + scr, tmp), 2),
        name="convffn",
    )(x2d, gn, halo, wup, wc, wdn, gfin)


def _prep_layer(l, g_mix, w_in, b_igate, b_fgate, g_mhead, a_sinks, w_a, w_b, w_out, g_ffn, w_up,
                w_conv, b_conv, w_down):
    w = w_in[l]
    seg = lambda i: w[:, _OFF[i]:_OFF[i + 1]]
    w_main = jnp.concatenate([seg(1), seg(3), seg(7), seg(9)], axis=1).astype(BF16)
    w_flex = tuple(seg(i).astype(BF16) for i in (0, 2, 6, 8))
    w_gate8 = jnp.concatenate([seg(4), seg(5)], axis=1)
    w_gate = jnp.pad(w_gate8, ((0, 0), (0, LANES_V7X - 2 * M_HEADS))).astype(BF16)
    w_gate_t = jnp.pad(w_gate8.T, ((0, SUBLANES_V7X), (0, 0))).astype(BF16)
    bias8 = jnp.concatenate([b_igate[l], b_fgate[l]]).astype(F32)
    brow = jnp.pad(bias8, (0, LANES_V7X - 2 * M_HEADS)).reshape(1, LANES_V7X)
    bcol = jnp.broadcast_to(bias8[:, None], (SUBLANES_V7X, LANES_V7X))
    sink8 = a_sinks[l].reshape(A_HEADS, 1).astype(F32)
    sinks = jnp.broadcast_to(sink8, (A_HEADS, LANES_V7X))
    sink_rows = jnp.broadcast_to(jnp.repeat(sink8, SAMPLE_PAD_T, axis=0), (A_HEADS * SAMPLE_PAD_T, LANES_V7X))
    up = w_up[l]
    wup = jnp.stack([jnp.concatenate([up[:, j * FF_CHUNK:(j + 1) * FF_CHUNK],
                                      up[:, D_FF + j * FF_CHUNK:D_FF + (j + 1) * FF_CHUNK]], axis=1)
                     for j in range(N_FF_CHUNKS)]).astype(BF16)
    cw = jnp.concatenate([w_conv[l], b_conv[l][None, :]], axis=0)
    cw = jnp.pad(cw, ((0, SUBLANES_V7X - CONV_W - 1), (0, 0))).astype(F32)
    wc = jnp.stack([jnp.concatenate([cw[:, j * FF_CHUNK:(j + 1) * FF_CHUNK],
                                     cw[:, D_FF + j * FF_CHUNK:D_FF + (j + 1) * FF_CHUNK]], axis=1)
                    for j in range(N_FF_CHUNKS)])
    wdn = w_down[l].reshape(N_FF_CHUNKS, FF_CHUNK, D_MODEL).astype(BF16)
    return dict(g_mix=g_mix[l].reshape(1, D_MODEL), w_main=w_main, w_flex=w_flex,
                w_flex_t=tuple(w.T for w in w_flex), w_gate=w_gate, w_gate_t=w_gate_t,
                brow=brow, bcol=bcol, gmh=g_mhead[l].reshape(1, M_WIDTH), sinks=sinks, sink_rows=sink_rows,
                wa=w_a[l].astype(BF16), wb=w_b[l].astype(BF16), wo=w_out[l].astype(BF16),
                g_ffn=g_ffn[l].reshape(1, D_MODEL), wup=wup, wc=wc, wdn=wdn)


def _rope_tables(pos):
    half = A_HEAD_DIM // 2
    inv = ROPE_THETA ** (-jnp.arange(0, A_HEAD_DIM, 2, dtype=F32) / A_HEAD_DIM)
    ang = pos.astype(F32)[:, None] * inv[None, :]
    ang = jnp.tile(jnp.concatenate([ang, ang], axis=-1), (1, LANES_V7X // A_HEAD_DIM))
    sign = jnp.where((jnp.arange(LANES_V7X) % A_HEAD_DIM) < half, -1.0, 1.0).astype(F32)
    return jnp.cos(ang), jnp.sin(ang) * sign[None, :]


def _rope_tables_t(pos):
    inv = ROPE_THETA ** (-jnp.arange(0, A_HEAD_DIM, 2, dtype=F32) / A_HEAD_DIM)
    ang = inv[:, None] * pos.astype(F32)[None, :]
    return jnp.cos(ang), jnp.sin(ang)


def _prompt_trunk(x, layers, g_final):
    B, S, _ = x.shape
    n = B * S
    x2d = x.reshape(n, D_MODEL)
    cos, sin = _rope_tables(jnp.arange(S, dtype=jnp.int32))
    cos_t, sin_t = _rope_tables_t(jnp.arange(S, dtype=jnp.int32))
    nchunks = S // M_CHUNK
    c0 = jnp.zeros((B, 2, LANES_V7X, LANES_V7X), F32)
    n0 = jnp.zeros((B, 2, LANES_V7X), F32)
    m0 = jnp.zeros((B, SUBLANES_V7X, LANES_V7X), F32)
    halo0 = jnp.zeros((B, CONV_W - 1, 2 * D_FF), F32)
    outs = dict(C=[], n=[], m=[], k=[], v=[], conv=[])
    for l, p in enumerate(layers):
        mk, mo, ak, gt, mq_t, mv_t, aq_t, av_t, gc, gr = _inproj(
            x2d, p["g_mix"], p["w_main"], p["w_flex_t"], p["w_gate"], p["w_gate_t"],
            cos, sin, cos_t, sin_t, S // ROW_TILE, True)
        hm, c_new, n_new, m_new = _mlstm_t(mq_t, mk, mv_t, gc, gr, p["brow"], p["bcol"], c0, n0, m0,
                                           nseq=B, nchunks=nchunks)
        ha, v_last = _swa_prompt(aq_t, ak, av_t, p["sinks"], nseq=B, nblk=S // WINDOW)
        x2d = _merge(hm, mo, ha, gt, x2d, p["gmh"], p["wa"], p["wb"], p["wo"])
        x2d, conv = _ffn(x2d, p["g_ffn"], halo0, p["wup"], p["wc"], p["wdn"], g_final,
                         nseq=B, t=FFN_TILE, g=1, final_norm=(l == DEPTH - 1))
        c_new = jnp.transpose(c_new.reshape(B, M_HEADS // 2, M_V_DIM, 2, M_QK_DIM), (0, 1, 3, 4, 2))
        outs["C"].append(c_new.reshape(B, M_HEADS, M_QK_DIM, M_V_DIM))
        outs["n"].append(n_new.reshape(B, M_HEADS, M_QK_DIM))
        outs["m"].append(m_new[:, :M_HEADS, 0])
        outs["k"].append(ak.reshape(B, S, A_KV_HEADS, A_HEAD_DIM)[:, S - WINDOW:])
        outs["v"].append(v_last.reshape(B, WINDOW, A_KV_HEADS, A_HEAD_DIM))
        outs["conv"].append(conv)
    return (x2d.reshape(B, S, D_MODEL),) + tuple(jnp.stack(outs[k]) for k in ("C", "n", "m", "k", "v", "conv"))


def _sample_trunk(x, st_c, st_n, st_m, st_k, st_v, st_conv, layers, g_final):
    B, T, _ = x.shape
    P = SAMPLE_PAD_T
    bt = LANES_V7X // P
    gb = SAMPLE_FFN_GROUP
    ng = B // gb
    pos =PAST_LEN + (jnp.arange(B * P, dtype=jnp.int32) % P)
    cos, sin = _rope_tables(pos)
    c_all = st_c.reshape(DEPTH * B, 2, LANES_V7X, LANES_V7X)
    n_all = st_n.reshape(DEPTH * B, 2, LANES_V7X)
    k_all = st_k.reshape(DEPTH * B, WINDOW, KV_W)
    v_all = st_v.reshape(DEPTH * B, WINDOW, KV_W)
    xpad = jnp.pad(x, ((0, 0), (0, P - T), (0, 0))).reshape(B * P, D_MODEL)
    outs = dict(C=[], n=[], m=[], k=[], v=[], conv=[])
    y_bm = None
    for l, p in enumerate(layers):
        mk, mo, ak, gt, mq, mv, aq, av, gc, gr = _inproj(
            xpad, p["g_mix"], p["w_main"], p["w_flex"], p["w_gate"], p["w_gate_t"],
            cos, sin, cos, sin, (B * P) // ROW_TILE, False)
        m0 = jnp.broadcast_to(jnp.pad(st_m[l], ((0, 0), (0, SUBLANES_V7X - M_HEADS)))[:, :, None],
                              (B, SUBLANES_V7X, LANES_V7X))
        hm, c_new, n_new, m_new = _mlstm(mq, mk, mv, gc, gr, p["brow"], p["bcol"], c_all, n_all, m0,
                                         layer=l, nseq=B, nchunks=1, chunk=P, real=T, bt=bt, mm_dtype=F32)
        q5 = jnp.transpose(aq.reshape(B, P, A_KV_HEADS, A_GROUP, A_HEAD_DIM), (0, 2, 3, 1, 4))
        q5 = q5.reshape(B, A_KV_HEADS, A_GROUP * P, A_HEAD_DIM)
        zq = jnp.zeros_like(q5[:, 0])
        q_rows = jnp.concatenate([jnp.concatenate([q5[:, 0], zq], axis=-1),
                                  jnp.concatenate([zq, q5[:, 1]], axis=-1)], axis=1)
        o_rows, k_new, v_new = _swa_sample(q_rows, ak, av, k_all, v_all, p["sink_rows"],
                                           layer=l, nseq=B, bt=bt, real=T)
        o5 = jnp.stack([o_rows[:, :A_GROUP * P, :A_HEAD_DIM], o_rows[:, A_GROUP * P:, A_HEAD_DIM:]], axis=1)
        ha = jnp.transpose(o5.reshape(B, A_KV_HEADS, A_GROUP, P, A_HEAD_DIM),
                           (0, 3, 1, 2, 4)).reshape(B * P, A_WIDTH)
        xm = _merge(hm, mo, ha, gt, xpad, p["gmh"], p["wa"], p["wb"], p["wo"])
        x_tm = jnp.transpose(xm.reshape(ng, gb, P, D_MODEL)[:, :, :T], (0, 2, 1, 3)).reshape(T * B, D_MODEL)
        halo = jnp.transpose(st_conv[l].reshape(ng, gb, CONV_W - 1, 2 * D_FF),
                             (0, 2, 1, 3)).reshape(ng, (CONV_W - 1) * gb, 2 * D_FF)
        y_tm, conv = _ffn(x_tm, p["g_ffn"], halo, p["wup"], p["wc"], p["wdn"], g_final,
                          nseq=ng, t=T * gb, g=gb, final_norm=(l == DEPTH - 1))
        y_bm = jnp.transpose(y_tm.reshape(ng, T, gb, D_MODEL), (0, 2, 1, 3)).reshape(B, T, D_MODEL)
        xpad = jnp.pad(y_bm, ((0, 0), (0, P - T), (0, 0))).reshape(B * P, D_MODEL)
        outs["C"].append(c_new.reshape(B, M_HEADS, M_QK_DIM, M_V_DIM))
        outs["n"].append(n_new.reshape(B, M_HEADS, M_QK_DIM))
        outs["m"].append(m_new[:, :M_HEADS, 0])
        outs["k"].append(k_new.reshape(B, WINDOW, A_KV_HEADS, A_HEAD_DIM))
        outs["v"].append(v_new.reshape(B, WINDOW, A_KV_HEADS, A_HEAD_DIM))
        outs["conv"].append(jnp.transpose(conv.reshape(ng, CONV_W - 1, gb, 2 * D_FF),
                                          (0, 2, 1, 3)).reshape(B, CONV_W - 1, 2 * D_FF))
    return (y_bm,) + tuple(jnp.stack(outs[k]) for k in ("C", "n", "m", "k", "v", "conv"))


def kernel(x_prompt, x_sample, state_mlstm_C, state_mlstm_n, state_mlstm_m, cache_swa_k, cache_swa_v,
           state_ffn_conv, g_mix, w_in, b_igate, b_fgate, g_mhead, a_sinks, w_a, w_b, w_out, g_ffn,
           w_up, w_conv, b_conv, w_down, g_final):
    layers = [_prep_layer(l, g_mix, w_in, b_igate, b_fgate, g_mhead, a_sinks, w_a, w_b, w_out, g_ffn,
                          w_up, w_conv, b_conv, w_down) for l in range(DEPTH)]
    gfin = g_final.reshape(1, D_MODEL)
    yp, p_c, p_n, p_m, p_k, p_v, p_conv = _prompt_trunk(x_prompt, layers, gfin)
    ys, s_c, s_n, s_m, s_k, s_v, s_conv = _sample_trunk(
        x_sample, state_mlstm_C, state_mlstm_n, state_mlstm_m, cache_swa_k, cache_swa_v,
        state_ffn_conv, layers, gfin)
    return (yp, ys, p_c, p_n, p_m, p_k, p_v, p_conv, s_c, s_n, s_m, s_k, s_v, s_conv)
```

```python
import functools

import numpy as np
import jax
import jax.numpy as jnp
from jax import lax
from jax.experimental import pallas as pl
from jax.experimental.pallas import tpu as pltpu

F32 = jnp.float32
BF16 = jnp.bfloat16

D_MODEL = 1024
DEPTH = 2
PAST_LEN = 16384
M_HEADS = 4
M_QK_DIM = 64
M_V_DIM = 128
M_WIDTH = M_HEADS * M_V_DIM
M_CHUNK = 128
A_HEADS = 8
A_KV_HEADS = 2
A_GROUP = A_HEADS // A_KV_HEADS
A_HEAD_DIM = 64
A_WIDTH = A_HEADS * A_HEAD_DIM
WINDOW = 128
ROPE_THETA = 10000.0
D_FF = 2816
CONV_W = 3
EPS = 1e-6

QK_W = M_HEADS * M_QK_DIM
KV_W = A_KV_HEADS * A_HEAD_DIM
GATE_W = 2 * D_MODEL
_SPLIT = (QK_W, QK_W, M_WIDTH, M_WIDTH, M_HEADS, M_HEADS, A_WIDTH, KV_W, KV_W, GATE_W)
_OFF = np.concatenate([[0], np.cumsum(_SPLIT)]).tolist()
_MAIN_W = (QK_W, M_WIDTH, KV_W, GATE_W)
_MAIN_OFF = np.concatenate([[0], np.cumsum(_MAIN_W)]).tolist()
MAIN_COLS = _MAIN_OFF[-1]

LANES_V7X = 128
SUBLANES_V7X = 8
VMEM_BYTES_V7X = 64 * 1024 * 1024
VMEM_LIMIT_CAP_V7X = 60000 * 1024

SAMPLE_PAD_T = SUBLANES_V7X
FF_CHUNK = 2 * LANES_V7X
N_FF_CHUNKS = D_FF // FF_CHUNK
ROW_TILE = 512
SAMPLE_FFN_GROUP = 32
SWA_QUERY_BLOCKS = 2
FFN_TILE = 1024
FFN_ROWS = 128
INPROJ_ROWS = 256


def _vmem_limit(block_bytes, scratch_bytes, temp_bytes):
    need = 2 * block_bytes + scratch_bytes + temp_bytes
    assert need <= VMEM_LIMIT_CAP_V7X, need
    return VMEM_LIMIT_CAP_V7X


def _params(vmem_bytes, n_axes):
    return pltpu.CompilerParams(dimension_semantics=("arbitrary",) * n_axes,
                                vmem_limit_bytes=vmem_bytes)


def _resident(shape):
    nd = len(shape)
    return pl.BlockSpec(shape, lambda *_: (0,) * nd, pipeline_mode=pl.Buffered(1))


def _log_sigmoid(x):
    return jnp.minimum(x, 0.0) - jnp.log1p(jnp.exp(-jnp.abs(x)))


_NT = (((1,), (1,)), ((), ()))


def _inproj_kernel(x_ref, g_ref, w_ref, wmq_ref, wmv_ref, wq_ref, wv_ref, wg_ref, wgt_ref, cos_ref,
                   sin_ref, cosq_ref, sinq_ref, mk_ref, mo_ref, ak_ref, gt_ref, mq_ref, mv_ref, aq_ref,
                   av_ref, gc_ref, gr_ref, *, tokens_on_lanes):
    t = x_ref.shape[0]
    rb = min(INPROJ_ROWS, t)
    lane = lax.broadcasted_iota(jnp.int32, (rb, LANES_V7X), 1)
    first_half = (lane % A_HEAD_DIM) < (A_HEAD_DIM // 2)
    half = A_HEAD_DIM // 2
    scale = A_HEAD_DIM ** -0.5

    def rope(v, c, s):
        rot = jnp.where(first_half,
                        pltpu.roll(v, LANES_V7X - A_HEAD_DIM // 2, 1),
                        pltpu.roll(v, A_HEAD_DIM // 2, 1))
        return v * c + rot * s

    for r0 in range(0, t, rb):
        rows = slice(r0, r0 + rb)
        x = x_ref[rows, :]
        xn = (x * lax.rsqrt(jnp.mean(x * x, axis=-1, keepdims=True) + EPS) * g_ref[...]).astype(BF16)

        def proj(idx):
            return jnp.dot(xn, w_ref[:, _MAIN_OFF[idx]:_MAIN_OFF[idx + 1]], preferred_element_type=F32)

        def flex(w_r):
            if tokens_on_lanes:
                return lax.dot_general(w_r[...], xn, _NT, preferred_element_type=F32)
            return jnp.dot(xn, w_r[...], preferred_element_type=F32)

        mk, mo, ak, gt = proj(0), proj(1), proj(2), proj(3)
        mq, mv, aq, av = flex(wmq_ref), flex(wmv_ref), flex(wq_ref), flex(wv_ref)
        gc = jnp.dot(xn, wg_ref[...], preferred_element_type=F32)
        gr = lax.dot_general(wgt_ref[...], xn, _NT, preferred_element_type=F32)

        mk_ref[rows, :] = mk
        mo_ref[rows, :] = mo
        gt_ref[rows, :] = gt
        ak_ref[rows, :] = rope(ak, cos_ref[rows, :], sin_ref[rows, :])
        gc_ref[rows, :] = gc
        gr_ref[:, rows] = gr[:SUBLANES_V7X]
        if tokens_on_lanes:
            cq = cosq_ref[:, rows]
            sq = sinq_ref[:, rows]
            mq_ref[:, rows] = mq * (M_QK_DIM ** -0.5)
            mv_ref[:, rows] = mv
            av_ref[:, rows] = av
            for h in range(A_HEADS):
                x1 = aq[h * A_HEAD_DIM:h * A_HEAD_DIM + half]
                x2 = aq[h * A_HEAD_DIM + half:(h + 1) * A_HEAD_DIM]
                aq_ref[h * A_HEAD_DIM:h * A_HEAD_DIM + half, rows] = (x1 * cq - x2 * sq) * scale
                aq_ref[h * A_HEAD_DIM + half:(h + 1) * A_HEAD_DIM, rows] = (x2 * cq + x1 * sq) * scale
        else:
            cq = cosq_ref[rows, :]
            sq = sinq_ref[rows, :]
            mq_ref[rows, :] = mq * (M_QK_DIM ** -0.5)
            mv_ref[rows, :] = mv
            av_ref[rows, :] = av
            for c in range(A_WIDTH // LANES_V7X):
                sl = slice(c * LANES_V7X, (c + 1) * LANES_V7X)
                aq_ref[rows, sl] = rope(aq[:, sl], cq, sq) * scale


_FLEX_W = (QK_W, M_WIDTH, A_WIDTH, KV_W)


def _inproj(x2d, g, w_main, w_flex, w_gate, w_gate_t, cos, sin, cos_q, sin_q, tiles_per_seq,
            tokens_on_lanes):
    n = x2d.shape[0]
    t = min(ROW_TILE, n)
    grid = (n // t,)
    row = lambda w: pl.BlockSpec((t, w), lambda i: (i, 0))
    col = lambda h: pl.BlockSpec((h, t), lambda i: (0, i))
    tab = pl.BlockSpec((t, LANES_V7X), lambda i: (i % tiles_per_seq, 0))
    if tokens_on_lanes:
        qtab = pl.BlockSpec((A_HEAD_DIM // 2, t), lambda i: (0, i % tiles_per_seq))
        flex_specs = [col(w) for w in _FLEX_W]
        flex_shapes = [(w, n) for w in _FLEX_W]
    else:
        qtab = tab
        flex_specs = [row(w) for w in _FLEX_W]
        flex_shapes = [(n, w) for w in _FLEX_W]
    out_shapes = [jax.ShapeDtypeStruct((n, w), F32) for w in _MAIN_W]
    out_shapes += [jax.ShapeDtypeStruct(s, F32) for s in flex_shapes]
    out_shapes += [jax.ShapeDtypeStruct((n, LANES_V7X), F32), jax.ShapeDtypeStruct((SUBLANES_V7X, n), F32)]
    out_specs = [row(w) for w in _MAIN_W] + flex_specs + [row(LANES_V7X), col(SUBLANES_V7X)]
    flex_cols = sum(_FLEX_W)
    blk = 4 * t * (D_MODEL + MAIN_COLS + flex_cols + 5 * LANES_V7X + SUBLANES_V7X)
    res = 2 * D_MODEL * (MAIN_COLS + flex_cols + LANES_V7X + 2 * SUBLANES_V7X)
    tmp = 4 * t * (GATE_W + D_MODEL)
    kern = functools.partial(_inproj_kernel, tokens_on_lanes=tokens_on_lanes)
    return pl.pallas_call(
        kern,
        grid=grid,
        in_specs=[row(D_MODEL), _resident((1, D_MODEL)), _resident((D_MODEL, MAIN_COLS))]
        + [_resident(w.shape) for w in w_flex]
        + [_resident((D_MODEL, LANES_V7X)), _resident((2 * SUBLANES_V7X, D_MODEL)), tab, tab, qtab, qtab],
        out_specs=out_specs,
        out_shape=out_shapes,
        compiler_params=_params(_vmem_limit(blk, res, tmp), 1),
        name="inproj",
    )(x2d, g, w_main, *w_flex, w_gate, w_gate_t, cos, sin, cos_q, sin_q)


def _mlstm_kernel(q_ref, k_ref, v_ref, gc_ref, gr_ref, brow_ref, bcol_ref, c0_ref, n0_ref, m0_ref,
                  h_ref, c_ref, n_ref, m_ref, *, chunk, real, bt, mm_dtype):
    L = chunk
    ci = pl.program_id(1)

    @pl.when(ci == 0)
    def _():
        c_ref[...] = c0_ref[...]
        n_ref[...] = n0_ref[...]
        m_ref[...] = m0_ref[...]

    def mx(a):
        return a.astype(mm_dtype)

    def split3(a):
        hi = a.astype(BF16).astype(F32)
        r1 = a - hi
        mid = r1.astype(BF16).astype(F32)
        lo = (r1 - mid).astype(BF16).astype(F32)
        return hi, mid, lo

    ti = lax.broadcasted_iota(jnp.int32, (L, L), 0)
    si = lax.broadcasted_iota(jnp.int32, (L, L), 1)
    causal = si <= ti
    tri = causal.astype(F32)
    tri_t = (ti <= si).astype(F32)
    lane_g = lax.broadcasted_iota(jnp.int32, (L, LANES_V7X), 1)
    row_g = lax.broadcasted_iota(jnp.int32, (L, LANES_V7X), 0)
    is_f_col = (lane_g >= M_HEADS) & (lane_g < 2 * M_HEADS)
    row_r = lax.broadcasted_iota(jnp.int32, (SUBLANES_V7X, L), 0)
    col_r = lax.broadcasted_iota(jnp.int32, (SUBLANES_V7X, L), 1)
    is_f_row = row_r >= M_HEADS
    lane_p = lax.broadcasted_iota(jnp.int32, (L, LANES_V7X), 1)
    half_masks = (lane_p < M_QK_DIM, lane_p >= M_QK_DIM)
    rows_c = lax.broadcasted_iota(jnp.int32, (2 * M_QK_DIM, 1), 0)
    lanes_n = lax.broadcasted_iota(jnp.int32, (1, LANES_V7X), 1)
    neg_inf = jnp.float32(-jnp.inf)

    units = [(bi, h) for bi in range(bt) for h in range(M_HEADS)]
    gates = []
    for bi in range(bt):
        rs = slice(bi * L, (bi + 1) * L)
        gcol = gc_ref[rs, :] + brow_ref[...]
        gcol = jnp.where(is_f_col, _log_sigmoid(gcol), gcol)
        fcol = jnp.where(is_f_col, gcol, 0.0)
        grow = gr_ref[:, rs] + bcol_ref[:, :L]
        grow = jnp.where(is_f_row, _log_sigmoid(grow), grow)
        frow = jnp.where(is_f_row, grow, 0.0)
        if real < L:
            gcol = jnp.where(row_g < real, gcol, neg_inf)
            fcol = jnp.where(row_g < real, fcol, 0.0)
            grow = jnp.where(col_r < real, grow, neg_inf)
            frow = jnp.where(col_r < real, frow, 0.0)
        cum_c = sum(jnp.dot(tri, p, preferred_element_type=F32) for p in split3(fcol))
        cum_r = sum(jnp.dot(p, tri_t, preferred_element_type=F32) for p in split3(frow))
        gates.append((gcol, grow, cum_c, cum_r))

    qhs, kps, vhs, s_raw, q_c = {}, {}, {}, {}, {}
    for bi, h in units:
        rs = slice(bi * L, (bi + 1) * L)
        pair, half = divmod(h, 2)
        ps = slice(pair * LANES_V7X, (pair + 1) * LANES_V7X)
        kp = k_ref[rs, ps]
        qh = jnp.where(half_masks[half], q_ref[rs, ps], 0.0)
        qhs[bi, h], kps[bi, h] = qh, kp
        vhs[bi, h] = v_ref[rs, h * M_V_DIM:(h + 1) * M_V_DIM]
        s_raw[bi, h] = lax.dot_general(mx(qh), mx(kp), _NT, preferred_element_type=F32)
        q_c[bi, h] = jnp.dot(mx(qh), mx(c_ref[bi, pair]), preferred_element_type=F32)

    s_w, kws, iscales, dens_, mrows, decays, new_m = {}, {}, {}, {}, {}, {}, {}
    for bi, h in units:
        gcol, grow, cum_c, cum_r = gates[bi]
        pair, half = divmod(h, 2)
        b_col = cum_c[:, M_HEADS + h:M_HEADS + h + 1]
        li_col = gcol[:, h:h + 1]
        b_row = cum_r[M_HEADS + h:M_HEADS + h + 1, :]
        li_row = grow[h:h + 1, :]
        m0 = m_ref[bi, h:h + 1, 0:1]
        npair = n_ref[bi, pair:pair + 1, :]
        dmat = jnp.where(causal, b_col - b_row + li_row, neg_inf)
        inter = b_col + m0
        mrow = jnp.maximum(inter, jnp.max(dmat, axis=-1, keepdims=True))
        iscale = jnp.exp(inter - mrow)
        s = s_raw[bi, h] * jnp.exp(dmat - mrow)
        qn = jnp.sum(qhs[bi, h] * npair, axis=-1, keepdims=True)
        dens_[bi, h] = iscale * qn + jnp.sum(s, axis=-1, keepdims=True)
        m_new = mrow[L - 1:L, :]
        b_last = b_col[L - 1:L, :]
        w_col = jnp.exp(b_last - b_col + li_col - m_new)
        decays[bi, h] = jnp.exp(b_last + m0 - m_new)
        kws[bi, h] = jnp.where(half_masks[half], kps[bi, h] * w_col, 0.0)
        s_w[bi, h], iscales[bi, h], mrows[bi, h], new_m[bi, h] = s, iscale, mrow, m_new

    d_c = {}
    for bi, h in units:
        rs = slice(bi * L, (bi + 1) * L)
        vh = vhs[bi, h]
        num = iscales[bi, h] * q_c[bi, h] + jnp.dot(mx(s_w[bi, h]), mx(vh), preferred_element_type=F32)
        h_ref[rs, h * M_V_DIM:(h + 1) * M_V_DIM] = \
            num / jnp.maximum(jnp.abs(dens_[bi, h]), jnp.exp(-mrows[bi, h]))
        d_c[bi, h] = lax.dot_general(mx(kws[bi, h]), mx(vh), (((0,), (0,)), ((), ())),
                                     preferred_element_type=F32)

    for bi in range(bt):
        for pair in range(M_HEADS // 2):
            h0, h1 = 2 * pair, 2 * pair + 1
            npair = n_ref[bi, pair:pair + 1, :]
            d_n = jnp.sum(kws[bi, h0], axis=0, keepdims=True) + jnp.sum(kws[bi, h1], axis=0, keepdims=True)
            c_ref[bi, pair] = jnp.where(rows_c < M_QK_DIM, decays[bi, h0], decays[bi, h1]) * c_ref[bi, pair] \
                + d_c[bi, h0] + d_c[bi, h1]
            n_ref[bi, pair:pair + 1, :] = \
                jnp.where(lanes_n < M_QK_DIM, decays[bi, h0], decays[bi, h1]) * npair + d_n
        for h in range(M_HEADS):
            m_ref[bi, h:h + 1, :] = jnp.broadcast_to(new_m[bi, h], (1, LANES_V7X))


def _mlstm(q, k, v, gc, gr, brow, bcol, c0, n0, m0, *, layer, nseq, nchunks, chunk, real, bt, mm_dtype):
    rows = bt * chunk
    grid = (nseq // bt, nchunks)
    nb = nseq // bt
    rmap = lambda b, c: (b * nchunks + c, 0)
    smap3 = lambda b, c: (b, 0, 0)
    smap4 = lambda b, c: (b, 0, 0, 0)
    c_spec = pl.BlockSpec((bt, 2, LANES_V7X, LANES_V7X), smap4)
    n_spec = pl.BlockSpec((bt, 2, LANES_V7X), smap3)
    m_spec = pl.BlockSpec((bt, SUBLANES_V7X, LANES_V7X), smap3)
    c_in = pl.BlockSpec((bt, 2, LANES_V7X, LANES_V7X), lambda b, c: (layer * nb + b, 0, 0, 0))
    n_in = pl.BlockSpec((bt, 2, LANES_V7X), lambda b, c: (layer * nb + b, 0, 0))
    n = q.shape[0]
    blk = 4 * (rows * (2 * QK_W + 2 * M_WIDTH + LANES_V7X + SUBLANES_V7X)
               + 2 * bt * (2 * LANES_V7X * LANES_V7X + 2 * LANES_V7X + SUBLANES_V7X * LANES_V7X))
    tmp = 4 * 16 * chunk * max(chunk, LANES_V7X)
    kern = functools.partial(_mlstm_kernel, chunk=chunk, real=real, bt=bt, mm_dtype=mm_dtype)
    return pl.pallas_call(
        kern,
        grid=grid,
        in_specs=[pl.BlockSpec((rows, QK_W), rmap), pl.BlockSpec((rows, QK_W), rmap),
                  pl.BlockSpec((rows, M_WIDTH), rmap), pl.BlockSpec((rows, LANES_V7X), rmap),
                  pl.BlockSpec((SUBLANES_V7X, rows), lambda b, c: (0, b * nchunks + c)),
                  _resident((1, LANES_V7X)), _resident((SUBLANES_V7X, LANES_V7X)),
                  c_in, n_in, m_spec],
        out_specs=[pl.BlockSpec((rows, M_WIDTH), rmap), c_spec, n_spec, m_spec],
        out_shape=[jax.ShapeDtypeStruct((n, M_WIDTH), F32),
                   jax.ShapeDtypeStruct((nseq, 2, LANES_V7X, LANES_V7X), F32),
                   jax.ShapeDtypeStruct((nseq, 2, LANES_V7X), F32),
                   jax.ShapeDtypeStruct((nseq, SUBLANES_V7X, LANES_V7X), F32)],
        compiler_params=_params(_vmem_limit(blk, 0, tmp), 2),
        name="mlstm",
    )(q, k, v, gc, gr, brow, bcol, c0, n0, m0)


def _mlstm_t_kernel(qt_ref, k_ref, vt_ref, gc_ref, gr_ref, brow_ref, bcol_ref, c0_ref, n0_ref, m0_ref,
                    h_ref, c_ref, n_ref, m_ref):
    L = M_CHUNK
    ci = pl.program_id(1)

    @pl.when(ci == 0)
    def _():
        c_ref[...] = c0_ref[...]
        n_ref[...] = n0_ref[...]
        m_ref[...] = m0_ref[...]

    def split3(a):
        hi = a.astype(BF16).astype(F32)
        r1 = a - hi
        mid = r1.astype(BF16).astype(F32)
        lo = (r1 - mid).astype(BF16).astype(F32)
        return hi, mid, lo

    si = lax.broadcasted_iota(jnp.int32, (L, L), 0)
    ti = lax.broadcasted_iota(jnp.int32, (L, L), 1)
    causal = si <= ti
    tri = (ti <= si).astype(F32)
    tri_t = causal.astype(F32)
    lane_g = lax.broadcasted_iota(jnp.int32, (L, LANES_V7X), 1)
    is_f_col = (lane_g >= M_HEADS) & (lane_g < 2 * M_HEADS)
    row_r = lax.broadcasted_iota(jnp.int32, (SUBLANES_V7X, L), 0)
    is_f_row = row_r >= M_HEADS
    half_masks = (lane_g < M_QK_DIM, lane_g >= M_QK_DIM)
    lanes_n = lax.broadcasted_iota(jnp.int32, (1, LANES_V7X), 1)
    neg_inf = jnp.float32(-jnp.inf)
    PADR = 2 * SUBLANES_V7X

    gcol = gc_ref[...] + brow_ref[...]
    gcol = jnp.where(is_f_col, _log_sigmoid(gcol), gcol)
    grow = gr_ref[...] + bcol_ref[...]
    grow = jnp.where(is_f_row, _log_sigmoid(grow), grow)
    cum_c = sum(jnp.dot(tri, p, preferred_element_type=F32)
                for p in split3(jnp.where(is_f_col, gcol, 0.0)))
    cum_r = sum(jnp.dot(p, tri_t, preferred_element_type=F32)
                for p in split3(jnp.where(is_f_row, grow, 0.0)))

    zq = jnp.zeros((M_QK_DIM, L), BF16)
    zpad = jnp.zeros((PADR - 1, LANES_V7X), F32)
    kps, s_raw, inter = [], [], []
    for h in range(M_HEADS):
        pair, half = divmod(h, 2)
        kp = k_ref[:, pair * LANES_V7X:(pair + 1) * LANES_V7X]
        qh = qt_ref[h * M_QK_DIM:(h + 1) * M_QK_DIM, :].astype(BF16)
        qz = jnp.concatenate([qh, zq] if half == 0 else [zq, qh], axis=0)
        caug = jnp.concatenate([c_ref[0, pair], n_ref[0, pair:pair + 1, :], zpad], axis=0)
        kps.append(kp)
        s_raw.append(jnp.dot(kp.astype(BF16), qz, preferred_element_type=F32))
        inter.append(jnp.dot(caug.astype(BF16), qz, preferred_element_type=F32))
    s_w, iscales, dens_, m_rows, w_rows, decays = [], [], [], [], [], []
    for h in range(M_HEADS):
        b_row = cum_r[M_HEADS + h:M_HEADS + h + 1, :]
        li_row = grow[h:h + 1, :]
        d_col = gcol[:, h:h + 1] - cum_c[:, M_HEADS + h:M_HEADS + h + 1]
        m0 = m_ref[0, h:h + 1, 0:1]
        dmat = jnp.where(causal, b_row + d_col, neg_inf)
        inter_m = b_row + m0
        m_row = jnp.maximum(inter_m, jnp.max(dmat, axis=0, keepdims=True))
        s = s_raw[h] * jnp.exp(dmat - m_row)
        iscale = jnp.exp(inter_m - m_row)
        dens_.append(iscale * inter[h][M_V_DIM:M_V_DIM + 1, :] + jnp.sum(s, axis=0, keepdims=True))
        m_new = m_row[:, L - 1:L]
        b_last = b_row[:, L - 1:L]
        w_rows.append(jnp.exp(b_last - b_row + li_row - m_new))
        decays.append(jnp.exp(b_last + m0 - m_new))
        s_w.append(s.astype(BF16))
        iscales.append(iscale)
        m_rows.append(m_row)
    outs, upds = [], []
    for h in range(M_HEADS):
        pair, half = divmod(h, 2)
        vt = vt_ref[h * M_V_DIM:(h + 1) * M_V_DIM, :]
        num = iscales[h] * inter[h][:M_V_DIM, :] + jnp.dot(vt.astype(BF16), s_w[h], preferred_element_type=F32)
        outs.append(num / jnp.maximum(jnp.abs(dens_[h]), jnp.exp(-m_rows[h])))
        vaug = jnp.concatenate([vt * w_rows[h], w_rows[h], jnp.zeros((PADR - 1, L), F32)], axis=0)
        kmask = jnp.where(half_masks[half], kps[h], 0.0)
        upds.append(jnp.dot(vaug.astype(BF16), kmask.astype(BF16), preferred_element_type=F32))
    for pair in range(M_HEADS // 2):
        dl = jnp.where(lanes_n < M_QK_DIM, decays[2 * pair], decays[2 * pair + 1])
        u0, u1 = upds[2 * pair], upds[2 * pair + 1]
        n_old = n_ref[0, pair:pair + 1, :]
        c_ref[0, pair] = dl * c_ref[0, pair] + u0[:M_V_DIM] + u1[:M_V_DIM]
        n_ref[0, pair:pair + 1, :] = dl * n_old + u0[M_V_DIM:M_V_DIM + 1] + u1[M_V_DIM:M_V_DIM + 1]
    for h in range(M_HEADS):
        m_ref[0, h:h + 1, :] = jnp.broadcast_to(m_rows[h][:, L - 1:L], (1, LANES_V7X))
    h_ref[...] = jnp.concatenate(outs, axis=0).T


def _mlstm_t(qt, k, vt, gc, gr, brow, bcol, c0, n0, m0, *, nseq, nchunks):
    L = M_CHUNK
    n = k.shape[0]
    rmap = lambda b, c: (b * nchunks + c, 0)
    cmap = lambda b, c: (0, b * nchunks + c)
    smap3 = lambda b, c: (b, 0, 0)
    smap4 = lambda b, c: (b, 0, 0, 0)
    c_spec = pl.BlockSpec((1, 2, LANES_V7X, LANES_V7X), smap4)
    n_spec = pl.BlockSpec((1, 2, LANES_V7X), smap3)
    m_spec = pl.BlockSpec((1, SUBLANES_V7X, LANES_V7X), smap3)
    blk = 4 * (L * (2 * QK_W + 2 * M_WIDTH + LANES_V7X + SUBLANES_V7X)
               + 2 * (2 * LANES_V7X * LANES_V7X + 2 * LANES_V7X + SUBLANES_V7X * LANES_V7X))
    tmp = 4 * 48 * L * L
    return pl.pallas_call(
        _mlstm_t_kernel,
        grid=(nseq, nchunks),
        in_specs=[pl.BlockSpec((QK_W, L), cmap), pl.BlockSpec((L, QK_W), rmap),
                  pl.BlockSpec((M_WIDTH, L), cmap), pl.BlockSpec((L, LANES_V7X), rmap),
                  pl.BlockSpec((SUBLANES_V7X, L), cmap),
                  _resident((1, LANES_V7X)), _resident((SUBLANES_V7X, LANES_V7X)),
                  c_spec, n_spec, m_spec],
        out_specs=[pl.BlockSpec((L, M_WIDTH), rmap), c_spec, n_spec, m_spec],
        out_shape=[jax.ShapeDtypeStruct((n, M_WIDTH), F32),
                   jax.ShapeDtypeStruct((nseq, 2, LANES_V7X, LANES_V7X), F32),
                   jax.ShapeDtypeStruct((nseq, 2, LANES_V7X), F32),
                   jax.ShapeDtypeStruct((nseq, SUBLANES_V7X, LANES_V7X), F32)],
        compiler_params=_params(_vmem_limit(blk, 0, tmp), 2),
        name="mlstm_t",
    )(qt, k, vt, gc, gr, brow, bcol, c0, n0, m0)


def _swa_prompt_kernel(qt_ref, kc_ref, kp_ref, vtc_ref, vtp_ref, sink_ref, o_ref, vlast_ref):
    blk = pl.program_id(1)
    L = WINDOW
    nq = kc_ref.shape[0] // L

    @pl.when(blk == pl.num_programs(1) - 1)
    def _():
        vlast_ref[0] = vtc_ref[:, (nq - 1) * L:nq * L].T

    key = lax.broadcasted_iota(jnp.int32, (2 * L, L), 0)
    qry = lax.broadcasted_iota(jnp.int32, (2 * L, L), 1)
    prev_off = jnp.where(blk > 0, 0, L)
    band = (key >= L) & (key - L <= qry)
    masks = [((key < L) & (key >= qry + prev_off)) | band] + [((key < L) & (key >= qry)) | band] * (nq - 1)
    k_all = jnp.concatenate([kp_ref[...], kc_ref[...]], axis=0).astype(BF16)
    vt_all = jnp.concatenate([vtp_ref[...], vtc_ref[...]], axis=1).astype(BF16)
    zeros = jnp.zeros((A_HEAD_DIM, L), BF16)
    units = [(qb, hd) for qb in range(nq) for hd in range(A_HEADS)]
    scores = {}
    for qb, hd in units:
        qh = qt_ref[hd * A_HEAD_DIM:(hd + 1) * A_HEAD_DIM, qb * L:(qb + 1) * L].astype(BF16)
        qz = jnp.concatenate([qh, zeros] if hd < A_GROUP else [zeros, qh], axis=0)
        scores[qb, hd] = jnp.dot(k_all[qb * L:(qb + 2) * L], qz, preferred_element_type=F32)
    probs = {}
    dens = {}
    for qb, hd in units:
        s = jnp.where(masks[qb], scores[qb, hd], -jnp.inf)
        sink = sink_ref[hd:hd + 1, 0:1]
        m = jnp.maximum(jnp.max(s, axis=0, keepdims=True), sink)
        p = jnp.exp(s - m)
        dens[qb, hd] = jnp.sum(p, axis=0, keepdims=True) + jnp.exp(sink - m)
        probs[qb, hd] = p.astype(BF16)
    for qb in range(nq):
        outs = []
        for hd in range(A_HEADS):
            kv = hd // A_GROUP
            o = jnp.dot(vt_all[kv * A_HEAD_DIM:(kv + 1) * A_HEAD_DIM, qb * L:(qb + 2) * L], probs[qb, hd],
                        preferred_element_type=F32)
            outs.append(o / dens[qb, hd])
        o_ref[qb * L:(qb + 1) * L, :] = jnp.concatenate(outs, axis=0).T


def _swa_prompt(qt, k, vt, sinks, *, nseq, nblk):
    n = k.shape[0]
    L = WINDOW
    nq = SWA_QUERY_BLOCKS
    steps = nblk // nq
    cur = lambda b, i: (b * steps + i, 0)
    prev = lambda b, i: (b * nblk + jnp.maximum(nq * i - 1, 0), 0)
    cur_t = lambda b, i: (0, b * steps + i)
    prev_t = lambda b, i: (0, b * nblk + jnp.maximum(nq * i - 1, 0))
    blk = 4 * L * (2 * nq * A_WIDTH + 2 * (nq + 1) * KV_W)
    tmp = 4 * 24 * 2 * L * L * nq
    return pl.pallas_call(
        _swa_prompt_kernel,
        grid=(nseq, steps),
        in_specs=[pl.BlockSpec((A_WIDTH, nq * L), cur_t),
                  pl.BlockSpec((nq * L, KV_W), cur), pl.BlockSpec((L, KV_W), prev),
                  pl.BlockSpec((KV_W, nq * L), cur_t), pl.BlockSpec((KV_W, L), prev_t),
                  _resident((A_HEADS, LANES_V7X))],
        out_specs=[pl.BlockSpec((nq * L, A_WIDTH), cur), pl.BlockSpec((1, L, KV_W), lambda b, i: (b, 0, 0))],
        out_shape=[jax.ShapeDtypeStruct((n, A_WIDTH), F32), jax.ShapeDtypeStruct((nseq, L, KV_W), F32)],
        compiler_params=_params(_vmem_limit(blk, 0, tmp), 2),
        name="swa_prompt",
    )(qt, k, k, vt, vt, sinks)


def _swa_sample_kernel(q_ref, kn_ref, vn_ref, kc_ref, vc_ref, sink_ref, o_ref, ko_ref, vo_ref,
                       *, bt, real):
    P = SAMPLE_PAD_T
    W = WINDOW
    R = A_HEADS * P
    row = lax.broadcasted_iota(jnp.int32, (R, 2 * W), 0)
    key = lax.broadcasted_iota(jnp.int32, (R, 2 * W), 1)
    step = row % P
    mask = ((key < W) & (key >= step)) | ((key >= W) & (key - W <= step) & (key - W < real))
    sink = sink_ref[:, 0:1]
    pad = jnp.zeros((W - P, KV_W), F32)
    scores = []
    for bi in range(bt):
        kext = jnp.concatenate([kc_ref[bi], kn_ref[bi * P:(bi + 1) * P, :], pad], axis=0)
        scores.append(lax.dot_general(q_ref[bi], kext, _NT, preferred_element_type=F32))
    probs = []
    dens = []
    for bi in range(bt):
        s = jnp.where(mask, scores[bi], -jnp.inf)
        m = jnp.maximum(jnp.max(s, axis=-1, keepdims=True), sink)
        p = jnp.exp(s - m)
        dens.append(jnp.sum(p, axis=-1, keepdims=True) + jnp.exp(sink - m))
        probs.append(p)
    for bi in range(bt):
        rs = slice(bi * P, (bi + 1) * P)
        vn = vn_ref[rs, :]
        vext = jnp.concatenate([vc_ref[bi], vn, pad], axis=0)
        o_ref[bi] = jnp.dot(probs[bi], vext, preferred_element_type=F32) / dens[bi]
        ko_ref[bi, 0:W - real, :] = kc_ref[bi, real:W, :]
        ko_ref[bi, W - real:W, :] = kn_ref[bi * P:bi * P + real, :]
        vo_ref[bi, 0:W - real, :] = vc_ref[bi, real:W, :]
        vo_ref[bi, W - real:W, :] = vn[0:real, :]


def _swa_sample(q_rows, kn, vn, kcache, vcache, sinks, *, layer, nseq, bt, real):
    rows = bt * SAMPLE_PAD_T
    R = A_HEADS * SAMPLE_PAD_T
    nb = nseq // bt
    rmap = lambda i: (i, 0)
    cmap = lambda i: (layer * nb + i, 0, 0)
    omap = lambda i: (i, 0, 0)
    cache = pl.BlockSpec((bt, WINDOW, KV_W), cmap)
    blk = 4 * (2 * bt * R * KV_W + 2 * rows * KV_W + 4 * bt * WINDOW * KV_W)
    tmp = 4 * 16 * R * 2 * WINDOW
    kern = functools.partial(_swa_sample_kernel, bt=bt, real=real)
    return pl.pallas_call(
        kern,
        grid=(nb,),
        in_specs=[pl.BlockSpec((bt, R, KV_W), omap), pl.BlockSpec((rows, KV_W), rmap),
                  pl.BlockSpec((rows, KV_W), rmap), cache, cache, _resident((R, LANES_V7X))],
        out_specs=[pl.BlockSpec((bt, R, KV_W), omap),
                   pl.BlockSpec((bt, WINDOW, KV_W), omap), pl.BlockSpec((bt, WINDOW, KV_W), omap)],
        out_shape=[jax.ShapeDtypeStruct((nseq, R, KV_W), F32),
                   jax.ShapeDtypeStruct((nseq, WINDOW, KV_W), F32),
                   jax.ShapeDtypeStruct((nseq, WINDOW, KV_W), F32)],
        compiler_params=_params(_vmem_limit(blk, 0, tmp), 1),
        name="swa_sample",
    )(q_rows, kn, vn, kcache, vcache, sinks)


def _merge_kernel(hm_ref, mo_ref, ha_ref, gt_ref, x_ref, gmh_ref, wa_ref, wb_ref, wo_ref, o_ref):
    hm = hm_ref[...]
    parts = []
    for h in range(M_HEADS):
        v = hm[:, h * M_V_DIM:(h + 1) * M_V_DIM]
        parts.append(v * lax.rsqrt(jnp.mean(v * v, axis=-1, keepdims=True) + EPS))
    hn = jnp.concatenate(parts, axis=1) * gmh_ref[...] * jax.nn.sigmoid(mo_ref[...])
    a = jnp.dot(hn.astype(BF16), wa_ref[...], preferred_element_type=F32)
    b = jnp.dot(ha_ref[...].astype(BF16), wb_ref[...], preferred_element_type=F32)
    gates = jax.nn.sigmoid(gt_ref[...])
    mixed = gates[:, :D_MODEL] * a + gates[:, D_MODEL:] * b
    o_ref[...] = x_ref[...] + jnp.dot(mixed.astype(BF16), wo_ref[...], preferred_element_type=F32)


def _merge(hm, mo, ha, gt, x2d, gmh, wa, wb, wo):
    n = x2d.shape[0]
    t = min(ROW_TILE, n)
    row = lambda w: pl.BlockSpec((t, w), lambda i: (i, 0))
    blk = 4 * t * (3 * M_WIDTH + GATE_W + 2 * D_MODEL)
    res = 2 * (2 * M_WIDTH * D_MODEL + D_MODEL * D_MODEL)
    tmp = 4 * t * (GATE_W + 3 * D_MODEL)
    return pl.pallas_call(
        _merge_kernel,
        grid=(n // t,),
        in_specs=[row(M_WIDTH), row(M_WIDTH), row(A_WIDTH), row(GATE_W), row(D_MODEL),
                  _resident((1, M_WIDTH)), _resident((M_WIDTH, D_MODEL)),
                  _resident((A_WIDTH, D_MODEL)), _resident((D_MODEL, D_MODEL))],
        out_specs=row(D_MODEL),
        out_shape=jax.ShapeDtypeStruct((n, D_MODEL), F32),
        compiler_params=_params(_vmem_limit(blk, res, tmp), 1),
        name="merge",
    )(hm, mo, ha, gt, x2d, gmh, wa, wb, wo)


def _ffn_kernel(x_ref, gn_ref, halo_ref, wup_ref, wc_ref, wdn_ref, gfin_ref, o_ref, halo_out_ref,
                xn_s, acc_s, halo_s, *, g, final_norm):
    t = x_ref.shape[0]
    rb = min(FFN_ROWS, t)
    hp = halo_s.shape[1]
    ti = pl.program_id(1)
    last = pl.num_programs(1) - 1

    @pl.when(ti == 0)
    def _():
        halo_s[...] = jnp.zeros_like(halo_s)
        for j in range(N_FF_CHUNKS):
            halo_s[j, hp - 2 * g:hp, :FF_CHUNK] = halo_ref[0, :, j * FF_CHUNK:(j + 1) * FF_CHUNK]
            halo_s[j, hp - 2 * g:hp, FF_CHUNK:] = halo_ref[0, :, D_FF + j * FF_CHUNK:D_FF + (j + 1) * FF_CHUNK]

    x = x_ref[...]
    xn_s[...] = (x * lax.rsqrt(jnp.mean(x * x, axis=-1, keepdims=True) + EPS) * gn_ref[...]).astype(BF16)

    for j in range(N_FF_CHUNKS):
        w_up = wup_ref[j]
        us = [jnp.dot(xn_s[r:r + rb, :], w_up, preferred_element_type=F32) for r in range(0, t, rb)]
        wc = wc_ref[j]
        prev = halo_s[j]
        acts = []
        for u in us:
            ext = jnp.concatenate([prev, u], axis=0)
            u1 = ext[hp - g:hp - g + rb]
            u2 = ext[hp - 2 * g:hp - 2 * g + rb]
            prev = ext[rb:rb + hp]
            c = wc[3:4, :] + wc[0:1, :] * u2 + wc[1:2, :] * u1 + wc[2:3, :] * u
            a = c[:, :FF_CHUNK]
            act = 0.5 * a * (1.0 + lax.erf(a * np.float32(np.sqrt(0.5)))) * c[:, FF_CHUNK:]
            acts.append(act.astype(BF16))
        halo_s[j] = prev
        w_dn = wdn_ref[j]
        for i, act in enumerate(acts):
            d = jnp.dot(act, w_dn, preferred_element_type=F32)
            if j == 0:
                acc_s[i * rb:(i + 1) * rb, :] = d
            else:
                acc_s[i * rb:(i + 1) * rb, :] += d

    y = x + acc_s[...]
    if final_norm:
        y = y * lax.rsqrt(jnp.mean(y * y, axis=-1, keepdims=True) + EPS) * gfin_ref[...]
    o_ref[...] = y

    @pl.when(ti == last)
    def _():
        for j in range(N_FF_CHUNKS):
            halo_out_ref[0, :, j * FF_CHUNK:(j + 1) * FF_CHUNK] = halo_s[j, hp - 2 * g:hp, :FF_CHUNK]
            halo_out_ref[0, :, D_FF + j * FF_CHUNK:D_FF + (j + 1) * FF_CHUNK] = halo_s[j, hp - 2 * g:hp, FF_CHUNK:]


def _ffn(x2d, gn, halo, wup, wc, wdn, gfin, *, nseq, t, g, final_norm):
    n = x2d.shape[0]
    tiles = n // (nseq * t)
    hp = -(-2 * g // SUBLANES_V7X) * SUBLANES_V7X
    rmap = lambda s, i: (s * tiles + i, 0)
    hmap = lambda s, i: (s, 0, 0)
    blk = 4 * (2 * t * D_MODEL + 2 * 2 * g * 2 * D_FF)
    res = 2 * (D_MODEL * 2 * D_FF + D_FF * D_MODEL) + 4 * N_FF_CHUNKS * SUBLANES_V7X * 2 * FF_CHUNK
    scr = 2 * t * D_MODEL + 4 * t * D_MODEL + 4 * N_FF_CHUNKS * hp * 2 * FF_CHUNK
    tmp = 4 * t * (8 * FF_CHUNK + D_MODEL)
    kern = functools.partial(_ffn_kernel, g=g, final_norm=final_norm)
    return pl.pallas_call(
        kern,
        grid=(nseq, tiles),
        in_specs=[pl.BlockSpec((t, D_MODEL), rmap), _resident((1, D_MODEL)),
                  pl.BlockSpec((1, 2 * g, 2 * D_FF), hmap),
                  _resident((N_FF_CHUNKS, D_MODEL, 2 * FF_CHUNK)),
                  _resident((N_FF_CHUNKS, SUBLANES_V7X, 2 * FF_CHUNK)),
                  _resident((N_FF_CHUNKS, FF_CHUNK, D_MODEL)), _resident((1, D_MODEL))],
        out_specs=[pl.BlockSpec((t, D_MODEL), rmap), pl.BlockSpec((1, 2 * g, 2 * D_FF), hmap)],
        out_shape=[jax.ShapeDtypeStruct((n, D_MODEL), F32),
                   jax.ShapeDtypeStruct((nseq, 2 * g, 2 * D_FF), F32)],
        scratch_shapes=[pltpu.VMEM((t, D_MODEL), BF16), pltpu.VMEM((t, D_MODEL), F32),
                        pltpu.VMEM((N_FF_CHUNKS, hp, 2 * FF_CHUNK), F32)],
        compiler_params=_params(_vmem_limit(blk, res + scr, tmp), 2),
        name="convffn",
    )(x2d, gn, halo, wup, wc, wdn, gfin)


def _prep_layer(l, g_mix, w_in, b_igate, b_fgate, g_mhead, a_sinks, w_a, w_b, w_out, g_ffn, w_up,
                w_conv, b_conv, w_down):
    w = w_in[l]
    seg = lambda i: w[:, _OFF[i]:_OFF[i + 1]]
    w_main = jnp.concatenate([seg(1), seg(3), seg(7), seg(9)], axis=1).astype(BF16)
    w_flex = tuple(seg(i).astype(BF16) for i in (0, 2, 6, 8))
    w_gate8 = jnp.concatenate([seg(4), seg(5)], axis=1)
    w_gate = jnp.pad(w_gate8, ((0, 0), (0, LANES_V7X - 2 * M_HEADS))).astype(BF16)
    w_gate_t = jnp.pad(w_gate8.T, ((0, SUBLANES_V7X), (0, 0))).astype(BF16)
    bias8 = jnp.concatenate([b_igate[l], b_fgate[l]]).astype(F32)
    brow = jnp.pad(bias8, (0, LANES_V7X - 2 * M_HEADS)).reshape(1, LANES_V7X)
    bcol = jnp.broadcast_to(bias8[:, None], (SUBLANES_V7X, LANES_V7X))
    sink8 = a_sinks[l].reshape(A_HEADS, 1).astype(F32)
    sinks = jnp.broadcast_to(sink8, (A_HEADS, LANES_V7X))
    sink_rows = jnp.broadcast_to(jnp.repeat(sink8, SAMPLE_PAD_T, axis=0), (A_HEADS * SAMPLE_PAD_T, LANES_V7X))
    up = w_up[l]
    wup = jnp.stack([jnp.concatenate([up[:, j * FF_CHUNK:(j + 1) * FF_CHUNK],
                                      up[:, D_FF + j * FF_CHUNK:D_FF + (j + 1) * FF_CHUNK]], axis=1)
                     for j in range(N_FF_CHUNKS)]).astype(BF16)
    cw = jnp.concatenate([w_conv[l], b_conv[l][None, :]], axis=0)
    cw = jnp.pad(cw, ((0, SUBLANES_V7X - CONV_W - 1), (0, 0))).astype(F32)
    wc = jnp.stack([jnp.concatenate([cw[:, j * FF_CHUNK:(j + 1) * FF_CHUNK],
                                     cw[:, D_FF + j * FF_CHUNK:D_FF + (j + 1) * FF_CHUNK]], axis=1)
                    for j in range(N_FF_CHUNKS)])
    wdn = w_down[l].reshape(N_FF_CHUNKS, FF_CHUNK, D_MODEL).astype(BF16)
    return dict(g_mix=g_mix[l].reshape(1, D_MODEL), w_main=w_main, w_flex=w_flex,
                w_flex_t=tuple(w.T for w in w_flex), w_gate=w_gate, w_gate_t=w_gate_t,
                brow=brow, bcol=bcol, gmh=g_mhead[l].reshape(1, M_WIDTH), sinks=sinks, sink_rows=sink_rows,
                wa=w_a[l].astype(BF16), wb=w_b[l].astype(BF16), wo=w_out[l].astype(BF16),
                g_ffn=g_ffn[l].reshape(1, D_MODEL), wup=wup, wc=wc, wdn=wdn)


def _rope_tables(pos):
    half = A_HEAD_DIM // 2
    inv = ROPE_THETA ** (-jnp.arange(0, A_HEAD_DIM, 2, dtype=F32) / A_HEAD_DIM)
    ang = pos.astype(F32)[:, None] * inv[None, :]
    ang = jnp.tile(jnp.concatenate([ang, ang], axis=-1), (1, LANES_V7X // A_HEAD_DIM))
    sign = jnp.where((jnp.arange(LANES_V7X) % A_HEAD_DIM) < half, -1.0, 1.0).astype(F32)
    return jnp.cos(ang), jnp.sin(ang) * sign[None, :]


def _rope_tables_t(pos):
    inv = ROPE_THETA ** (-jnp.arange(0, A_HEAD_DIM, 2, dtype=F32) / A_HEAD_DIM)
    ang = inv[:, None] * pos.astype(F32)[None, :]
    return jnp.cos(ang), jnp.sin(ang)


def _prompt_trunk(x, layers, g_final):
    B, S, _ = x.shape
    n = B * S
    x2d = x.reshape(n, D_MODEL)
    cos, sin = _rope_tables(jnp.arange(S, dtype=jnp.int32))
    cos_t, sin_t = _rope_tables_t(jnp.arange(S, dtype=jnp.int32))
    nchunks = S // M_CHUNK
    c0 = jnp.zeros((B, 2, LANES_V7X, LANES_V7X), F32)
    n0 = jnp.zeros((B, 2, LANES_V7X), F32)
    m0 = jnp.zeros((B, SUBLANES_V7X, LANES_V7X), F32)
    halo0 = jnp.zeros((B, CONV_W - 1, 2 * D_FF), F32)
    outs = dict(C=[], n=[], m=[], k=[], v=[], conv=[])
    for l, p in enumerate(layers):
        mk, mo, ak, gt, mq_t, mv_t, aq_t, av_t, gc, gr = _inproj(
            x2d, p["g_mix"], p["w_main"], p["w_flex_t"], p["w_gate"], p["w_gate_t"],
            cos, sin, cos_t, sin_t, S // ROW_TILE, True)
        hm, c_new, n_new, m_new = _mlstm_t(mq_t, mk, mv_t, gc, gr, p["brow"], p["bcol"], c0, n0, m0,
                                           nseq=B, nchunks=nchunks)
        ha, v_last = _swa_prompt(aq_t, ak, av_t, p["sinks"], nseq=B, nblk=S // WINDOW)
        x2d = _merge(hm, mo, ha, gt, x2d, p["gmh"], p["wa"], p["wb"], p["wo"])
        x2d, conv = _ffn(x2d, p["g_ffn"], halo0, p["wup"], p["wc"], p["wdn"], g_final,
                         nseq=B, t=FFN_TILE, g=1, final_norm=(l == DEPTH - 1))
        c_new = jnp.transpose(c_new.reshape(B, M_HEADS // 2, M_V_DIM, 2, M_QK_DIM), (0, 1, 3, 4, 2))
        outs["C"].append(c_new.reshape(B, M_HEADS, M_QK_DIM, M_V_DIM))
        outs["n"].append(n_new.reshape(B, M_HEADS, M_QK_DIM))
        outs["m"].append(m_new[:, :M_HEADS, 0])
        outs["k"].append(ak.reshape(B, S, A_KV_HEADS, A_HEAD_DIM)[:, S - WINDOW:])
        outs["v"].append(v_last.reshape(B, WINDOW, A_KV_HEADS, A_HEAD_DIM))
        outs["conv"].append(conv)
    return (x2d.reshape(B, S, D_MODEL),) + tuple(jnp.stack(outs[k]) for k in ("C", "n", "m", "k", "v", "conv"))


def _sample_trunk(x, st_c, st_n, st_m, st_k, st_v, st_conv, layers, g_final):
    B, T, _ = x.shape
    P = SAMPLE_PAD_T
    bt = LANES_V7X // P
    gb = SAMPLE_FFN_GROUP
    ng = B // gb
    pos =PAST_LEN + (jnp.arange(B * P, dtype=jnp.int32) % P)
    cos, sin = _rope_tables(pos)
    c_all = st_c.reshape(DEPTH * B, 2, LANES_V7X, LANES_V7X)
    n_all = st_n.reshape(DEPTH * B, 2, LANES_V7X)
    k_all = st_k.reshape(DEPTH * B, WINDOW, KV_W)
    v_all = st_v.reshape(DEPTH * B, WINDOW, KV_W)
    xpad = jnp.pad(x, ((0, 0), (0, P - T), (0, 0))).reshape(B * P, D_MODEL)
    outs = dict(C=[], n=[], m=[], k=[], v=[], conv=[])
    y_bm = None
    for l, p in enumerate(layers):
        mk, mo, ak, gt, mq, mv, aq, av, gc, gr = _inproj(
            xpad, p["g_mix"], p["w_main"], p["w_flex"], p["w_gate"], p["w_gate_t"],
            cos, sin, cos, sin, (B * P) // ROW_TILE, False)
        m0 = jnp.broadcast_to(jnp.pad(st_m[l], ((0, 0), (0, SUBLANES_V7X - M_HEADS)))[:, :, None],
                              (B, SUBLANES_V7X, LANES_V7X))
        hm, c_new, n_new, m_new = _mlstm(mq, mk, mv, gc, gr, p["brow"], p["bcol"], c_all, n_all, m0,
                                         layer=l, nseq=B, nchunks=1, chunk=P, real=T, bt=bt, mm_dtype=F32)
        q5 = jnp.transpose(aq.reshape(B, P, A_KV_HEADS, A_GROUP, A_HEAD_DIM), (0, 2, 3, 1, 4))
        q5 = q5.reshape(B, A_KV_HEADS, A_GROUP * P, A_HEAD_DIM)
        zq = jnp.zeros_like(q5[:, 0])
        q_rows = jnp.concatenate([jnp.concatenate([q5[:, 0], zq], axis=-1),
                                  jnp.concatenate([zq, q5[:, 1]], axis=-1)], axis=1)
        o_rows, k_new, v_new = _swa_sample(q_rows, ak, av, k_all, v_all, p["sink_rows"],
                                           layer=l, nseq=B, bt=bt, real=T)
        o5 = jnp.stack([o_rows[:, :A_GROUP * P, :A_HEAD_DIM], o_rows[:, A_GROUP * P:, A_HEAD_DIM:]], axis=1)
        ha = jnp.transpose(o5.reshape(B, A_KV_HEADS, A_GROUP, P, A_HEAD_DIM),
                           (0, 3, 1, 2, 4)).reshape(B * P, A_WIDTH)
        xm = _merge(hm, mo, ha, gt, xpad, p["gmh"], p["wa"], p["wb"], p["wo"])
        x_tm = jnp.transpose(xm.reshape(ng, gb, P, D_MODEL)[:, :, :T], (0, 2, 1, 3)).reshape(T * B, D_MODEL)
        halo = jnp.transpose(st_conv[l].reshape(ng, gb, CONV_W - 1, 2 * D_FF),
                             (0, 2, 1, 3)).reshape(ng, (CONV_W - 1) * gb, 2 * D_FF)
        y_tm, conv = _ffn(x_tm, p["g_ffn"], halo, p["wup"], p["wc"], p["wdn"], g_final,
                          nseq=ng, t=T * gb, g=gb, final_norm=(l == DEPTH - 1))
        y_bm = jnp.transpose(y_tm.reshape(ng, T, gb, D_MODEL), (0, 2, 1, 3)).reshape(B, T, D_MODEL)
        xpad = jnp.pad(y_bm, ((0, 0), (0, P - T), (0, 0))).reshape(B * P, D_MODEL)
        outs["C"].append(c_new.reshape(B, M_HEADS, M_QK_DIM, M_V_DIM))
        outs["n"].append(n_new.reshape(B, M_HEADS, M_QK_DIM))
        outs["m"].append(m_new[:, :M_HEADS, 0])
        outs["k"].append(k_new.reshape(B, WINDOW, A_KV_HEADS, A_HEAD_DIM))
        outs["v"].append(v_new.reshape(B, WINDOW, A_KV_HEADS, A_HEAD_DIM))
        outs["conv"].append(jnp.transpose(conv.reshape(ng, CONV_W - 1, gb, 2 * D_FF),
                                          (0, 2, 1, 3)).reshape(B, CONV_W - 1, 2 * D_FF))
    return (y_bm,) + tuple(jnp.stack(outs[k]) for k in ("C", "n", "m", "k", "v", "conv"))


def kernel(x_prompt, x_sample, state_mlstm_C, state_mlstm_n, state_mlstm_m, cache_swa_k, cache_swa_v,
           state_ffn_conv, g_mix, w_in, b_igate, b_fgate, g_mhead, a_sinks, w_a, w_b, w_out, g_ffn,
           w_up, w_conv, b_conv, w_down, g_final):
    layers = [_prep_layer(l, g_mix, w_in, b_igate, b_fgate, g_mhead, a_sinks, w_a, w_b, w_out, g_ffn,
                          w_up, w_conv, b_conv, w_down) for l in range(DEPTH)]
    gfin = g_final.reshape(1, D_MODEL)
    yp, p_c, p_n, p_m, p_k, p_v, p_conv = _prompt_trunk(x_prompt, layers, gfin)
    ys, s_c, s_n, s_m, s_k, s_v, s_conv = _sample_trunk(
        x_sample, state_mlstm_C, state_mlstm_n, state_mlstm_m, cache_swa_k, cache_swa_v,
        state_ffn_conv, layers, gfin)
    return (yp, ys, p_c, p_n, p_m, p_k, p_v, p_conv, s_c, s_n, s_m, s_k, s_v, s_conv)
```

```python
import functools

import numpy as np
import jax
import jax.numpy as jnp
from jax import lax
from jax.experimental import pallas as pl
from jax.experimental.pallas import tpu as pltpu

F32 = jnp.float32
BF16 = jnp.bfloat16

D_MODEL = 1024
DEPTH = 2
PAST_LEN = 16384
M_HEADS = 4
M_QK_DIM = 64
M_V_DIM = 128
M_WIDTH = M_HEADS * M_V_DIM
M_CHUNK = 128
A_HEADS = 8
A_KV_HEADS = 2
A_GROUP = A_HEADS // A_KV_HEADS
A_HEAD_DIM = 64
A_WIDTH = A_HEADS * A_HEAD_DIM
WINDOW = 128
ROPE_THETA = 10000.0
D_FF = 2816
CONV_W = 3
EPS = 1e-6

QK_W = M_HEADS * M_QK_DIM
KV_W = A_KV_HEADS * A_HEAD_DIM
GATE_W = 2 * D_MODEL
_SPLIT = (QK_W, QK_W, M_WIDTH, M_WIDTH, M_HEADS, M_HEADS, A_WIDTH, KV_W, KV_W, GATE_W)
_OFF = np.concatenate([[0], np.cumsum(_SPLIT)]).tolist()
_MAIN_W = (QK_W, M_WIDTH, KV_W, GATE_W)
_MAIN_OFF = np.concatenate([[0], np.cumsum(_MAIN_W)]).tolist()
MAIN_COLS = _MAIN_OFF[-1]

LANES_V7X = 128
SUBLANES_V7X = 8
VMEM_BYTES_V7X = 64 * 1024 * 1024
VMEM_LIMIT_CAP_V7X = 60000 * 1024

SAMPLE_PAD_T = SUBLANES_V7X
FF_CHUNK = 2 * LANES_V7X
N_FF_CHUNKS = D_FF // FF_CHUNK
ROW_TILE = 512
SAMPLE_FFN_GROUP = 32
SWA_QUERY_BLOCKS = 4
FFN_TILE = 1024
FFN_ROWS = 128
INPROJ_ROWS = 256


def _vmem_limit(block_bytes, scratch_bytes, temp_bytes):
    need = 2 * block_bytes + scratch_bytes + temp_bytes
    assert need <= VMEM_LIMIT_CAP_V7X, need
    return VMEM_LIMIT_CAP_V7X


def _params(vmem_bytes, n_axes):
    return pltpu.CompilerParams(dimension_semantics=("arbitrary",) * n_axes,
                                vmem_limit_bytes=vmem_bytes)


def _resident(shape):
    nd = len(shape)
    return pl.BlockSpec(shape, lambda *_: (0,) * nd, pipeline_mode=pl.Buffered(1))


def _log_sigmoid(x):
    return jnp.minimum(x, 0.0) - jnp.log1p(jnp.exp(-jnp.abs(x)))


_NT = (((1,), (1,)), ((), ()))


def _inproj_kernel(x_ref, g_ref, w_ref, wmq_ref, wmv_ref, wq_ref, wv_ref, wg_ref, wgt_ref, cos_ref,
                   sin_ref, cosq_ref, sinq_ref, mk_ref, mo_ref, ak_ref, gt_ref, mq_ref, mv_ref, aq_ref,
                   av_ref, gc_ref, gr_ref, *, tokens_on_lanes):
    t = x_ref.shape[0]
    rb = min(INPROJ_ROWS, t)
    lane = lax.broadcasted_iota(jnp.int32, (rb, LANES_V7X), 1)
    first_half = (lane % A_HEAD_DIM) < (A_HEAD_DIM // 2)
    half = A_HEAD_DIM // 2
    scale = A_HEAD_DIM ** -0.5

    def rope(v, c, s):
        rot = jnp.where(first_half,
                        pltpu.roll(v, LANES_V7X - A_HEAD_DIM // 2, 1),
                        pltpu.roll(v, A_HEAD_DIM // 2, 1))
        return v * c + rot * s

    for r0 in range(0, t, rb):
        rows = slice(r0, r0 + rb)
        x = x_ref[rows, :]
        xn = (x * lax.rsqrt(jnp.mean(x * x, axis=-1, keepdims=True) + EPS) * g_ref[...]).astype(BF16)

        def proj(idx):
            return jnp.dot(xn, w_ref[:, _MAIN_OFF[idx]:_MAIN_OFF[idx + 1]], preferred_element_type=F32)

        def flex(w_r):
            if tokens_on_lanes:
                return lax.dot_general(w_r[...], xn, _NT, preferred_element_type=F32)
            return jnp.dot(xn, w_r[...], preferred_element_type=F32)

        mk, mo, ak, gt = proj(0), proj(1), proj(2), proj(3)
        mq, mv, aq, av = flex(wmq_ref), flex(wmv_ref), flex(wq_ref), flex(wv_ref)
        gc = jnp.dot(xn, wg_ref[...], preferred_element_type=F32)
        gr = lax.dot_general(wgt_ref[...], xn, _NT, preferred_element_type=F32)

        mk_ref[rows, :] = mk
        mo_ref[rows, :] = mo
        gt_ref[rows, :] = gt
        ak_ref[rows, :] = rope(ak, cos_ref[rows, :], sin_ref[rows, :])
        gc_ref[rows, :] = gc
        gr_ref[:, rows] = gr[:SUBLANES_V7X]
        if tokens_on_lanes:
            cq = cosq_ref[:, rows]
            sq = sinq_ref[:, rows]
            mq_ref[:, rows] = mq * (M_QK_DIM ** -0.5)
            mv_ref[:, rows] = mv
            av_ref[:, rows] = av
            for h in range(A_HEADS):
                x1 = aq[h * A_HEAD_DIM:h * A_HEAD_DIM + half]
                x2 = aq[h * A_HEAD_DIM + half:(h + 1) * A_HEAD_DIM]
                aq_ref[h * A_HEAD_DIM:h * A_HEAD_DIM + half, rows] = (x1 * cq - x2 * sq) * scale
                aq_ref[h * A_HEAD_DIM + half:(h + 1) * A_HEAD_DIM, rows] = (x2 * cq + x1 * sq) * scale
        else:
            cq = cosq_ref[rows, :]
            sq = sinq_ref[rows, :]
            mq_ref[rows, :] = mq * (M_QK_DIM ** -0.5)
            mv_ref[rows, :] = mv
            av_ref[rows, :] = av
            for c in range(A_WIDTH // LANES_V7X):
                sl = slice(c * LANES_V7X, (c + 1) * LANES_V7X)
                aq_ref[rows, sl] = rope(aq[:, sl], cq, sq) * scale


_FLEX_W = (QK_W, M_WIDTH, A_WIDTH, KV_W)


def _inproj(x2d, g, w_main, w_flex, w_gate, w_gate_t, cos, sin, cos_q, sin_q, tiles_per_seq,
            tokens_on_lanes):
    n = x2d.shape[0]
    t = min(ROW_TILE, n)
    grid = (n // t,)
    row = lambda w: pl.BlockSpec((t, w), lambda i: (i, 0))
    col = lambda h: pl.BlockSpec((h, t), lambda i: (0, i))
    tab = pl.BlockSpec((t, LANES_V7X), lambda i: (i % tiles_per_seq, 0))
    if tokens_on_lanes:
        qtab = pl.BlockSpec((A_HEAD_DIM // 2, t), lambda i: (0, i % tiles_per_seq))
        flex_specs = [col(w) for w in _FLEX_W]
        flex_shapes = [(w, n) for w in _FLEX_W]
    else:
        qtab = tab
        flex_specs = [row(w) for w in _FLEX_W]
        flex_shapes = [(n, w) for w in _FLEX_W]
    out_shapes = [jax.ShapeDtypeStruct((n, w), F32) for w in _MAIN_W]
    out_shapes += [jax.ShapeDtypeStruct(s, F32) for s in flex_shapes]
    out_shapes += [jax.ShapeDtypeStruct((n, LANES_V7X), F32), jax.ShapeDtypeStruct((SUBLANES_V7X, n), F32)]
    out_specs = [row(w) for w in _MAIN_W] + flex_specs + [row(LANES_V7X), col(SUBLANES_V7X)]
    flex_cols = sum(_FLEX_W)
    blk = 4 * t * (D_MODEL + MAIN_COLS + flex_cols + 5 * LANES_V7X + SUBLANES_V7X)
    res = 2 * D_MODEL * (MAIN_COLS + flex_cols + LANES_V7X + 2 * SUBLANES_V7X)
    tmp = 4 * t * (GATE_W + D_MODEL)
    kern = functools.partial(_inproj_kernel, tokens_on_lanes=tokens_on_lanes)
    return pl.pallas_call(
        kern,
        grid=grid,
        in_specs=[row(D_MODEL), _resident((1, D_MODEL)), _resident((D_MODEL, MAIN_COLS))]
        + [_resident(w.shape) for w in w_flex]
        + [_resident((D_MODEL, LANES_V7X)), _resident((2 * SUBLANES_V7X, D_MODEL)), tab, tab, qtab, qtab],
        out_specs=out_specs,
        out_shape=out_shapes,
        compiler_params=_params(_vmem_limit(blk, res, tmp), 1),
        name="inproj",
    )(x2d, g, w_main, *w_flex, w_gate, w_gate_t, cos, sin, cos_q, sin_q)


def _mlstm_kernel(q_ref, k_ref, v_ref, gc_ref, gr_ref, brow_ref, bcol_ref, c0_ref, n0_ref, m0_ref,
                  h_ref, c_ref, n_ref, m_ref, *, chunk, real, bt, mm_dtype):
    L = chunk
    ci = pl.program_id(1)

    @pl.when(ci == 0)
    def _():
        c_ref[...] = c0_ref[...]
        n_ref[...] = n0_ref[...]
        m_ref[...] = m0_ref[...]

    def mx(a):
        return a.astype(mm_dtype)

    def split3(a):
        hi = a.astype(BF16).astype(F32)
        r1 = a - hi
        mid = r1.astype(BF16).astype(F32)
        lo = (r1 - mid).astype(BF16).astype(F32)
        return hi, mid, lo

    ti = lax.broadcasted_iota(jnp.int32, (L, L), 0)
    si = lax.broadcasted_iota(jnp.int32, (L, L), 1)
    causal = si <= ti
    tri = causal.astype(F32)
    tri_t = (ti <= si).astype(F32)
    lane_g = lax.broadcasted_iota(jnp.int32, (L, LANES_V7X), 1)
    row_g = lax.broadcasted_iota(jnp.int32, (L, LANES_V7X), 0)
    is_f_col = (lane_g >= M_HEADS) & (lane_g < 2 * M_HEADS)
    row_r = lax.broadcasted_iota(jnp.int32, (SUBLANES_V7X, L), 0)
    col_r = lax.broadcasted_iota(jnp.int32, (SUBLANES_V7X, L), 1)
    is_f_row = row_r >= M_HEADS
    lane_p = lax.broadcasted_iota(jnp.int32, (L, LANES_V7X), 1)
    half_masks = (lane_p < M_QK_DIM, lane_p >= M_QK_DIM)
    rows_c = lax.broadcasted_iota(jnp.int32, (2 * M_QK_DIM, 1), 0)
    lanes_n = lax.broadcasted_iota(jnp.int32, (1, LANES_V7X), 1)
    neg_inf = jnp.float32(-jnp.inf)

    units = [(bi, h) for bi in range(bt) for h in range(M_HEADS)]
    gates = []
    for bi in range(bt):
        rs = slice(bi * L, (bi + 1) * L)
        gcol = gc_ref[rs, :] + brow_ref[...]
        gcol = jnp.where(is_f_col, _log_sigmoid(gcol), gcol)
        fcol = jnp.where(is_f_col, gcol, 0.0)
        grow = gr_ref[:, rs] + bcol_ref[:, :L]
        grow = jnp.where(is_f_row, _log_sigmoid(grow), grow)
        frow = jnp.where(is_f_row, grow, 0.0)
        if real < L:
            gcol = jnp.where(row_g < real, gcol, neg_inf)
            fcol = jnp.where(row_g < real, fcol, 0.0)
            grow = jnp.where(col_r < real, grow, neg_inf)
            frow = jnp.where(col_r < real, frow, 0.0)
        cum_c = sum(jnp.dot(tri, p, preferred_element_type=F32) for p in split3(fcol))
        cum_r = sum(jnp.dot(p, tri_t, preferred_element_type=F32) for p in split3(frow))
        gates.append((gcol, grow, cum_c, cum_r))

    qhs, kps, vhs, s_raw, q_c = {}, {}, {}, {}, {}
    for bi, h in units:
        rs = slice(bi * L, (bi + 1) * L)
        pair, half = divmod(h, 2)
        ps = slice(pair * LANES_V7X, (pair + 1) * LANES_V7X)
        kp = k_ref[rs, ps]
        qh = jnp.where(half_masks[half], q_ref[rs, ps], 0.0)
        qhs[bi, h], kps[bi, h] = qh, kp
        vhs[bi, h] = v_ref[rs, h * M_V_DIM:(h + 1) * M_V_DIM]
        s_raw[bi, h] = lax.dot_general(mx(qh), mx(kp), _NT, preferred_element_type=F32)
        q_c[bi, h] = jnp.dot(mx(qh), mx(c_ref[bi, pair]), preferred_element_type=F32)

    s_w, kws, iscales, dens_, mrows, decays, new_m = {}, {}, {}, {}, {}, {}, {}
    for bi, h in units:
        gcol, grow, cum_c, cum_r = gates[bi]
        pair, half = divmod(h, 2)
        b_col = cum_c[:, M_HEADS + h:M_HEADS + h + 1]
        li_col = gcol[:, h:h + 1]
        b_row = cum_r[M_HEADS + h:M_HEADS + h + 1, :]
        li_row = grow[h:h + 1, :]
        m0 = m_ref[bi, h:h + 1, 0:1]
        npair = n_ref[bi, pair:pair + 1, :]
        dmat = jnp.where(causal, b_col - b_row + li_row, neg_inf)
        inter = b_col + m0
        mrow = jnp.maximum(inter, jnp.max(dmat, axis=-1, keepdims=True))
        iscale = jnp.exp(inter - mrow)
        s = s_raw[bi, h] * jnp.exp(dmat - mrow)
        qn = jnp.sum(qhs[bi, h] * npair, axis=-1, keepdims=True)
        dens_[bi, h] = iscale * qn + jnp.sum(s, axis=-1, keepdims=True)
        m_new = mrow[L - 1:L, :]
        b_last = b_col[L - 1:L, :]
        w_col = jnp.exp(b_last - b_col + li_col - m_new)
        decays[bi, h] = jnp.exp(b_last + m0 - m_new)
        kws[bi, h] = jnp.where(half_masks[half], kps[bi, h] * w_col, 0.0)
        s_w[bi, h], iscales[bi, h], mrows[bi, h], new_m[bi, h] = s, iscale, mrow, m_new

    d_c = {}
    for bi, h in units:
        rs = slice(bi * L, (bi + 1) * L)
        vh = vhs[bi, h]
        num = iscales[bi, h] * q_c[bi, h] + jnp.dot(mx(s_w[bi, h]), mx(vh), preferred_element_type=F32)
        h_ref[rs, h * M_V_DIM:(h + 1) * M_V_DIM] = \
            num / jnp.maximum(jnp.abs(dens_[bi, h]), jnp.exp(-mrows[bi, h]))
        d_c[bi, h] = lax.dot_general(mx(kws[bi, h]), mx(vh), (((0,), (0,)), ((), ())),
                                     preferred_element_type=F32)

    for bi in range(bt):
        for pair in range(M_HEADS // 2):
            h0, h1 = 2 * pair, 2 * pair + 1
            npair = n_ref[bi, pair:pair + 1, :]
            d_n = jnp.sum(kws[bi, h0], axis=0, keepdims=True) + jnp.sum(kws[bi, h1], axis=0, keepdims=True)
            c_ref[bi, pair] = jnp.where(rows_c < M_QK_DIM, decays[bi, h0], decays[bi, h1]) * c_ref[bi, pair] \
                + d_c[bi, h0] + d_c[bi, h1]
            n_ref[bi, pair:pair + 1, :] = \
                jnp.where(lanes_n < M_QK_DIM, decays[bi, h0], decays[bi, h1]) * npair + d_n
        for h in range(M_HEADS):
            m_ref[bi, h:h + 1, :] = jnp.broadcast_to(new_m[bi, h], (1, LANES_V7X))


def _mlstm(q, k, v, gc, gr, brow, bcol, c0, n0, m0, *, layer, nseq, nchunks, chunk, real, bt, mm_dtype):
    rows = bt * chunk
    grid = (nseq // bt, nchunks)
    nb = nseq // bt
    rmap = lambda b, c: (b * nchunks + c, 0)
    smap3 = lambda b, c: (b, 0, 0)
    smap4 = lambda b, c: (b, 0, 0, 0)
    c_spec = pl.BlockSpec((bt, 2, LANES_V7X, LANES_V7X), smap4)
    n_spec = pl.BlockSpec((bt, 2, LANES_V7X), smap3)
    m_spec = pl.BlockSpec((bt, SUBLANES_V7X, LANES_V7X), smap3)
    c_in = pl.BlockSpec((bt, 2, LANES_V7X, LANES_V7X), lambda b, c: (layer * nb + b, 0, 0, 0))
    n_in = pl.BlockSpec((bt, 2, LANES_V7X), lambda b, c: (layer * nb + b, 0, 0))
    n = q.shape[0]
    blk = 4 * (rows * (2 * QK_W + 2 * M_WIDTH + LANES_V7X + SUBLANES_V7X)
               + 2 * bt * (2 * LANES_V7X * LANES_V7X + 2 * LANES_V7X + SUBLANES_V7X * LANES_V7X))
    tmp = 4 * 16 * chunk * max(chunk, LANES_V7X)
    kern = functools.partial(_mlstm_kernel, chunk=chunk, real=real, bt=bt, mm_dtype=mm_dtype)
    return pl.pallas_call(
        kern,
        grid=grid,
        in_specs=[pl.BlockSpec((rows, QK_W), rmap), pl.BlockSpec((rows, QK_W), rmap),
                  pl.BlockSpec((rows, M_WIDTH), rmap), pl.BlockSpec((rows, LANES_V7X), rmap),
                  pl.BlockSpec((SUBLANES_V7X, rows), lambda b, c: (0, b * nchunks + c)),
                  _resident((1, LANES_V7X)), _resident((SUBLANES_V7X, LANES_V7X)),
                  c_in, n_in, m_spec],
        out_specs=[pl.BlockSpec((rows, M_WIDTH), rmap), c_spec, n_spec, m_spec],
        out_shape=[jax.ShapeDtypeStruct((n, M_WIDTH), F32),
                   jax.ShapeDtypeStruct((nseq, 2, LANES_V7X, LANES_V7X), F32),
                   jax.ShapeDtypeStruct((nseq, 2, LANES_V7X), F32),
                   jax.ShapeDtypeStruct((nseq, SUBLANES_V7X, LANES_V7X), F32)],
        compiler_params=_params(_vmem_limit(blk, 0, tmp), 2),
        name="mlstm",
    )(q, k, v, gc, gr, brow, bcol, c0, n0, m0)


def _mlstm_t_kernel(qt_ref, k_ref, vt_ref, gc_ref, gr_ref, brow_ref, bcol_ref, c0_ref, n0_ref, m0_ref,
                    h_ref, c_ref, n_ref, m_ref):
    L = M_CHUNK
    ci = pl.program_id(1)

    @pl.when(ci == 0)
    def _():
        c_ref[...] = c0_ref[...]
        n_ref[...] = n0_ref[...]
        m_ref[...] = m0_ref[...]

    def split3(a):
        hi = a.astype(BF16).astype(F32)
        r1 = a - hi
        mid = r1.astype(BF16).astype(F32)
        lo = (r1 - mid).astype(BF16).astype(F32)
        return hi, mid, lo

    si = lax.broadcasted_iota(jnp.int32, (L, L), 0)
    ti = lax.broadcasted_iota(jnp.int32, (L, L), 1)
    causal = si <= ti
    tri = (ti <= si).astype(F32)
    tri_t = causal.astype(F32)
    lane_g = lax.broadcasted_iota(jnp.int32, (L, LANES_V7X), 1)
    is_f_col = (lane_g >= M_HEADS) & (lane_g < 2 * M_HEADS)
    row_r = lax.broadcasted_iota(jnp.int32, (SUBLANES_V7X, L), 0)
    is_f_row = row_r >= M_HEADS
    half_masks = (lane_g < M_QK_DIM, lane_g >= M_QK_DIM)
    lanes_n = lax.broadcasted_iota(jnp.int32, (1, LANES_V7X), 1)
    neg_inf = jnp.float32(-jnp.inf)
    PADR = 2 * SUBLANES_V7X

    gcol = gc_ref[...] + brow_ref[...]
    gcol = jnp.where(is_f_col, _log_sigmoid(gcol), gcol)
    grow = gr_ref[...] + bcol_ref[...]
    grow = jnp.where(is_f_row, _log_sigmoid(grow), grow)
    cum_c = sum(jnp.dot(tri, p, preferred_element_type=F32)
                for p in split3(jnp.where(is_f_col, gcol, 0.0)))
    cum_r = sum(jnp.dot(p, tri_t, preferred_element_type=F32)
                for p in split3(jnp.where(is_f_row, grow, 0.0)))

    zq = jnp.zeros((M_QK_DIM, L), BF16)
    zpad = jnp.zeros((PADR - 1, LANES_V7X), F32)
    kps, s_raw, inter = [], [], []
    for h in range(M_HEADS):
        pair, half = divmod(h, 2)
        kp = k_ref[:, pair * LANES_V7X:(pair + 1) * LANES_V7X]
        qh = qt_ref[h * M_QK_DIM:(h + 1) * M_QK_DIM, :].astype(BF16)
        qz = jnp.concatenate([qh, zq] if half == 0 else [zq, qh], axis=0)
        caug = jnp.concatenate([c_ref[0, pair], n_ref[0, pair:pair + 1, :], zpad], axis=0)
        kps.append(kp)
        s_raw.append(jnp.dot(kp.astype(BF16), qz, preferred_element_type=F32))
        inter.append(jnp.dot(caug.astype(BF16), qz, preferred_element_type=F32))
    s_w, iscales, dens_, m_rows, w_rows, decays = [], [], [], [], [], []
    for h in range(M_HEADS):
        b_row = cum_r[M_HEADS + h:M_HEADS + h + 1, :]
        li_row = grow[h:h + 1, :]
        d_col = gcol[:, h:h + 1] - cum_c[:, M_HEADS + h:M_HEADS + h + 1]
        m0 = m_ref[0, h:h + 1, 0:1]
        dmat = jnp.where(causal, b_row + d_col, neg_inf)
        inter_m = b_row + m0
        m_row = jnp.maximum(inter_m, jnp.max(dmat, axis=0, keepdims=True))
        s = s_raw[h] * jnp.exp(dmat - m_row)
        iscale = jnp.exp(inter_m - m_row)
        dens_.append(iscale * inter[h][M_V_DIM:M_V_DIM + 1, :] + jnp.sum(s, axis=0, keepdims=True))
        m_new = m_row[:, L - 1:L]
        b_last = b_row[:, L - 1:L]
        w_rows.append(jnp.exp(b_last - b_row + li_row - m_new))
        decays.append(jnp.exp(b_last + m0 - m_new))
        s_w.append(s.astype(BF16))
        iscales.append(iscale)
        m_rows.append(m_row)
    outs, upds = [], []
    for h in range(M_HEADS):
        pair, half = divmod(h, 2)
        vt = vt_ref[h * M_V_DIM:(h + 1) * M_V_DIM, :]
        num = iscales[h] * inter[h][:M_V_DIM, :] + jnp.dot(vt.astype(BF16), s_w[h], preferred_element_type=F32)
        outs.append(num / jnp.maximum(jnp.abs(dens_[h]), jnp.exp(-m_rows[h])))
        vaug = jnp.concatenate([vt * w_rows[h], w_rows[h], jnp.zeros((PADR - 1, L), F32)], axis=0)
        kmask = jnp.where(half_masks[half], kps[h], 0.0)
        upds.append(jnp.dot(vaug.astype(BF16), kmask.astype(BF16), preferred_element_type=F32))
    for pair in range(M_HEADS // 2):
        dl = jnp.where(lanes_n < M_QK_DIM, decays[2 * pair], decays[2 * pair + 1])
        u0, u1 = upds[2 * pair], upds[2 * pair + 1]
        n_old = n_ref[0, pair:pair + 1, :]
        c_ref[0, pair] = dl * c_ref[0, pair] + u0[:M_V_DIM] + u1[:M_V_DIM]
        n_ref[0, pair:pair + 1, :] = dl * n_old + u0[M_V_DIM:M_V_DIM + 1] + u1[M_V_DIM:M_V_DIM + 1]
    for h in range(M_HEADS):
        m_ref[0, h:h + 1, :] = jnp.broadcast_to(m_rows[h][:, L - 1:L], (1, LANES_V7X))
    h_ref[...] = jnp.concatenate(outs, axis=0).T


def _mlstm_t(qt, k, vt, gc, gr, brow, bcol, c0, n0, m0, *, nseq, nchunks):
    L = M_CHUNK
    n = k.shape[0]
    rmap = lambda b, c: (b * nchunks + c, 0)
    cmap = lambda b, c: (0, b * nchunks + c)
    smap3 = lambda b, c: (b, 0, 0)
    smap4 = lambda b, c: (b, 0, 0, 0)
    c_spec = pl.BlockSpec((1, 2, LANES_V7X, LANES_V7X), smap4)
    n_spec = pl.BlockSpec((1, 2, LANES_V7X), smap3)
    m_spec = pl.BlockSpec((1, SUBLANES_V7X, LANES_V7X), smap3)
    blk = 4 * (L * (2 * QK_W + 2 * M_WIDTH + LANES_V7X + SUBLANES_V7X)
               + 2 * (2 * LANES_V7X * LANES_V7X + 2 * LANES_V7X + SUBLANES_V7X * LANES_V7X))
    tmp = 4 * 48 * L * L
    return pl.pallas_call(
        _mlstm_t_kernel,
        grid=(nseq, nchunks),
        in_specs=[pl.BlockSpec((QK_W, L), cmap), pl.BlockSpec((L, QK_W), rmap),
                  pl.BlockSpec((M_WIDTH, L), cmap), pl.BlockSpec((L, LANES_V7X), rmap),
                  pl.BlockSpec((SUBLANES_V7X, L), cmap),
                  _resident((1, LANES_V7X)), _resident((SUBLANES_V7X, LANES_V7X)),
                  c_spec, n_spec, m_spec],
        out_specs=[pl.BlockSpec((L, M_WIDTH), rmap), c_spec, n_spec, m_spec],
        out_shape=[jax.ShapeDtypeStruct((n, M_WIDTH), F32),
                   jax.ShapeDtypeStruct((nseq, 2, LANES_V7X, LANES_V7X), F32),
                   jax.ShapeDtypeStruct((nseq, 2, LANES_V7X), F32),
                   jax.ShapeDtypeStruct((nseq, SUBLANES_V7X, LANES_V7X), F32)],
        compiler_params=_params(_vmem_limit(blk, 0, tmp), 2),
        name="mlstm_t",
    )(qt, k, vt, gc, gr, brow, bcol, c0, n0, m0)


def _swa_prompt_kernel(qt_ref, kc_ref, kp_ref, vtc_ref, vtp_ref, sink_ref, o_ref, vlast_ref):
    blk = pl.program_id(1)
    L = WINDOW
    nq = kc_ref.shape[0] // L

    @pl.when(blk == pl.num_programs(1) - 1)
    def _():
        vlast_ref[0] = vtc_ref[:, (nq - 1) * L:nq * L].T

    key = lax.broadcasted_iota(jnp.int32, (2 * L, L), 0)
    qry = lax.broadcasted_iota(jnp.int32, (2 * L, L), 1)
    prev_off = jnp.where(blk > 0, 0, L)
    band = (key >= L) & (key - L <= qry)
    masks = [((key < L) & (key >= qry + prev_off)) | band] + [((key < L) & (key >= qry)) | band] * (nq - 1)
    k_all = jnp.concatenate([kp_ref[...], kc_ref[...]], axis=0).astype(BF16)
    vt_all = jnp.concatenate([vtp_ref[...], vtc_ref[...]], axis=1).astype(BF16)
    zeros = jnp.zeros((A_HEAD_DIM, L), BF16)
    units = [(qb, hd) for qb in range(nq) for hd in range(A_HEADS)]
    scores = {}
    for qb, hd in units:
        qh = qt_ref[hd * A_HEAD_DIM:(hd + 1) * A_HEAD_DIM, qb * L:(qb + 1) * L].astype(BF16)
        qz = jnp.concatenate([qh, zeros] if hd < A_GROUP else [zeros, qh], axis=0)
        scores[qb, hd] = jnp.dot(k_all[qb * L:(qb + 2) * L], qz, preferred_element_type=F32)
    probs = {}
    dens = {}
    for qb, hd in units:
        s = jnp.where(masks[qb], scores[qb, hd], -jnp.inf)
        sink = sink_ref[hd:hd + 1, 0:1]
        m = jnp.maximum(jnp.max(s, axis=0, keepdims=True), sink)
        p = jnp.exp(s - m)
        dens[qb, hd] = jnp.sum(p, axis=0, keepdims=True) + jnp.exp(sink - m)
        probs[qb, hd] = p.astype(BF16)
    for qb in range(nq):
        outs = []
        for hd in range(A_HEADS):
            kv = hd // A_GROUP
            o = jnp.dot(vt_all[kv * A_HEAD_DIM:(kv + 1) * A_HEAD_DIM, qb * L:(qb + 2) * L], probs[qb, hd],
                        preferred_element_type=F32)
            outs.append(o / dens[qb, hd])
        o_ref[qb * L:(qb + 1) * L, :] = jnp.concatenate(outs, axis=0).T


def _swa_prompt(qt, k, vt, sinks, *, nseq, nblk):
    n = k.shape[0]
    L = WINDOW
    nq = SWA_QUERY_BLOCKS
    steps = nblk // nq
    cur = lambda b, i: (b * steps + i, 0)
    prev = lambda b, i: (b * nblk + jnp.maximum(nq * i - 1, 0), 0)
    cur_t = lambda b, i: (0, b * steps + i)
    prev_t = lambda b, i: (0, b * nblk + jnp.maximum(nq * i - 1, 0))
    blk = 4 * L * (2 * nq * A_WIDTH + 2 * (nq + 1) * KV_W)
    tmp = 4 * 24 * 2 * L * L * nq
    return pl.pallas_call(
        _swa_prompt_kernel,
        grid=(nseq, steps),
        in_specs=[pl.BlockSpec((A_WIDTH, nq * L), cur_t),
                  pl.BlockSpec((nq * L, KV_W), cur), pl.BlockSpec((L, KV_W), prev),
                  pl.BlockSpec((KV_W, nq * L), cur_t), pl.BlockSpec((KV_W, L), prev_t),
                  _resident((A_HEADS, LANES_V7X))],
        out_specs=[pl.BlockSpec((nq * L, A_WIDTH), cur), pl.BlockSpec((1, L, KV_W), lambda b, i: (b, 0, 0))],
        out_shape=[jax.ShapeDtypeStruct((n, A_WIDTH), F32), jax.ShapeDtypeStruct((nseq, L, KV_W), F32)],
        compiler_params=_params(_vmem_limit(blk, 0, tmp), 2),
        name="swa_prompt",
    )(qt, k, k, vt, vt, sinks)


def _swa_sample_kernel(q_ref, kn_ref, vn_ref, kc_ref, vc_ref, sink_ref, o_ref, ko_ref, vo_ref,
                       *, bt, real):
    P = SAMPLE_PAD_T
    W = WINDOW
    R = A_HEADS * P
    row = lax.broadcasted_iota(jnp.int32, (R, 2 * W), 0)
    key = lax.broadcasted_iota(jnp.int32, (R, 2 * W), 1)
    step = row % P
    mask = ((key < W) & (key >= step)) | ((key >= W) & (key - W <= step) & (key - W < real))
    sink = sink_ref[:, 0:1]
    pad = jnp.zeros((W - P, KV_W), F32)
    scores = []
    for bi in range(bt):
        kext = jnp.concatenate([kc_ref[bi], kn_ref[bi * P:(bi + 1) * P, :], pad], axis=0)
        scores.append(lax.dot_general(q_ref[bi], kext, _NT, preferred_element_type=F32))
    probs = []
    dens = []
    for bi in range(bt):
        s = jnp.where(mask, scores[bi], -jnp.inf)
        m = jnp.maximum(jnp.max(s, axis=-1, keepdims=True), sink)
        p = jnp.exp(s - m)
        dens.append(jnp.sum(p, axis=-1, keepdims=True) + jnp.exp(sink - m))
        probs.append(p)
    for bi in range(bt):
        rs = slice(bi * P, (bi + 1) * P)
        vn = vn_ref[rs, :]
        vext = jnp.concatenate([vc_ref[bi], vn, pad], axis=0)
        o_ref[bi] = jnp.dot(probs[bi], vext, preferred_element_type=F32) / dens[bi]
        ko_ref[bi, 0:W - real, :] = kc_ref[bi, real:W, :]
        ko_ref[bi, W - real:W, :] = kn_ref[bi * P:bi * P + real, :]
        vo_ref[bi, 0:W - real, :] = vc_ref[bi, real:W, :]
        vo_ref[bi, W - real:W, :] = vn[0:real, :]


def _swa_sample(q_rows, kn, vn, kcache, vcache, sinks, *, layer, nseq, bt, real):
    rows = bt * SAMPLE_PAD_T
    R = A_HEADS * SAMPLE_PAD_T
    nb = nseq // bt
    rmap = lambda i: (i, 0)
    cmap = lambda i: (layer * nb + i, 0, 0)
    omap = lambda i: (i, 0, 0)
    cache = pl.BlockSpec((bt, WINDOW, KV_W), cmap)
    blk = 4 * (2 * bt * R * KV_W + 2 * rows * KV_W + 4 * bt * WINDOW * KV_W)
    tmp = 4 * 16 * R * 2 * WINDOW
    kern = functools.partial(_swa_sample_kernel, bt=bt, real=real)
    return pl.pallas_call(
        kern,
        grid=(nb,),
        in_specs=[pl.BlockSpec((bt, R, KV_W), omap), pl.BlockSpec((rows, KV_W), rmap),
                  pl.BlockSpec((rows, KV_W), rmap), cache, cache, _resident((R, LANES_V7X))],
        out_specs=[pl.BlockSpec((bt, R, KV_W), omap),
                   pl.BlockSpec((bt, WINDOW, KV_W), omap), pl.BlockSpec((bt, WINDOW, KV_W), omap)],
        out_shape=[jax.ShapeDtypeStruct((nseq, R, KV_W), F32),
                   jax.ShapeDtypeStruct((nseq, WINDOW, KV_W), F32),
                   jax.ShapeDtypeStruct((nseq, WINDOW, KV_W), F32)],
        compiler_params=_params(_vmem_limit(blk, 0, tmp), 1),
        name="swa_sample",
    )(q_rows, kn, vn, kcache, vcache, sinks)


def _merge_kernel(hm_ref, mo_ref, ha_ref, gt_ref, x_ref, gmh_ref, wa_ref, wb_ref, wo_ref, o_ref):
    hm = hm_ref[...]
    parts = []
    for h in range(M_HEADS):
        v = hm[:, h * M_V_DIM:(h + 1) * M_V_DIM]
        parts.append(v * lax.rsqrt(jnp.mean(v * v, axis=-1, keepdims=True) + EPS))
    hn = jnp.concatenate(parts, axis=1) * gmh_ref[...] * jax.nn.sigmoid(mo_ref[...])
    a = jnp.dot(hn.astype(BF16), wa_ref[...], preferred_element_type=F32)
    b = jnp.dot(ha_ref[...].astype(BF16), wb_ref[...], preferred_element_type=F32)
    gates = jax.nn.sigmoid(gt_ref[...])
    mixed = gates[:, :D_MODEL] * a + gates[:, D_MODEL:] * b
    o_ref[...] = x_ref[...] + jnp.dot(mixed.astype(BF16), wo_ref[...], preferred_element_type=F32)


def _merge(hm, mo, ha, gt, x2d, gmh, wa, wb, wo):
    n = x2d.shape[0]
    t = min(ROW_TILE, n)
    row = lambda w: pl.BlockSpec((t, w), lambda i: (i, 0))
    blk = 4 * t * (3 * M_WIDTH + GATE_W + 2 * D_MODEL)
    res = 2 * (2 * M_WIDTH * D_MODEL + D_MODEL * D_MODEL)
    tmp = 4 * t * (GATE_W + 3 * D_MODEL)
    return pl.pallas_call(
        _merge_kernel,
        grid=(n // t,),
        in_specs=[row(M_WIDTH), row(M_WIDTH), row(A_WIDTH), row(GATE_W), row(D_MODEL),
                  _resident((1, M_WIDTH)), _resident((M_WIDTH, D_MODEL)),
                  _resident((A_WIDTH, D_MODEL)), _resident((D_MODEL, D_MODEL))],
        out_specs=row(D_MODEL),
        out_shape=jax.ShapeDtypeStruct((n, D_MODEL), F32),
        compiler_params=_params(_vmem_limit(blk, res, tmp), 1),
        name="merge",
    )(hm, mo, ha, gt, x2d, gmh, wa, wb, wo)


def _ffn_kernel(x_ref, gn_ref, halo_ref, wup_ref, wc_ref, wdn_ref, gfin_ref, o_ref, halo_out_ref,
                xn_s, acc_s, halo_s, *, g, final_norm):
    t = x_ref.shape[0]
    rb = min(FFN_ROWS, t)
    hp = halo_s.shape[1]
    ti = pl.program_id(1)
    last = pl.num_programs(1) - 1

    @pl.when(ti == 0)
    def _():
        halo_s[...] = jnp.zeros_like(halo_s)
        for j in range(N_FF_CHUNKS):
            halo_s[j, hp - 2 * g:hp, :FF_CHUNK] = halo_ref[0, :, j * FF_CHUNK:(j + 1) * FF_CHUNK]
            halo_s[j, hp - 2 * g:hp, FF_CHUNK:] = halo_ref[0, :, D_FF + j * FF_CHUNK:D_FF + (j + 1) * FF_CHUNK]

    x = x_ref[...]
    xn_s[...] = (x * lax.rsqrt(jnp.mean(x * x, axis=-1, keepdims=True) + EPS) * gn_ref[...]).astype(BF16)

    for j in range(N_FF_CHUNKS):
        w_up = wup_ref[j]
        us = [jnp.dot(xn_s[r:r + rb, :], w_up, preferred_element_type=F32) for r in range(0, t, rb)]
        wc = wc_ref[j]
        prev = halo_s[j]
        acts = []
        for u in us:
            ext = jnp.concatenate([prev, u], axis=0)
            u1 = ext[hp - g:hp - g + rb]
            u2 = ext[hp - 2 * g:hp - 2 * g + rb]
            prev = ext[rb:rb + hp]
            c = wc[3:4, :] + wc[0:1, :] * u2 + wc[1:2, :] * u1 + wc[2:3, :] * u
            a = c[:, :FF_CHUNK]
            act = 0.5 * a * (1.0 + lax.erf(a * np.float32(np.sqrt(0.5)))) * c[:, FF_CHUNK:]
            acts.append(act.astype(BF16))
        halo_s[j] = prev
        w_dn = wdn_ref[j]
        for i, act in enumerate(acts):
            d = jnp.dot(act, w_dn, preferred_element_type=F32)
            if j == 0:
                acc_s[i * rb:(i + 1) * rb, :] = d
            else:
                acc_s[i * rb:(i + 1) * rb, :] += d

    y = x + acc_s[...]
    if final_norm:
        y = y * lax.rsqrt(jnp.mean(y * y, axis=-1, keepdims=True) + EPS) * gfin_ref[...]
    o_ref[...] = y

    @pl.when(ti == last)
    def _():
        for j in range(N_FF_CHUNKS):
            halo_out_ref[0, :, j * FF_CHUNK:(j + 1) * FF_CHUNK] = halo_s[j, hp - 2 * g:hp, :FF_CHUNK]
            halo_out_ref[0, :, D_FF + j * FF_CHUNK:D_FF + (j + 1) * FF_CHUNK] = halo_s[j, hp - 2 * g:hp, FF_CHUNK:]


def _ffn(x2d, gn, halo, wup, wc, wdn, gfin, *, nseq, t, g, final_norm):
    n = x2d.shape[0]
    tiles = n // (nseq * t)
    hp = -(-2 * g // SUBLANES_V7X) * SUBLANES_V7X
    rmap = lambda s, i: (s * tiles + i, 0)
    hmap = lambda s, i: (s, 0, 0)
    blk = 4 * (2 * t * D_MODEL + 2 * 2 * g * 2 * D_FF)
    res = 2 * (D_MODEL * 2 * D_FF + D_FF * D_MODEL) + 4 * N_FF_CHUNKS * SUBLANES_V7X * 2 * FF_CHUNK
    scr = 2 * t * D_MODEL + 4 * t * D_MODEL + 4 * N_FF_CHUNKS * hp * 2 * FF_CHUNK
    tmp = 4 * t * (8 * FF_CHUNK + D_MODEL)
    kern = functools.partial(_ffn_kernel, g=g, final_norm=final_norm)
    return pl.pallas_call(
        kern,
        grid=(nseq, tiles),
        in_specs=[pl.BlockSpec((t, D_MODEL), rmap), _resident((1, D_MODEL)),
                  pl.BlockSpec((1, 2 * g, 2 * D_FF), hmap),
                  _resident((N_FF_CHUNKS, D_MODEL, 2 * FF_CHUNK)),
                  _resident((N_FF_CHUNKS, SUBLANES_V7X, 2 * FF_CHUNK)),
                  _resident((N_FF_CHUNKS, FF_CHUNK, D_MODEL)), _resident((1, D_MODEL))],
        out_specs=[pl.BlockSpec((t, D_MODEL), rmap), pl.BlockSpec((1, 2 * g, 2 * D_FF), hmap)],
        out_shape=[jax.ShapeDtypeStruct((n, D_MODEL), F32),
                   jax.ShapeDtypeStruct((nseq, 2 * g, 2 * D_FF), F32)],
        scratch_shapes=[pltpu.VMEM((t, D_MODEL), BF16), pltpu.VMEM((t, D_MODEL), F32),
                        pltpu.VMEM((N_FF_CHUNKS, hp, 2 * FF_CHUNK), F32)],
        compiler_params=_params(_vmem_limit(blk, res + scr, tmp), 2),
        name="convffn",
    )(x2d, gn, halo, wup, wc, wdn, gfin)


def _prep_layer(l, g_mix, w_in, b_igate, b_fgate, g_mhead, a_sinks, w_a, w_b, w_out, g_ffn, w_up,
                w_conv, b_conv, w_down):
    w = w_in[l]
    seg = lambda i: w[:, _OFF[i]:_OFF[i + 1]]
    w_main = jnp.concatenate([seg(1), seg(3), seg(7), seg(9)], axis=1).astype(BF16)
    w_flex = tuple(seg(i).astype(BF16) for i in (0, 2, 6, 8))
    w_gate8 = jnp.concatenate([seg(4), seg(5)], axis=1)
    w_gate = jnp.pad(w_gate8, ((0, 0), (0, LANES_V7X - 2 * M_HEADS))).astype(BF16)
    w_gate_t = jnp.pad(w_gate8.T, ((0, SUBLANES_V7X), (0, 0))).astype(BF16)
    bias8 = jnp.concatenate([b_igate[l], b_fgate[l]]).astype(F32)
    brow = jnp.pad(bias8, (0, LANES_V7X - 2 * M_HEADS)).reshape(1, LANES_V7X)
    bcol = jnp.broadcast_to(bias8[:, None], (SUBLANES_V7X, LANES_V7X))
    sink8 = a_sinks[l].reshape(A_HEADS, 1).astype(F32)
    sinks = jnp.broadcast_to(sink8, (A_HEADS, LANES_V7X))
    sink_rows = jnp.broadcast_to(jnp.repeat(sink8, SAMPLE_PAD_T, axis=0), (A_HEADS * SAMPLE_PAD_T, LANES_V7X))
    up = w_up[l]
    wup = jnp.stack([jnp.concatenate([up[:, j * FF_CHUNK:(j + 1) * FF_CHUNK],
                                      up[:, D_FF + j * FF_CHUNK:D_FF + (j + 1) * FF_CHUNK]], axis=1)
                     for j in range(N_FF_CHUNKS)]).astype(BF16)
    cw = jnp.concatenate([w_conv[l], b_conv[l][None, :]], axis=0)
    cw = jnp.pad(cw, ((0, SUBLANES_V7X - CONV_W - 1), (0, 0))).astype(F32)
    wc = jnp.stack([jnp.concatenate([cw[:, j * FF_CHUNK:(j + 1) * FF_CHUNK],
                                     cw[:, D_FF + j * FF_CHUNK:D_FF + (j + 1) * FF_CHUNK]], axis=1)
                    for j in range(N_FF_CHUNKS)])
    wdn = w_down[l].reshape(N_FF_CHUNKS, FF_CHUNK, D_MODEL).astype(BF16)
    return dict(g_mix=g_mix[l].reshape(1, D_MODEL), w_main=w_main, w_flex=w_flex,
                w_flex_t=tuple(w.T for w in w_flex), w_gate=w_gate, w_gate_t=w_gate_t,
                brow=brow, bcol=bcol, gmh=g_mhead[l].reshape(1, M_WIDTH), sinks=sinks, sink_rows=sink_rows,
                wa=w_a[l].astype(BF16), wb=w_b[l].astype(BF16), wo=w_out[l].astype(BF16),
                g_ffn=g_ffn[l].reshape(1, D_MODEL), wup=wup, wc=wc, wdn=wdn)


def _rope_tables(pos):
    half = A_HEAD_DIM // 2
    inv = ROPE_THETA ** (-jnp.arange(0, A_HEAD_DIM, 2, dtype=F32) / A_HEAD_DIM)
    ang = pos.astype(F32)[:, None] * inv[None, :]
    ang = jnp.tile(jnp.concatenate([ang, ang], axis=-1), (1, LANES_V7X // A_HEAD_DIM))
    sign = jnp.where((jnp.arange(LANES_V7X) % A_HEAD_DIM) < half, -1.0, 1.0).astype(F32)
    return jnp.cos(ang), jnp.sin(ang) * sign[None, :]


def _rope_tables_t(pos):
    inv = ROPE_THETA ** (-jnp.arange(0, A_HEAD_DIM, 2, dtype=F32) / A_HEAD_DIM)
    ang = inv[:, None] * pos.astype(F32)[None, :]
    return jnp.cos(ang), jnp.sin(ang)


def _prompt_trunk(x, layers, g_final):
    B, S, _ = x.shape
    n = B * S
    x2d = x.reshape(n, D_MODEL)
    cos, sin = _rope_tables(jnp.arange(S, dtype=jnp.int32))
    cos_t, sin_t = _rope_tables_t(jnp.arange(S, dtype=jnp.int32))
    nchunks = S // M_CHUNK
    c0 = jnp.zeros((B, 2, LANES_V7X, LANES_V7X), F32)
    n0 = jnp.zeros((B, 2, LANES_V7X), F32)
    m0 = jnp.zeros((B, SUBLANES_V7X, LANES_V7X), F32)
    halo0 = jnp.zeros((B, CONV_W - 1, 2 * D_FF), F32)
    outs = dict(C=[], n=[], m=[], k=[], v=[], conv=[])
    for l, p in enumerate(layers):
        mk, mo, ak, gt, mq_t, mv_t, aq_t, av_t, gc, gr = _inproj(
            x2d, p["g_mix"], p["w_main"], p["w_flex_t"], p["w_gate"], p["w_gate_t"],
            cos, sin, cos_t, sin_t, S // ROW_TILE, True)
        hm, c_new, n_new, m_new = _mlstm_t(mq_t, mk, mv_t, gc, gr, p["brow"], p["bcol"], c0, n0, m0,
                                           nseq=B, nchunks=nchunks)
        ha, v_last = _swa_prompt(aq_t, ak, av_t, p["sinks"], nseq=B, nblk=S // WINDOW)
        x2d = _merge(hm, mo, ha, gt, x2d, p["gmh"], p["wa"], p["wb"], p["wo"])
        x2d, conv = _ffn(x2d, p["g_ffn"], halo0, p["wup"], p["wc"], p["wdn"], g_final,
                         nseq=B, t=FFN_TILE, g=1, final_norm=(l == DEPTH - 1))
        c_new = jnp.transpose(c_new.reshape(B, M_HEADS // 2, M_V_DIM, 2, M_QK_DIM), (0, 1, 3, 4, 2))
        outs["C"].append(c_new.reshape(B, M_HEADS, M_QK_DIM, M_V_DIM))
        outs["n"].append(n_new.reshape(B, M_HEADS, M_QK_DIM))
        outs["m"].append(m_new[:, :M_HEADS, 0])
        outs["k"].append(ak.reshape(B, S, A_KV_HEADS, A_HEAD_DIM)[:, S - WINDOW:])
        outs["v"].append(v_last.reshape(B, WINDOW, A_KV_HEADS, A_HEAD_DIM))
        outs["conv"].append(conv)
    return (x2d.reshape(B, S, D_MODEL),) + tuple(jnp.stack(outs[k]) for k in ("C", "n", "m", "k", "v", "conv"))


def _sample_trunk(x, st_c, st_n, st_m, st_k, st_v, st_conv, layers, g_final):
    B, T, _ = x.shape
    P = SAMPLE_PAD_T
    bt = LANES_V7X // P
    gb = SAMPLE_FFN_GROUP
    ng = B // gb
    pos =PAST_LEN + (jnp.arange(B * P, dtype=jnp.int32) % P)
    cos, sin = _rope_tables(pos)
    c_all = st_c.reshape(DEPTH * B, 2, LANES_V7X, LANES_V7X)
    n_all = st_n.reshape(DEPTH * B, 2, LANES_V7X)
    k_all = st_k.reshape(DEPTH * B, WINDOW, KV_W)
    v_all = st_v.reshape(DEPTH * B, WINDOW, KV_W)
    xpad = jnp.pad(x, ((0, 0), (0, P - T), (0, 0))).reshape(B * P, D_MODEL)
    outs = dict(C=[], n=[], m=[], k=[], v=[], conv=[])
    y_bm = None
    for l, p in enumerate(layers):
        mk, mo, ak, gt, mq, mv, aq, av, gc, gr = _inproj(
            xpad, p["g_mix"], p["w_main"], p["w_flex"], p["w_gate"], p["w_gate_t"],
            cos, sin, cos, sin, (B * P) // ROW_TILE, False)
        m0 = jnp.broadcast_to(jnp.pad(st_m[l], ((0, 0), (0, SUBLANES_V7X - M_HEADS)))[:, :, None],
                              (B, SUBLANES_V7X, LANES_V7X))
        hm, c_new, n_new, m_new = _mlstm(mq, mk, mv, gc, gr, p["brow"], p["bcol"], c_all, n_all, m0,
                                         layer=l, nseq=B, nchunks=1, chunk=P, real=T, bt=bt, mm_dtype=F32)
        q5 = jnp.transpose(aq.reshape(B, P, A_KV_HEADS, A_GROUP, A_HEAD_DIM), (0, 2, 3, 1, 4))
        q5 = q5.reshape(B, A_KV_HEADS, A_GROUP * P, A_HEAD_DIM)
        zq = jnp.zeros_like(q5[:, 0])
        q_rows = jnp.concatenate([jnp.concatenate([q5[:, 0], zq], axis=-1),
                                  jnp.concatenate([zq, q5[:, 1]], axis=-1)], axis=1)
        o_rows, k_new, v_new = _swa_sample(q_rows, ak, av, k_all, v_all, p["sink_rows"],
                                           layer=l, nseq=B, bt=bt, real=T)
        o5 = jnp.stack([o_rows[:, :A_GROUP * P, :A_HEAD_DIM], o_rows[:, A_GROUP * P:, A_HEAD_DIM:]], axis=1)
        ha = jnp.transpose(o5.reshape(B, A_KV_HEADS, A_GROUP, P, A_HEAD_DIM),
                           (0, 3, 1, 2, 4)).reshape(B * P, A_WIDTH)
        xm = _merge(hm, mo, ha, gt, xpad, p["gmh"], p["wa"], p["wb"], p["wo"])
        x_tm = jnp.transpose(xm.reshape(ng, gb, P, D_MODEL)[:, :, :T], (0, 2, 1, 3)).reshape(T * B, D_MODEL)
        halo = jnp.transpose(st_conv[l].reshape(ng, gb, CONV_W - 1, 2 * D_FF),
                             (0, 2, 1, 3)).reshape(ng, (CONV_W - 1) * gb, 2 * D_FF)
        y_tm, conv = _ffn(x_tm, p["g_ffn"], halo, p["wup"], p["wc"], p["wdn"], g_final,
                          nseq=ng, t=T * gb, g=gb, final_norm=(l == DEPTH - 1))
        y_bm = jnp.transpose(y_tm.reshape(ng, T, gb, D_MODEL), (0, 2, 1, 3)).reshape(B, T, D_MODEL)
        xpad = jnp.pad(y_bm, ((0, 0), (0, P - T), (0, 0))).reshape(B * P, D_MODEL)
        outs["C"].append(c_new.reshape(B, M_HEADS, M_QK_DIM, M_V_DIM))
        outs["n"].append(n_new.reshape(B, M_HEADS, M_QK_DIM))
        outs["m"].append(m_new[:, :M_HEADS, 0])
        outs["k"].append(k_new.reshape(B, WINDOW, A_KV_HEADS, A_HEAD_DIM))
        outs["v"].append(v_new.reshape(B, WINDOW, A_KV_HEADS, A_HEAD_DIM))
        outs["conv"].append(jnp.transpose(conv.reshape(ng, CONV_W - 1, gb, 2 * D_FF),
                                          (0, 2, 1, 3)).reshape(B, CONV_W - 1, 2 * D_FF))
    return (y_bm,) + tuple(jnp.stack(outs[k]) for k in ("C", "n", "m", "k", "v", "conv"))


def kernel(x_prompt, x_sample, state_mlstm_C, state_mlstm_n, state_mlstm_m, cache_swa_k, cache_swa_v,
           state_ffn_conv, g_mix, w_in, b_igate, b_fgate, g_mhead, a_sinks, w_a, w_b, w_out, g_ffn,
           w_up, w_conv, b_conv, w_down, g_final):
    layers = [_prep_layer(l, g_mix, w_in, b_igate, b_fgate, g_mhead, a_sinks, w_a, w_b, w_out, g_ffn,
                          w_up, w_conv, b_conv, w_down) for l in range(DEPTH)]
    gfin = g_final.reshape(1, D_MODEL)
    yp, p_c, p_n, p_m, p_k, p_v, p_conv = _prompt_trunk(x_prompt, layers, gfin)
    ys, s_c, s_n, s_m, s_k, s_v, s_conv = _sample_trunk(
        x_sample, state_mlstm_C, state_mlstm_n, state_mlstm_m, cache_swa_k, cache_swa_v,
        state_ffn_conv, layers, gfin)
    return (yp, ys, p_c, p_n, p_m, p_k, p_v, p_conv, s_c, s_n, s_m, s_k, s_v, s_conv)
```

```python
import functools

import numpy as np
import jax
import jax.numpy as jnp
from jax import lax
from jax.experimental import pallas as pl
from jax.experimental.pallas import tpu as pltpu

F32 = jnp.float32
BF16 = jnp.bfloat16

D_MODEL = 1024
DEPTH = 2
PAST_LEN = 16384
M_HEADS = 4
M_QK_DIM = 64
M_V_DIM = 128
M_WIDTH = M_HEADS * M_V_DIM
M_CHUNK = 128
A_HEADS = 8
A_KV_HEADS = 2
A_GROUP = A_HEADS // A_KV_HEADS
A_HEAD_DIM = 64
A_WIDTH = A_HEADS * A_HEAD_DIM
WINDOW = 128
ROPE_THETA = 10000.0
D_FF = 2816
CONV_W = 3
EPS = 1e-6

QK_W = M_HEADS * M_QK_DIM
KV_W = A_KV_HEADS * A_HEAD_DIM
GATE_W = 2 * D_MODEL
_SPLIT = (QK_W, QK_W, M_WIDTH, M_WIDTH, M_HEADS, M_HEADS, A_WIDTH, KV_W, KV_W, GATE_W)
_OFF = np.concatenate([[0], np.cumsum(_SPLIT)]).tolist()
_MAIN_W = (QK_W, M_WIDTH, KV_W, GATE_W)
_MAIN_OFF = np.concatenate([[0], np.cumsum(_MAIN_W)]).tolist()
MAIN_COLS = _MAIN_OFF[-1]

LANES_V7X = 128
SUBLANES_V7X = 8
VMEM_BYTES_V7X = 64 * 1024 * 1024
VMEM_LIMIT_CAP_V7X = 60000 * 1024

SAMPLE_PAD_T = SUBLANES_V7X
FF_CHUNK = 2 * LANES_V7X
N_FF_CHUNKS = D_FF // FF_CHUNK
ROW_TILE = 512
SAMPLE_FFN_GROUP = 32
SWA_QUERY_BLOCKS = 8
FFN_TILE = 1024
FFN_ROWS = 128
INPROJ_ROWS = 256


def _vmem_limit(block_bytes, scratch_bytes, temp_bytes):
    need = 2 * block_bytes + scratch_bytes + temp_bytes
    assert need <= VMEM_LIMIT_CAP_V7X, need
    return VMEM_LIMIT_CAP_V7X


def _params(vmem_bytes, n_axes):
    return pltpu.CompilerParams(dimension_semantics=("arbitrary",) * n_axes,
                                vmem_limit_bytes=vmem_bytes)


def _resident(shape):
    nd = len(shape)
    return pl.BlockSpec(shape, lambda *_: (0,) * nd, pipeline_mode=pl.Buffered(1))


def _log_sigmoid(x):
    return jnp.minimum(x, 0.0) - jnp.log1p(jnp.exp(-jnp.abs(x)))


_NT = (((1,), (1,)), ((), ()))


def _inproj_kernel(x_ref, g_ref, w_ref, wmq_ref, wmv_ref, wq_ref, wv_ref, wg_ref, wgt_ref, cos_ref,
                   sin_ref, cosq_ref, sinq_ref, mk_ref, mo_ref, ak_ref, gt_ref, mq_ref, mv_ref, aq_ref,
                   av_ref, gc_ref, gr_ref, *, tokens_on_lanes):
    t = x_ref.shape[0]
    rb = min(INPROJ_ROWS, t)
    lane = lax.broadcasted_iota(jnp.int32, (rb, LANES_V7X), 1)
    first_half = (lane % A_HEAD_DIM) < (A_HEAD_DIM // 2)
    half = A_HEAD_DIM // 2
    scale = A_HEAD_DIM ** -0.5

    def rope(v, c, s):
        rot = jnp.where(first_half,
                        pltpu.roll(v, LANES_V7X - A_HEAD_DIM // 2, 1),
                        pltpu.roll(v, A_HEAD_DIM // 2, 1))
        return v * c + rot * s

    for r0 in range(0, t, rb):
        rows = slice(r0, r0 + rb)
        x = x_ref[rows, :]
        xn = (x * lax.rsqrt(jnp.mean(x * x, axis=-1, keepdims=True) + EPS) * g_ref[...]).astype(BF16)

        def proj(idx):
            return jnp.dot(xn, w_ref[:, _MAIN_OFF[idx]:_MAIN_OFF[idx + 1]], preferred_element_type=F32)

        def flex(w_r):
            if tokens_on_lanes:
                return lax.dot_general(w_r[...], xn, _NT, preferred_element_type=F32)
            return jnp.dot(xn, w_r[...], preferred_element_type=F32)

        mk, mo, ak, gt = proj(0), proj(1), proj(2), proj(3)
        mq, mv, aq, av = flex(wmq_ref), flex(wmv_ref), flex(wq_ref), flex(wv_ref)
        gc = jnp.dot(xn, wg_ref[...], preferred_element_type=F32)
        gr = lax.dot_general(wgt_ref[...], xn, _NT, preferred_element_type=F32)

        mk_ref[rows, :] = mk
        mo_ref[rows, :] = mo
        gt_ref[rows, :] = gt
        ak_ref[rows, :] = rope(ak, cos_ref[rows, :], sin_ref[rows, :])
        gc_ref[rows, :] = gc
        gr_ref[:, rows] = gr[:SUBLANES_V7X]
        if tokens_on_lanes:
            cq = cosq_ref[:, rows]
            sq = sinq_ref[:, rows]
            mq_ref[:, rows] = mq * (M_QK_DIM ** -0.5)
            mv_ref[:, rows] = mv
            av_ref[:, rows] = av
            for h in range(A_HEADS):
                x1 = aq[h * A_HEAD_DIM:h * A_HEAD_DIM + half]
                x2 = aq[h * A_HEAD_DIM + half:(h + 1) * A_HEAD_DIM]
                aq_ref[h * A_HEAD_DIM:h * A_HEAD_DIM + half, rows] = (x1 * cq - x2 * sq) * scale
                aq_ref[h * A_HEAD_DIM + half:(h + 1) * A_HEAD_DIM, rows] = (x2 * cq + x1 * sq) * scale
        else:
            cq = cosq_ref[rows, :]
            sq = sinq_ref[rows, :]
            mq_ref[rows, :] = mq * (M_QK_DIM ** -0.5)
            mv_ref[rows, :] = mv
            av_ref[rows, :] = av
            for c in range(A_WIDTH // LANES_V7X):
                sl = slice(c * LANES_V7X, (c + 1) * LANES_V7X)
                aq_ref[rows, sl] = rope(aq[:, sl], cq, sq) * scale


_FLEX_W = (QK_W, M_WIDTH, A_WIDTH, KV_W)


def _inproj(x2d, g, w_main, w_flex, w_gate, w_gate_t, cos, sin, cos_q, sin_q, tiles_per_seq,
            tokens_on_lanes):
    n = x2d.shape[0]
    t = min(ROW_TILE, n)
    grid = (n // t,)
    row = lambda w: pl.BlockSpec((t, w), lambda i: (i, 0))
    col = lambda h: pl.BlockSpec((h, t), lambda i: (0, i))
    tab = pl.BlockSpec((t, LANES_V7X), lambda i: (i % tiles_per_seq, 0))
    if tokens_on_lanes:
        qtab = pl.BlockSpec((A_HEAD_DIM // 2, t), lambda i: (0, i % tiles_per_seq))
        flex_specs = [col(w) for w in _FLEX_W]
        flex_shapes = [(w, n) for w in _FLEX_W]
    else:
        qtab = tab
        flex_specs = [row(w) for w in _FLEX_W]
        flex_shapes = [(n, w) for w in _FLEX_W]
    out_shapes = [jax.ShapeDtypeStruct((n, w), F32) for w in _MAIN_W]
    out_shapes += [jax.ShapeDtypeStruct(s, F32) for s in flex_shapes]
    out_shapes += [jax.ShapeDtypeStruct((n, LANES_V7X), F32), jax.ShapeDtypeStruct((SUBLANES_V7X, n), F32)]
    out_specs = [row(w) for w in _MAIN_W] + flex_specs + [row(LANES_V7X), col(SUBLANES_V7X)]
    flex_cols = sum(_FLEX_W)
    blk = 4 * t * (D_MODEL + MAIN_COLS + flex_cols + 5 * LANES_V7X + SUBLANES_V7X)
    res = 2 * D_MODEL * (MAIN_COLS + flex_cols + LANES_V7X + 2 * SUBLANES_V7X)
    tmp = 4 * t * (GATE_W + D_MODEL)
    kern = functools.partial(_inproj_kernel, tokens_on_lanes=tokens_on_lanes)
    return pl.pallas_call(
        kern,
        grid=grid,
        in_specs=[row(D_MODEL), _resident((1, D_MODEL)), _resident((D_MODEL, MAIN_COLS))]
        + [_resident(w.shape) for w in w_flex]
        + [_resident((D_MODEL, LANES_V7X)), _resident((2 * SUBLANES_V7X, D_MODEL)), tab, tab, qtab, qtab],
        out_specs=out_specs,
        out_shape=out_shapes,
        compiler_params=_params(_vmem_limit(blk, res, tmp), 1),
        name="inproj",
    )(x2d, g, w_main, *w_flex, w_gate, w_gate_t, cos, sin, cos_q, sin_q)


def _mlstm_kernel(q_ref, k_ref, v_ref, gc_ref, gr_ref, brow_ref, bcol_ref, c0_ref, n0_ref, m0_ref,
                  h_ref, c_ref, n_ref, m_ref, *, chunk, real, bt, mm_dtype):
    L = chunk
    ci = pl.program_id(1)

    @pl.when(ci == 0)
    def _():
        c_ref[...] = c0_ref[...]
        n_ref[...] = n0_ref[...]
        m_ref[...] = m0_ref[...]

    def mx(a):
        return a.astype(mm_dtype)

    def split3(a):
        hi = a.astype(BF16).astype(F32)
        r1 = a - hi
        mid = r1.astype(BF16).astype(F32)
        lo = (r1 - mid).astype(BF16).astype(F32)
        return hi, mid, lo

    ti = lax.broadcasted_iota(jnp.int32, (L, L), 0)
    si = lax.broadcasted_iota(jnp.int32, (L, L), 1)
    causal = si <= ti
    tri = causal.astype(F32)
    tri_t = (ti <= si).astype(F32)
    lane_g = lax.broadcasted_iota(jnp.int32, (L, LANES_V7X), 1)
    row_g = lax.broadcasted_iota(jnp.int32, (L, LANES_V7X), 0)
    is_f_col = (lane_g >= M_HEADS) & (lane_g < 2 * M_HEADS)
    row_r = lax.broadcasted_iota(jnp.int32, (SUBLANES_V7X, L), 0)
    col_r = lax.broadcasted_iota(jnp.int32, (SUBLANES_V7X, L), 1)
    is_f_row = row_r >= M_HEADS
    lane_p = lax.broadcasted_iota(jnp.int32, (L, LANES_V7X), 1)
    half_masks = (lane_p < M_QK_DIM, lane_p >= M_QK_DIM)
    rows_c = lax.broadcasted_iota(jnp.int32, (2 * M_QK_DIM, 1), 0)
    lanes_n = lax.broadcasted_iota(jnp.int32, (1, LANES_V7X), 1)
    neg_inf = jnp.float32(-jnp.inf)

    units = [(bi, h) for bi in range(bt) for h in range(M_HEADS)]
    gates = []
    for bi in range(bt):
        rs = slice(bi * L, (bi + 1) * L)
        gcol = gc_ref[rs, :] + brow_ref[...]
        gcol = jnp.where(is_f_col, _log_sigmoid(gcol), gcol)
        fcol = jnp.where(is_f_col, gcol, 0.0)
        grow = gr_ref[:, rs] + bcol_ref[:, :L]
        grow = jnp.where(is_f_row, _log_sigmoid(grow), grow)
        frow = jnp.where(is_f_row, grow, 0.0)
        if real < L:
            gcol = jnp.where(row_g < real, gcol, neg_inf)
            fcol = jnp.where(row_g < real, fcol, 0.0)
            grow = jnp.where(col_r < real, grow, neg_inf)
            frow = jnp.where(col_r < real, frow, 0.0)
        cum_c = sum(jnp.dot(tri, p, preferred_element_type=F32) for p in split3(fcol))
        cum_r = sum(jnp.dot(p, tri_t, preferred_element_type=F32) for p in split3(frow))
        gates.append((gcol, grow, cum_c, cum_r))

    qhs, kps, vhs, s_raw, q_c = {}, {}, {}, {}, {}
    for bi, h in units:
        rs = slice(bi * L, (bi + 1) * L)
        pair, half = divmod(h, 2)
        ps = slice(pair * LANES_V7X, (pair + 1) * LANES_V7X)
        kp = k_ref[rs, ps]
        qh = jnp.where(half_masks[half], q_ref[rs, ps], 0.0)
        qhs[bi, h], kps[bi, h] = qh, kp
        vhs[bi, h] = v_ref[rs, h * M_V_DIM:(h + 1) * M_V_DIM]
        s_raw[bi, h] = lax.dot_general(mx(qh), mx(kp), _NT, preferred_element_type=F32)
        q_c[bi, h] = jnp.dot(mx(qh), mx(c_ref[bi, pair]), preferred_element_type=F32)

    s_w, kws, iscales, dens_, mrows, decays, new_m = {}, {}, {}, {}, {}, {}, {}
    for bi, h in units:
        gcol, grow, cum_c, cum_r = gates[bi]
        pair, half = divmod(h, 2)
        b_col = cum_c[:, M_HEADS + h:M_HEADS + h + 1]
        li_col = gcol[:, h:h + 1]
        b_row = cum_r[M_HEADS + h:M_HEADS + h + 1, :]
        li_row = grow[h:h + 1, :]
        m0 = m_ref[bi, h:h + 1, 0:1]
        npair = n_ref[bi, pair:pair + 1, :]
        dmat = jnp.where(causal, b_col - b_row + li_row, neg_inf)
        inter = b_col + m0
        mrow = jnp.maximum(inter, jnp.max(dmat, axis=-1, keepdims=True))
        iscale = jnp.exp(inter - mrow)
        s = s_raw[bi, h] * jnp.exp(dmat - mrow)
        qn = jnp.sum(qhs[bi, h] * npair, axis=-1, keepdims=True)
        dens_[bi, h] = iscale * qn + jnp.sum(s, axis=-1, keepdims=True)
        m_new = mrow[L - 1:L, :]
        b_last = b_col[L - 1:L, :]
        w_col = jnp.exp(b_last - b_col + li_col - m_new)
        decays[bi, h] = jnp.exp(b_last + m0 - m_new)
        kws[bi, h] = jnp.where(half_masks[half], kps[bi, h] * w_col, 0.0)
        s_w[bi, h], iscales[bi, h], mrows[bi, h], new_m[bi, h] = s, iscale, mrow, m_new

    d_c = {}
    for bi, h in units:
        rs = slice(bi * L, (bi + 1) * L)
        vh = vhs[bi, h]
        num = iscales[bi, h] * q_c[bi, h] + jnp.dot(mx(s_w[bi, h]), mx(vh), preferred_element_type=F32)
        h_ref[rs, h * M_V_DIM:(h + 1) * M_V_DIM] = \
            num / jnp.maximum(jnp.abs(dens_[bi, h]), jnp.exp(-mrows[bi, h]))
        d_c[bi, h] = lax.dot_general(mx(kws[bi, h]), mx(vh), (((0,), (0,)), ((), ())),
                                     preferred_element_type=F32)

    for bi in range(bt):
        for pair in range(M_HEADS // 2):
            h0, h1 = 2 * pair, 2 * pair + 1
            npair = n_ref[bi, pair:pair + 1, :]
            d_n = jnp.sum(kws[bi, h0], axis=0, keepdims=True) + jnp.sum(kws[bi, h1], axis=0, keepdims=True)
            c_ref[bi, pair] = jnp.where(rows_c < M_QK_DIM, decays[bi, h0], decays[bi, h1]) * c_ref[bi, pair] \
                + d_c[bi, h0] + d_c[bi, h1]
            n_ref[bi, pair:pair + 1, :] = \
                jnp.where(lanes_n < M_QK_DIM, decays[bi, h0], decays[bi, h1]) * npair + d_n
        for h in range(M_HEADS):
            m_ref[bi, h:h + 1, :] = jnp.broadcast_to(new_m[bi, h], (1, LANES_V7X))


def _mlstm(q, k, v, gc, gr, brow, bcol, c0, n0, m0, *, layer, nseq, nchunks, chunk, real, bt, mm_dtype):
    rows = bt * chunk
    grid = (nseq // bt, nchunks)
    nb = nseq // bt
    rmap = lambda b, c: (b * nchunks + c, 0)
    smap3 = lambda b, c: (b, 0, 0)
    smap4 = lambda b, c: (b, 0, 0, 0)
    c_spec = pl.BlockSpec((bt, 2, LANES_V7X, LANES_V7X), smap4)
    n_spec = pl.BlockSpec((bt, 2, LANES_V7X), smap3)
    m_spec = pl.BlockSpec((bt, SUBLANES_V7X, LANES_V7X), smap3)
    c_in = pl.BlockSpec((bt, 2, LANES_V7X, LANES_V7X), lambda b, c: (layer * nb + b, 0, 0, 0))
    n_in = pl.BlockSpec((bt, 2, LANES_V7X), lambda b, c: (layer * nb + b, 0, 0))
    n = q.shape[0]
    blk = 4 * (rows * (2 * QK_W + 2 * M_WIDTH + LANES_V7X + SUBLANES_V7X)
               + 2 * bt * (2 * LANES_V7X * LANES_V7X + 2 * LANES_V7X + SUBLANES_V7X * LANES_V7X))
    tmp = 4 * 16 * chunk * max(chunk, LANES_V7X)
    kern = functools.partial(_mlstm_kernel, chunk=chunk, real=real, bt=bt, mm_dtype=mm_dtype)
    return pl.pallas_call(
        kern,
        grid=grid,
        in_specs=[pl.BlockSpec((rows, QK_W), rmap), pl.BlockSpec((rows, QK_W), rmap),
                  pl.BlockSpec((rows, M_WIDTH), rmap), pl.BlockSpec((rows, LANES_V7X), rmap),
                  pl.BlockSpec((SUBLANES_V7X, rows), lambda b, c: (0, b * nchunks + c)),
                  _resident((1, LANES_V7X)), _resident((SUBLANES_V7X, LANES_V7X)),
                  c_in, n_in, m_spec],
        out_specs=[pl.BlockSpec((rows, M_WIDTH), rmap), c_spec, n_spec, m_spec],
        out_shape=[jax.ShapeDtypeStruct((n, M_WIDTH), F32),
                   jax.ShapeDtypeStruct((nseq, 2, LANES_V7X, LANES_V7X), F32),
                   jax.ShapeDtypeStruct((nseq, 2, LANES_V7X), F32),
                   jax.ShapeDtypeStruct((nseq, SUBLANES_V7X, LANES_V7X), F32)],
        compiler_params=_params(_vmem_limit(blk, 0, tmp), 2),
        name="mlstm",
    )(q, k, v, gc, gr, brow, bcol, c0, n0, m0)


def _mlstm_t_kernel(qt_ref, k_ref, vt_ref, gc_ref, gr_ref, brow_ref, bcol_ref, c0_ref, n0_ref, m0_ref,
                    h_ref, c_ref, n_ref, m_ref):
    L = M_CHUNK
    ci = pl.program_id(1)

    @pl.when(ci == 0)
    def _():
        c_ref[...] = c0_ref[...]
        n_ref[...] = n0_ref[...]
        m_ref[...] = m0_ref[...]

    def split3(a):
        hi = a.astype(BF16).astype(F32)
        r1 = a - hi
        mid = r1.astype(BF16).astype(F32)
        lo = (r1 - mid).astype(BF16).astype(F32)
        return hi, mid, lo

    si = lax.broadcasted_iota(jnp.int32, (L, L), 0)
    ti = lax.broadcasted_iota(jnp.int32, (L, L), 1)
    causal = si <= ti
    tri = (ti <= si).astype(F32)
    tri_t = causal.astype(F32)
    lane_g = lax.broadcasted_iota(jnp.int32, (L, LANES_V7X), 1)
    is_f_col = (lane_g >= M_HEADS) & (lane_g < 2 * M_HEADS)
    row_r = lax.broadcasted_iota(jnp.int32, (SUBLANES_V7X, L), 0)
    is_f_row = row_r >= M_HEADS
    half_masks = (lane_g < M_QK_DIM, lane_g >= M_QK_DIM)
    lanes_n = lax.broadcasted_iota(jnp.int32, (1, LANES_V7X), 1)
    neg_inf = jnp.float32(-jnp.inf)
    PADR = 2 * SUBLANES_V7X

    gcol = gc_ref[...] + brow_ref[...]
    gcol = jnp.where(is_f_col, _log_sigmoid(gcol), gcol)
    grow = gr_ref[...] + bcol_ref[...]
    grow = jnp.where(is_f_row, _log_sigmoid(grow), grow)
    cum_c = sum(jnp.dot(tri, p, preferred_element_type=F32)
                for p in split3(jnp.where(is_f_col, gcol, 0.0)))
    cum_r = sum(jnp.dot(p, tri_t, preferred_element_type=F32)
                for p in split3(jnp.where(is_f_row, grow, 0.0)))

    zq = jnp.zeros((M_QK_DIM, L), BF16)
    zpad = jnp.zeros((PADR - 1, LANES_V7X), F32)
    kps, s_raw, inter = [], [], []
    for h in range(M_HEADS):
        pair, half = divmod(h, 2)
        kp = k_ref[:, pair * LANES_V7X:(pair + 1) * LANES_V7X]
        qh = qt_ref[h * M_QK_DIM:(h + 1) * M_QK_DIM, :].astype(BF16)
        qz = jnp.concatenate([qh, zq] if half == 0 else [zq, qh], axis=0)
        caug = jnp.concatenate([c_ref[0, pair], n_ref[0, pair:pair + 1, :], zpad], axis=0)
        kps.append(kp)
        s_raw.append(jnp.dot(kp.astype(BF16), qz, preferred_element_type=F32))
        inter.append(jnp.dot(caug.astype(BF16), qz, preferred_element_type=F32))
    s_w, iscales, dens_, m_rows, w_rows, decays = [], [], [], [], [], []
    for h in range(M_HEADS):
        b_row = cum_r[M_HEADS + h:M_HEADS + h + 1, :]
        li_row = grow[h:h + 1, :]
        d_col = gcol[:, h:h + 1] - cum_c[:, M_HEADS + h:M_HEADS + h + 1]
        m0 = m_ref[0, h:h + 1, 0:1]
        dmat = jnp.where(causal, b_row + d_col, neg_inf)
        inter_m = b_row + m0
        m_row = jnp.maximum(inter_m, jnp.max(dmat, axis=0, keepdims=True))
        s = s_raw[h] * jnp.exp(dmat - m_row)
        iscale = jnp.exp(inter_m - m_row)
        dens_.append(iscale * inter[h][M_V_DIM:M_V_DIM + 1, :] + jnp.sum(s, axis=0, keepdims=True))
        m_new = m_row[:, L - 1:L]
        b_last = b_row[:, L - 1:L]
        w_rows.append(jnp.exp(b_last - b_row + li_row - m_new))
        decays.append(jnp.exp(b_last + m0 - m_new))
        s_w.append(s.astype(BF16))
        iscales.append(iscale)
        m_rows.append(m_row)
    outs, upds = [], []
    for h in range(M_HEADS):
        pair, half = divmod(h, 2)
        vt = vt_ref[h * M_V_DIM:(h + 1) * M_V_DIM, :]
        num = iscales[h] * inter[h][:M_V_DIM, :] + jnp.dot(vt.astype(BF16), s_w[h], preferred_element_type=F32)
        outs.append(num / jnp.maximum(jnp.abs(dens_[h]), jnp.exp(-m_rows[h])))
        vaug = jnp.concatenate([vt * w_rows[h], w_rows[h], jnp.zeros((PADR - 1, L), F32)], axis=0)
        kmask = jnp.where(half_masks[half], kps[h], 0.0)
        upds.append(jnp.dot(vaug.astype(BF16), kmask.astype(BF16), preferred_element_type=F32))
    for pair in range(M_HEADS // 2):
        dl = jnp.where(lanes_n < M_QK_DIM, decays[2 * pair], decays[2 * pair + 1])
        u0, u1 = upds[2 * pair], upds[2 * pair + 1]
        n_old = n_ref[0, pair:pair + 1, :]
        c_ref[0, pair] = dl * c_ref[0, pair] + u0[:M_V_DIM] + u1[:M_V_DIM]
        n_ref[0, pair:pair + 1, :] = dl * n_old + u0[M_V_DIM:M_V_DIM + 1] + u1[M_V_DIM:M_V_DIM + 1]
    for h in range(M_HEADS):
        m_ref[0, h:h + 1, :] = jnp.broadcast_to(m_rows[h][:, L - 1:L], (1, LANES_V7X))
    h_ref[...] = jnp.concatenate(outs, axis=0).T


def _mlstm_t(qt, k, vt, gc, gr, brow, bcol, c0, n0, m0, *, nseq, nchunks):
    L = M_CHUNK
    n = k.shape[0]
    rmap = lambda b, c: (b * nchunks + c, 0)
    cmap = lambda b, c: (0, b * nchunks + c)
    smap3 = lambda b, c: (b, 0, 0)
    smap4 = lambda b, c: (b, 0, 0, 0)
    c_spec = pl.BlockSpec((1, 2, LANES_V7X, LANES_V7X), smap4)
    n_spec = pl.BlockSpec((1, 2, LANES_V7X), smap3)
    m_spec = pl.BlockSpec((1, SUBLANES_V7X, LANES_V7X), smap3)
    blk = 4 * (L * (2 * QK_W + 2 * M_WIDTH + LANES_V7X + SUBLANES_V7X)
               + 2 * (2 * LANES_V7X * LANES_V7X + 2 * LANES_V7X + SUBLANES_V7X * LANES_V7X))
    tmp = 4 * 48 * L * L
    return pl.pallas_call(
        _mlstm_t_kernel,
        grid=(nseq, nchunks),
        in_specs=[pl.BlockSpec((QK_W, L), cmap), pl.BlockSpec((L, QK_W), rmap),
                  pl.BlockSpec((M_WIDTH, L), cmap), pl.BlockSpec((L, LANES_V7X), rmap),
                  pl.BlockSpec((SUBLANES_V7X, L), cmap),
                  _resident((1, LANES_V7X)), _resident((SUBLANES_V7X, LANES_V7X)),
                  c_spec, n_spec, m_spec],
        out_specs=[pl.BlockSpec((L, M_WIDTH), rmap), c_spec, n_spec, m_spec],
        out_shape=[jax.ShapeDtypeStruct((n, M_WIDTH), F32),
                   jax.ShapeDtypeStruct((nseq, 2, LANES_V7X, LANES_V7X), F32),
                   jax.ShapeDtypeStruct((nseq, 2, LANES_V7X), F32),
                   jax.ShapeDtypeStruct((nseq, SUBLANES_V7X, LANES_V7X), F32)],
        compiler_params=_params(_vmem_limit(blk, 0, tmp), 2),
        name="mlstm_t",
    )(qt, k, vt, gc, gr, brow, bcol, c0, n0, m0)


def _swa_prompt_kernel(qt_ref, kc_ref, kp_ref, vtc_ref, vtp_ref, sink_ref, o_ref, vlast_ref):
    blk = pl.program_id(1)
    L = WINDOW
    nq = kc_ref.shape[0] // L

    @pl.when(blk == pl.num_programs(1) - 1)
    def _():
        vlast_ref[0] = vtc_ref[:, (nq - 1) * L:nq * L].T

    key = lax.broadcasted_iota(jnp.int32, (2 * L, L), 0)
    qry = lax.broadcasted_iota(jnp.int32, (2 * L, L), 1)
    prev_off = jnp.where(blk > 0, 0, L)
    band = (key >= L) & (key - L <= qry)
    masks = [((key < L) & (key >= qry + prev_off)) | band] + [((key < L) & (key >= qry)) | band] * (nq - 1)
    k_all = jnp.concatenate([kp_ref[...], kc_ref[...]], axis=0).astype(BF16)
    vt_all = jnp.concatenate([vtp_ref[...], vtc_ref[...]], axis=1).astype(BF16)
    zeros = jnp.zeros((A_HEAD_DIM, L), BF16)
    units = [(qb, hd) for qb in range(nq) for hd in range(A_HEADS)]
    scores = {}
    for qb, hd in units:
        qh = qt_ref[hd * A_HEAD_DIM:(hd + 1) * A_HEAD_DIM, qb * L:(qb + 1) * L].astype(BF16)
        qz = jnp.concatenate([qh, zeros] if hd < A_GROUP else [zeros, qh], axis=0)
        scores[qb, hd] = jnp.dot(k_all[qb * L:(qb + 2) * L], qz, preferred_element_type=F32)
    probs = {}
    dens = {}
    for qb, hd in units:
        s = jnp.where(masks[qb], scores[qb, hd], -jnp.inf)
        sink = sink_ref[hd:hd + 1, 0:1]
        m = jnp.maximum(jnp.max(s, axis=0, keepdims=True), sink)
        p = jnp.exp(s - m)
        dens[qb, hd] = jnp.sum(p, axis=0, keepdims=True) + jnp.exp(sink - m)
        probs[qb, hd] = p.astype(BF16)
    for qb in range(nq):
        outs = []
        for hd in range(A_HEADS):
            kv = hd // A_GROUP
            o = jnp.dot(vt_all[kv * A_HEAD_DIM:(kv + 1) * A_HEAD_DIM, qb * L:(qb + 2) * L], probs[qb, hd],
                        preferred_element_type=F32)
            outs.append(o / dens[qb, hd])
        o_ref[qb * L:(qb + 1) * L, :] = jnp.concatenate(outs, axis=0).T


def _swa_prompt(qt, k, vt, sinks, *, nseq, nblk):
    n = k.shape[0]
    L = WINDOW
    nq = SWA_QUERY_BLOCKS
    steps = nblk // nq
    cur = lambda b, i: (b * steps + i, 0)
    prev = lambda b, i: (b * nblk + jnp.maximum(nq * i - 1, 0), 0)
    cur_t = lambda b, i: (0, b * steps + i)
    prev_t = lambda b, i: (0, b * nblk + jnp.maximum(nq * i - 1, 0))
    blk = 4 * L * (2 * nq * A_WIDTH + 2 * (nq + 1) * KV_W)
    tmp = 4 * 24 * 2 * L * L * nq
    return pl.pallas_call(
        _swa_prompt_kernel,
        grid=(nseq, steps),
        in_specs=[pl.BlockSpec((A_WIDTH, nq * L), cur_t),
                  pl.BlockSpec((nq * L, KV_W), cur), pl.BlockSpec((L, KV_W), prev),
                  pl.BlockSpec((KV_W, nq * L), cur_t), pl.BlockSpec((KV_W, L), prev_t),
                  _resident((A_HEADS, LANES_V7X))],
        out_specs=[pl.BlockSpec((nq * L, A_WIDTH), cur), pl.BlockSpec((1, L, KV_W), lambda b, i: (b, 0, 0))],
        out_shape=[jax.ShapeDtypeStruct((n, A_WIDTH), F32), jax.ShapeDtypeStruct((nseq, L, KV_W), F32)],
        compiler_params=_params(_vmem_limit(blk, 0, tmp), 2),
        name="swa_prompt",
    )(qt, k, k, vt, vt, sinks)


def _swa_sample_kernel(q_ref, kn_ref, vn_ref, kc_ref, vc_ref, sink_ref, o_ref, ko_ref, vo_ref,
                       *, bt, real):
    P = SAMPLE_PAD_T
    W = WINDOW
    R = A_HEADS * P
    row = lax.broadcasted_iota(jnp.int32, (R, 2 * W), 0)
    key = lax.broadcasted_iota(jnp.int32, (R, 2 * W), 1)
    step = row % P
    mask = ((key < W) & (key >= step)) | ((key >= W) & (key - W <= step) & (key - W < real))
    sink = sink_ref[:, 0:1]
    pad = jnp.zeros((W - P, KV_W), F32)
    scores = []
    for bi in range(bt):
        kext = jnp.concatenate([kc_ref[bi], kn_ref[bi * P:(bi + 1) * P, :], pad], axis=0)
        scores.append(lax.dot_general(q_ref[bi], kext, _NT, preferred_element_type=F32))
    probs = []
    dens = []
    for bi in range(bt):
        s = jnp.where(mask, scores[bi], -jnp.inf)
        m = jnp.maximum(jnp.max(s, axis=-1, keepdims=True), sink)
        p = jnp.exp(s - m)
        dens.append(jnp.sum(p, axis=-1, keepdims=True) + jnp.exp(sink - m))
        probs.append(p)
    for bi in range(bt):
        rs = slice(bi * P, (bi + 1) * P)
        vn = vn_ref[rs, :]
        vext = jnp.concatenate([vc_ref[bi], vn, pad], axis=0)
        o_ref[bi] = jnp.dot(probs[bi], vext, preferred_element_type=F32) / dens[bi]
        ko_ref[bi, 0:W - real, :] = kc_ref[bi, real:W, :]
        ko_ref[bi, W - real:W, :] = kn_ref[bi * P:bi * P + real, :]
        vo_ref[bi, 0:W - real, :] = vc_ref[bi, real:W, :]
        vo_ref[bi, W - real:W, :] = vn[0:real, :]


def _swa_sample(q_rows, kn, vn, kcache, vcache, sinks, *, layer, nseq, bt, real):
    rows = bt * SAMPLE_PAD_T
    R = A_HEADS * SAMPLE_PAD_T
    nb = nseq // bt
    rmap = lambda i: (i, 0)
    cmap = lambda i: (layer * nb + i, 0, 0)
    omap = lambda i: (i, 0, 0)
    cache = pl.BlockSpec((bt, WINDOW, KV_W), cmap)
    blk = 4 * (2 * bt * R * KV_W + 2 * rows * KV_W + 4 * bt * WINDOW * KV_W)
    tmp = 4 * 16 * R * 2 * WINDOW
    kern = functools.partial(_swa_sample_kernel, bt=bt, real=real)
    return pl.pallas_call(
        kern,
        grid=(nb,),
        in_specs=[pl.BlockSpec((bt, R, KV_W), omap), pl.BlockSpec((rows, KV_W), rmap),
                  pl.BlockSpec((rows, KV_W), rmap), cache, cache, _resident((R, LANES_V7X))],
        out_specs=[pl.BlockSpec((bt, R, KV_W), omap),
                   pl.BlockSpec((bt, WINDOW, KV_W), omap), pl.BlockSpec((bt, WINDOW, KV_W), omap)],
        out_shape=[jax.ShapeDtypeStruct((nseq, R, KV_W), F32),
                   jax.ShapeDtypeStruct((nseq, WINDOW, KV_W), F32),
                   jax.ShapeDtypeStruct((nseq, WINDOW, KV_W), F32)],
        compiler_params=_params(_vmem_limit(blk, 0, tmp), 1),
        name="swa_sample",
    )(q_rows, kn, vn, kcache, vcache, sinks)


def _merge_kernel(hm_ref, mo_ref, ha_ref, gt_ref, x_ref, gmh_ref, wa_ref, wb_ref, wo_ref, o_ref):
    hm = hm_ref[...]
    parts = []
    for h in range(M_HEADS):
        v = hm[:, h * M_V_DIM:(h + 1) * M_V_DIM]
        parts.append(v * lax.rsqrt(jnp.mean(v * v, axis=-1, keepdims=True) + EPS))
    hn = jnp.concatenate(parts, axis=1) * gmh_ref[...] * jax.nn.sigmoid(mo_ref[...])
    a = jnp.dot(hn.astype(BF16), wa_ref[...], preferred_element_type=F32)
    b = jnp.dot(ha_ref[...].astype(BF16), wb_ref[...], preferred_element_type=F32)
    gates = jax.nn.sigmoid(gt_ref[...])
    mixed = gates[:, :D_MODEL] * a + gates[:, D_MODEL:] * b
    o_ref[...] = x_ref[...] + jnp.dot(mixed.astype(BF16), wo_ref[...], preferred_element_type=F32)


def _merge(hm, mo, ha, gt, x2d, gmh, wa, wb, wo):
    n = x2d.shape[0]
    t = min(ROW_TILE, n)
    row = lambda w: pl.BlockSpec((t, w), lambda i: (i, 0))
    blk = 4 * t * (3 * M_WIDTH + GATE_W + 2 * D_MODEL)
    res = 2 * (2 * M_WIDTH * D_MODEL + D_MODEL * D_MODEL)
    tmp = 4 * t * (GATE_W + 3 * D_MODEL)
    return pl.pallas_call(
        _merge_kernel,
        grid=(n // t,),
        in_specs=[row(M_WIDTH), row(M_WIDTH), row(A_WIDTH), row(GATE_W), row(D_MODEL),
                  _resident((1, M_WIDTH)), _resident((M_WIDTH, D_MODEL)),
                  _resident((A_WIDTH, D_MODEL)), _resident((D_MODEL, D_MODEL))],
        out_specs=row(D_MODEL),
        out_shape=jax.ShapeDtypeStruct((n, D_MODEL), F32),
        compiler_params=_params(_vmem_limit(blk, res, tmp), 1),
        name="merge",
    )(hm, mo, ha, gt, x2d, gmh, wa, wb, wo)


def _ffn_kernel(x_ref, gn_ref, halo_ref, wup_ref, wc_ref, wdn_ref, gfin_ref, o_ref, halo_out_ref,
                xn_s, acc_s, halo_s, *, g, final_norm):
    t = x_ref.shape[0]
    rb = min(FFN_ROWS, t)
    hp = halo_s.shape[1]
    ti = pl.program_id(1)
    last = pl.num_programs(1) - 1

    @pl.when(ti == 0)
    def _():
        halo_s[...] = jnp.zeros_like(halo_s)
        for j in range(N_FF_CHUNKS):
            halo_s[j, hp - 2 * g:hp, :FF_CHUNK] = halo_ref[0, :, j * FF_CHUNK:(j + 1) * FF_CHUNK]
            halo_s[j, hp - 2 * g:hp, FF_CHUNK:] = halo_ref[0, :, D_FF + j * FF_CHUNK:D_FF + (j + 1) * FF_CHUNK]

    x = x_ref[...]
    xn_s[...] = (x * lax.rsqrt(jnp.mean(x * x, axis=-1, keepdims=True) + EPS) * gn_ref[...]).astype(BF16)

    for j in range(N_FF_CHUNKS):
        w_up = wup_ref[j]
        us = [jnp.dot(xn_s[r:r + rb, :], w_up, preferred_element_type=F32) for r in range(0, t, rb)]
        wc = wc_ref[j]
        prev = halo_s[j]
        acts = []
        for u in us:
            ext = jnp.concatenate([prev, u], axis=0)
            u1 = ext[hp - g:hp - g + rb]
            u2 = ext[hp - 2 * g:hp - 2 * g + rb]
            prev = ext[rb:rb + hp]
            c = wc[3:4, :] + wc[0:1, :] * u2 + wc[1:2, :] * u1 + wc[2:3, :] * u
            a = c[:, :FF_CHUNK]
            act = 0.5 * a * (1.0 + lax.erf(a * np.float32(np.sqrt(0.5)))) * c[:, FF_CHUNK:]
            acts.append(act.astype(BF16))
        halo_s[j] = prev
        w_dn = wdn_ref[j]
        for i, act in enumerate(acts):
            d = jnp.dot(act, w_dn, preferred_element_type=F32)
            if j == 0:
                acc_s[i * rb:(i + 1) * rb, :] = d
            else:
                acc_s[i * rb:(i + 1) * rb, :] += d

    y = x + acc_s[...]
    if final_norm:
        y = y * lax.rsqrt(jnp.mean(y * y, axis=-1, keepdims=True) + EPS) * gfin_ref[...]
    o_ref[...] = y

    @pl.when(ti == last)
    def _():
        for j in range(N_FF_CHUNKS):
            halo_out_ref[0, :, j * FF_CHUNK:(j + 1) * FF_CHUNK] = halo_s[j, hp - 2 * g:hp, :FF_CHUNK]
            halo_out_ref[0, :, D_FF + j * FF_CHUNK:D_FF + (j + 1) * FF_CHUNK] = halo_s[j, hp - 2 * g:hp, FF_CHUNK:]


def _ffn(x2d, gn, halo, wup, wc, wdn, gfin, *, nseq, t, g, final_norm):
    n = x2d.shape[0]
    tiles = n // (nseq * t)
    hp = -(-2 * g // SUBLANES_V7X) * SUBLANES_V7X
    rmap = lambda s, i: (s * tiles + i, 0)
    hmap = lambda s, i: (s, 0, 0)
    blk = 4 * (2 * t * D_MODEL + 2 * 2 * g * 2 * D_FF)
    res = 2 * (D_MODEL * 2 * D_FF + D_FF * D_MODEL) + 4 * N_FF_CHUNKS * SUBLANES_V7X * 2 * FF_CHUNK
    scr = 2 * t * D_MODEL + 4 * t * D_MODEL + 4 * N_FF_CHUNKS * hp * 2 * FF_CHUNK
    tmp = 4 * t * (8 * FF_CHUNK + D_MODEL)
    kern = functools.partial(_ffn_kernel, g=g, final_norm=final_norm)
    return pl.pallas_call(
        kern,
        grid=(nseq, tiles),
        in_specs=[pl.BlockSpec((t, D_MODEL), rmap), _resident((1, D_MODEL)),
                  pl.BlockSpec((1, 2 * g, 2 * D_FF), hmap),
                  _resident((N_FF_CHUNKS, D_MODEL, 2 * FF_CHUNK)),
                  _resident((N_FF_CHUNKS, SUBLANES_V7X, 2 * FF_CHUNK)),
                  _resident((N_FF_CHUNKS, FF_CHUNK, D_MODEL)), _resident((1, D_MODEL))],
        out_specs=[pl.BlockSpec((t, D_MODEL), rmap), pl.BlockSpec((1, 2 * g, 2 * D_FF), hmap)],
        out_shape=[jax.ShapeDtypeStruct((n, D_MODEL), F32),
                   jax.ShapeDtypeStruct((nseq, 2 * g, 2 * D_FF), F32)],
        scratch_shapes=[pltpu.VMEM((t, D_MODEL), BF16), pltpu.VMEM((t, D_MODEL), F32),
                        pltpu.VMEM((N_FF_CHUNKS, hp, 2 * FF_CHUNK), F32)],
        compiler_params=_params(_vmem_limit(blk, res + scr, tmp), 2),
        name="convffn",
    )(x2d, gn, halo, wup, wc, wdn, gfin)


def _prep_layer(l, g_mix, w_in, b_igate, b_fgate, g_mhead, a_sinks, w_a, w_b, w_out, g_ffn, w_up,
                w_conv, b_conv, w_down):
    w = w_in[l]
    seg = lambda i: w[:, _OFF[i]:_OFF[i + 1]]
    w_main = jnp.concatenate([seg(1), seg(3), seg(7), seg(9)], axis=1).astype(BF16)
    w_flex = tuple(seg(i).astype(BF16) for i in (0, 2, 6, 8))
    w_gate8 = jnp.concatenate([seg(4), seg(5)], axis=1)
    w_gate = jnp.pad(w_gate8, ((0, 0), (0, LANES_V7X - 2 * M_HEADS))).astype(BF16)
    w_gate_t = jnp.pad(w_gate8.T, ((0, SUBLANES_V7X), (0, 0))).astype(BF16)
    bias8 = jnp.concatenate([b_igate[l], b_fgate[l]]).astype(F32)
    brow = jnp.pad(bias8, (0, LANES_V7X - 2 * M_HEADS)).reshape(1, LANES_V7X)
    bcol = jnp.broadcast_to(bias8[:, None], (SUBLANES_V7X, LANES_V7X))
    sink8 = a_sinks[l].reshape(A_HEADS, 1).astype(F32)
    sinks = jnp.broadcast_to(sink8, (A_HEADS, LANES_V7X))
    sink_rows = jnp.broadcast_to(jnp.repeat(sink8, SAMPLE_PAD_T, axis=0), (A_HEADS * SAMPLE_PAD_T, LANES_V7X))
    up = w_up[l]
    wup = jnp.stack([jnp.concatenate([up[:, j * FF_CHUNK:(j + 1) * FF_CHUNK],
                                      up[:, D_FF + j * FF_CHUNK:D_FF + (j + 1) * FF_CHUNK]], axis=1)
                     for j in range(N_FF_CHUNKS)]).astype(BF16)
    cw = jnp.concatenate([w_conv[l], b_conv[l][None, :]], axis=0)
    cw = jnp.pad(cw, ((0, SUBLANES_V7X - CONV_W - 1), (0, 0))).astype(F32)
    wc = jnp.stack([jnp.concatenate([cw[:, j * FF_CHUNK:(j + 1) * FF_CHUNK],
                                     cw[:, D_FF + j * FF_CHUNK:D_FF + (j + 1) * FF_CHUNK]], axis=1)
                    for j in range(N_FF_CHUNKS)])
    wdn = w_down[l].reshape(N_FF_CHUNKS, FF_CHUNK, D_MODEL).astype(BF16)
    return dict(g_mix=g_mix[l].reshape(1, D_MODEL), w_main=w_main, w_flex=w_flex,
                w_flex_t=tuple(w.T for w in w_flex), w_gate=w_gate, w_gate_t=w_gate_t,
                brow=brow, bcol=bcol, gmh=g_mhead[l].reshape(1, M_WIDTH), sinks=sinks, sink_rows=sink_rows,
                wa=w_a[l].astype(BF16), wb=w_b[l].astype(BF16), wo=w_out[l].astype(BF16),
                g_ffn=g_ffn[l].reshape(1, D_MODEL), wup=wup, wc=wc, wdn=wdn)


def _rope_tables(pos):
    half = A_HEAD_DIM // 2
    inv = ROPE_THETA ** (-jnp.arange(0, A_HEAD_DIM, 2, dtype=F32) / A_HEAD_DIM)
    ang = pos.astype(F32)[:, None] * inv[None, :]
    ang = jnp.tile(jnp.concatenate([ang, ang], axis=-1), (1, LANES_V7X // A_HEAD_DIM))
    sign = jnp.where((jnp.arange(LANES_V7X) % A_HEAD_DIM) < half, -1.0, 1.0).astype(F32)
    return jnp.cos(ang), jnp.sin(ang) * sign[None, :]


def _rope_tables_t(pos):
    inv = ROPE_THETA ** (-jnp.arange(0, A_HEAD_DIM, 2, dtype=F32) / A_HEAD_DIM)
    ang = inv[:, None] * pos.astype(F32)[None, :]
    return jnp.cos(ang), jnp.sin(ang)


def _prompt_trunk(x, layers, g_final):
    B, S, _ = x.shape
    n = B * S
    x2d = x.reshape(n, D_MODEL)
    cos, sin = _rope_tables(jnp.arange(S, dtype=jnp.int32))
    cos_t, sin_t = _rope_tables_t(jnp.arange(S, dtype=jnp.int32))
    nchunks = S // M_CHUNK
    c0 = jnp.zeros((B, 2, LANES_V7X, LANES_V7X), F32)
    n0 = jnp.zeros((B, 2, LANES_V7X), F32)
    m0 = jnp.zeros((B, SUBLANES_V7X, LANES_V7X), F32)
    halo0 = jnp.zeros((B, CONV_W - 1, 2 * D_FF), F32)
    outs = dict(C=[], n=[], m=[], k=[], v=[], conv=[])
    for l, p in enumerate(layers):
        mk, mo, ak, gt, mq_t, mv_t, aq_t, av_t, gc, gr = _inproj(
            x2d, p["g_mix"], p["w_main"], p["w_flex_t"], p["w_gate"], p["w_gate_t"],
            cos, sin, cos_t, sin_t, S // ROW_TILE, True)
        hm, c_new, n_new, m_new = _mlstm_t(mq_t, mk, mv_t, gc, gr, p["brow"], p["bcol"], c0, n0, m0,
                                           nseq=B, nchunks=nchunks)
        ha, v_last = _swa_prompt(aq_t, ak, av_t, p["sinks"], nseq=B, nblk=S // WINDOW)
        x2d = _merge(hm, mo, ha, gt, x2d, p["gmh"], p["wa"], p["wb"], p["wo"])
        x2d, conv = _ffn(x2d, p["g_ffn"], halo0, p["wup"], p["wc"], p["wdn"], g_final,
                         nseq=B, t=FFN_TILE, g=1, final_norm=(l == DEPTH - 1))
        c_new = jnp.transpose(c_new.reshape(B, M_HEADS // 2, M_V_DIM, 2, M_QK_DIM), (0, 1, 3, 4, 2))
        outs["C"].append(c_new.reshape(B, M_HEADS, M_QK_DIM, M_V_DIM))
        outs["n"].append(n_new.reshape(B, M_HEADS, M_QK_DIM))
        outs["m"].append(m_new[:, :M_HEADS, 0])
        outs["k"].append(ak.reshape(B, S, A_KV_HEADS, A_HEAD_DIM)[:, S - WINDOW:])
        outs["v"].append(v_last.reshape(B, WINDOW, A_KV_HEADS, A_HEAD_DIM))
        outs["conv"].append(conv)
    return (x2d.reshape(B, S, D_MODEL),) + tuple(jnp.stack(outs[k]) for k in ("C", "n", "m", "k", "v", "conv"))


def _sample_trunk(x, st_c, st_n, st_m, st_k, st_v, st_conv, layers, g_final):
    B, T, _ = x.shape
    P = SAMPLE_PAD_T
    bt = LANES_V7X // P
    gb = SAMPLE_FFN_GROUP
    ng = B // gb
    pos =PAST_LEN + (jnp.arange(B * P, dtype=jnp.int32) % P)
    cos, sin = _rope_tables(pos)
    c_all = st_c.reshape(DEPTH * B, 2, LANES_V7X, LANES_V7X)
    n_all = st_n.reshape(DEPTH * B, 2, LANES_V7X)
    k_all = st_k.reshape(DEPTH * B, WINDOW, KV_W)
    v_all = st_v.reshape(DEPTH * B, WINDOW, KV_W)
    xpad = jnp.pad(x, ((0, 0), (0, P - T), (0, 0))).reshape(B * P, D_MODEL)
    outs = dict(C=[], n=[], m=[], k=[], v=[], conv=[])
    y_bm = None
    for l, p in enumerate(layers):
        mk, mo, ak, gt, mq, mv, aq, av, gc, gr = _inproj(
            xpad, p["g_mix"], p["w_main"], p["w_flex"], p["w_gate"], p["w_gate_t"],
            cos, sin, cos, sin, (B * P) // ROW_TILE, False)
        m0 = jnp.broadcast_to(jnp.pad(st_m[l], ((0, 0), (0, SUBLANES_V7X - M_HEADS)))[:, :, None],
                              (B, SUBLANES_V7X, LANES_V7X))
        hm, c_new, n_new, m_new = _mlstm(mq, mk, mv, gc, gr, p["brow"], p["bcol"], c_all, n_all, m0,
                                         layer=l, nseq=B, nchunks=1, chunk=P, real=T, bt=bt, mm_dtype=F32)
        q5 = jnp.transpose(aq.reshape(B, P, A_KV_HEADS, A_GROUP, A_HEAD_DIM), (0, 2, 3, 1, 4))
        q5 = q5.reshape(B, A_KV_HEADS, A_GROUP * P, A_HEAD_DIM)
        zq = jnp.zeros_like(q5[:, 0])
        q_rows = jnp.concatenate([jnp.concatenate([q5[:, 0], zq], axis=-1),
                                  jnp.concatenate([zq, q5[:, 1]], axis=-1)], axis=1)
        o_rows, k_new, v_new = _swa_sample(q_rows, ak, av, k_all, v_all, p["sink_rows"],
                                           layer=l, nseq=B, bt=bt, real=T)
        o5 = jnp.stack([o_rows[:, :A_GROUP * P, :A_HEAD_DIM], o_rows[:, A_GROUP * P:, A_HEAD_DIM:]], axis=1)
        ha = jnp.transpose(o5.reshape(B, A_KV_HEADS, A_GROUP, P, A_HEAD_DIM),
                           (0, 3, 1, 2, 4)).reshape(B * P, A_WIDTH)
        xm = _merge(hm, mo, ha, gt, xpad, p["gmh"], p["wa"], p["wb"], p["wo"])
        x_tm = jnp.transpose(xm.reshape(ng, gb, P, D_MODEL)[:, :, :T], (0, 2, 1, 3)).reshape(T * B, D_MODEL)
        halo = jnp.transpose(st_conv[l].reshape(ng, gb, CONV_W - 1, 2 * D_FF),
                             (0, 2, 1, 3)).reshape(ng, (CONV_W - 1) * gb, 2 * D_FF)
        y_tm, conv = _ffn(x_tm, p["g_ffn"], halo, p["wup"], p["wc"], p["wdn"], g_final,
                          nseq=ng, t=T * gb, g=gb, final_norm=(l == DEPTH - 1))
        y_bm = jnp.transpose(y_tm.reshape(ng, T, gb, D_MODEL), (0, 2, 1, 3)).reshape(B, T, D_MODEL)
        xpad = jnp.pad(y_bm, ((0, 0), (0, P - T), (0, 0))).reshape(B * P, D_MODEL)
        outs["C"].append(c_new.reshape(B, M_HEADS, M_QK_DIM, M_V_DIM))
        outs["n"].append(n_new.reshape(B, M_HEADS, M_QK_DIM))
        outs["m"].append(m_new[:, :M_HEADS, 0])
        outs["k"].append(k_new.reshape(B, WINDOW, A_KV_HEADS, A_HEAD_DIM))
        outs["v"].append(v_new.reshape(B, WINDOW, A_KV_HEADS, A_HEAD_DIM))
        outs["conv"].append(jnp.transpose(conv.reshape(ng, CONV_W - 1, gb, 2 * D_FF),
                                          (0, 2, 1, 3)).reshape(B, CONV_W - 1, 2 * D_FF))
    return (y_bm,) + tuple(jnp.stack(outs[k]) for k in ("C", "n", "m", "k", "v", "conv"))


def kernel(x_prompt, x_sample, state_mlstm_C, state_mlstm_n, state_mlstm_m, cache_swa_k, cache_swa_v,
           state_ffn_conv, g_mix, w_in, b_igate, b_fgate, g_mhead, a_sinks, w_a, w_b, w_out, g_ffn,
           w_up, w_conv, b_conv, w_down, g_final):
    layers = [_prep_layer(l, g_mix, w_in, b_igate, b_fgate, g_mhead, a_sinks, w_a, w_b, w_out, g_ffn,
                          w_up, w_conv, b_conv, w_down) for l in range(DEPTH)]
    gfin = g_final.reshape(1, D_MODEL)
    yp, p_c, p_n, p_m, p_k, p_v, p_conv = _prompt_trunk(x_prompt, layers, gfin)
    ys, s_c, s_n, s_m, s_k, s_v, s_conv = _sample_trunk(
        x_sample, state_mlstm_C, state_mlstm_n, state_mlstm_m, cache_swa_k, cache_swa_v,
        state_ffn_conv, layers, gfin)
    return (yp, ys, p_c, p_n, p_m, p_k, p_v, p_conv, s_c, s_n, s_m, s_k, s_v, s_conv)
```
